```python
import functools
import jax, jax.numpy as jnp
from jax import lax
import numpy as np

D_MODEL = 2048
BATCH = 4
SEQ = 2048
DEPTH = 2
DEC_BATCH = 128
DEC_SEQ = 1
PAST_LEN = 2048
PAGE_SIZE = 128

D_MIX = D_MODEL
D_A = D_MIX // 2
DH_A = 128
H_A = D_A // DH_A
Q_BLOCK = 128
D_B = D_MIX // 4
DH_B = 64
H_B = D_B // DH_B
R_DECAY = 64
R_ICLR = 64
R_GATE = 128
N_B_COLS = 3 * D_B + R_DECAY + R_ICLR + R_GATE
D_C = D_MIX - D_A - D_B
NB_C = 8
BS_C = D_C // NB_C
CONV_W = 4
RG_C = 8.0
IN_SIZES = (D_A, D_A, D_A, H_A, N_B_COLS, D_C, D_C)
N_IN = sum(IN_SIZES)
N_MEM = 256
H_X = 4
DH_X = 128
D_X = H_X * DH_X
D_FF = 4 * D_MODEL
NORM_EPS = 1e-6
LNX_EPS = 64e-5
NEG_INF = -1e30

kernel_name = 'hymba_fox_rwkv7_rglru_decode_step'


def rmsnorm(x, g):
    x32 = x.astype(jnp.float32)
    y = x32 * lax.rsqrt(jnp.mean(x32 * x32, axis=-1, keepdims=True) + NORM_EPS)
    return (y * g.astype(jnp.float32)).astype(x.dtype)


def split_at(t, sizes):
    idx = np.cumsum(np.array(sizes))[:-1].tolist()
    return jnp.split(t, idx, axis=-1)


def fox_scores(q, k, c_q, c_k, q_pos, k_pos):
    s = jnp.einsum('nqhd,nkhd->nhqk', q, k).astype(jnp.float32) * (DH_A ** -0.5)
    s = s + jnp.swapaxes(c_q, 1, 2)[..., None] - jnp.swapaxes(c_k, 1, 2)[:, :, None, :]
    return jnp.where(k_pos[None, :] <= q_pos[:, None], s, NEG_INF)


def fox_prompt(q, k, v, logf):
    n, t, h, d = q.shape
    nb = t // Q_BLOCK
    c = jnp.cumsum(logf, axis=1)
    pos = jnp.arange(t)
    qb = jnp.moveaxis(q.reshape(n, nb, Q_BLOCK, h, d), 1, 0)
    cb = jnp.moveaxis(c.reshape(n, nb, Q_BLOCK, h), 1, 0)
    pb = pos.reshape(nb, Q_BLOCK)

    def block(args):
        q_i, c_i, p_i = args
        p = jax.nn.softmax(fox_scores(q_i, k, c_i, c, p_i, pos), axis=-1)
        return jnp.einsum('nhqk,nkhd->nqhd', p.astype(v.dtype), v)

    o = lax.map(block, (qb, cb, pb))
    return jnp.moveaxis(o, 0, 1).reshape(n, t, h * d)


def fox_sample(q, k, v, logf, past_k, past_v, past_logf):
    n, t, h, d = q.shape
    p_len = past_k.shape[1]
    c = jnp.cumsum(jnp.concatenate([past_logf.astype(jnp.float32), logf], axis=1), axis=1)
    c_past, c_new = c[:, :p_len], c[:, p_len:]
    q_pos = p_len + jnp.arange(t)
    s_past = fox_scores(q, past_k, c_new, c_past, q_pos, jnp.arange(p_len))
    s_new = fox_scores(q, k, c_new, c_new, q_pos, q_pos)
    p = jax.nn.softmax(jnp.concatenate([s_past, s_new], axis=-1), axis=-1).astype(v.dtype)
    o = (jnp.einsum('nhqk,nkhd->nqhd', p[..., :p_len], past_v)
         + jnp.einsum('nhqk,nkhd->nqhd', p[..., p_len:], v))
    return o.reshape(n, t, h * d)


def rwkv7_mix(zb, shift0, wkv0, P):
    f32 = jnp.float32
    n, t, _ = zb.shape
    z_prev = jnp.concatenate([shift0[:, None, :].astype(zb.dtype), zb[:, :-1]], axis=1)
    zs = zb + P['rw_mu'] * (z_prev - zb)
    r, k, v, zw, za, zg = split_at(zs, (D_B, D_B, D_B, R_DECAY, R_ICLR, R_GATE))
    w = -jax.nn.softplus(-(P['rw_w0'] + jnp.tanh(zw) @ P['rw_w2']).astype(f32)) - 0.5
    decay = jnp.exp(-jnp.exp(w))
    a = jax.nn.sigmoid((P['rw_a0'] + za @ P['rw_a2']).astype(f32))
    g = (jax.nn.sigmoid(zg) @ P['rw_g2']).astype(f32)
    hd = lambda u: u.astype(f32).reshape(n, t, H_B, DH_B)
    kk = hd(k * P['rw_kk'])
    kk = kk / jnp.maximum(jnp.sqrt(jnp.sum(kk * kk, axis=-1, keepdims=True)), 1e-12)
    k = hd(k.astype(f32) * (1.0 + (a - 1.0) * P['rw_ka'].astype(f32)))
    r, v, decay, a = hd(r), hd(v), hd(decay), hd(a)

    def step(S, inp):
        r_t, w_t, k_t, v_t, kk_t, a_t = inp
        sa = jnp.einsum('nhvk,nhk->nhv', S, -kk_t)
        S = (S * w_t[:, :, None, :] + sa[..., None] * (kk_t * a_t)[:, :, None, :]
             + v_t[..., None] * k_t[:, :, None, :])
        return S, jnp.einsum('nhvk,nhk->nhv', S, r_t)

    tm = lambda u: jnp.moveaxis(u, 1, 0)
    S, y = lax.scan(step, wkv0.astype(f32), (tm(r), tm(decay), tm(k), tm(v), tm(kk), tm(a)))
    y = tm(y)
    mu = jnp.mean(y, axis=-1, keepdims=True)
    var = jnp.mean(jnp.square(y - mu), axis=-1, keepdims=True)
    yn = (y - mu) * lax.rsqrt(var + LNX_EPS)
    yn = yn * P['rw_lnx_w'].reshape(H_B, DH_B) + P['rw_lnx_b'].reshape(H_B, DH_B)
    bonus = jnp.sum(r * k * P['rw_rk'], axis=-1, keepdims=True) * v
    out = ((yn + bonus).reshape(n, t, D_B) * g).astype(zb.dtype)
    return out, zb[:, -1], S.astype(zb.dtype)


def rglru_mix(xc, gc, conv0, h0, P):
    f32 = jnp.float32
    n, t, _ = xc.shape
    xpad = jnp.concatenate([conv0.astype(xc.dtype), xc], axis=1)
    u = P['rg_conv_b'] + sum(xpad[:, j:j + t] * P['rg_conv_w'][j] for j in range(CONV_W))
    ub = u.reshape(n, t, NB_C, BS_C)
    gate_a = jax.nn.sigmoid((jnp.einsum('ntbi,bij->ntbj', ub, P['rg_wa']).reshape(n, t, D_C) + P['rg_ba']).astype(f32))
    gate_x = jax.nn.sigmoid((jnp.einsum('ntbi,bij->ntbj', ub, P['rg_wx']).reshape(n, t, D_C) + P['rg_bx']).astype(f32))
    log_a = -RG_C * gate_a * jax.nn.softplus(-P['rg_lambda'].astype(f32))
    a = jnp.exp(log_a)
    b = u.astype(f32) * gate_x * jnp.sqrt(-jnp.expm1(2.0 * log_a))

    def step(h, inp):
        a_t, b_t = inp
        h = a_t * h + b_t
        return h, h

    h_last, hs = lax.scan(step, h0.astype(f32), (jnp.moveaxis(a, 1, 0), jnp.moveaxis(b, 1, 0)))
    y = jnp.moveaxis(hs, 0, 1) * jax.nn.gelu(gc.astype(f32))
    return y.astype(xc.dtype), xpad[:, t:], h_last.astype(xc.dtype)


def mem_kv(mem, P):
    n, m, _ = mem.shape
    mn = rmsnorm(mem, P['norm_mem'])
    return ((mn @ P['w_xk']).reshape(n, m, H_X, DH_X), (mn @ P['w_xv']).reshape(n, m, H_X, DH_X))


def cross_attn(xn, mk, mv, P):
    n, t, _ = xn.shape
    q = (xn @ P['w_xq']).reshape(n, t, H_X, DH_X)
    s = jnp.einsum('nthd,nmhd->nhtm', q, mk).astype(jnp.float32) * (DH_X ** -0.5)
    p = jax.nn.softmax(s, axis=-1).astype(mv.dtype)
    o = jnp.einsum('nhtm,nmhd->nthd', p, mv).reshape(n, t, D_X)
    return o @ P['w_xo']


def trunk_layer(x, mk, mv, fox_fn, shift0, wkv0, conv0, h0, P):
    n, t, _ = x.shape
    z = rmsnorm(x, P['norm_mix']) @ P['w_in']
    qa, ka, va, fa, zb, xc, gc = split_at(z, IN_SIZES)
    hd = lambda u: u.reshape(n, t, H_A, DH_A)
    logf = jax.nn.log_sigmoid((fa + P['fox_bf']).astype(jnp.float32))
    ka, va = hd(ka), hd(va)
    ya = fox_fn(hd(qa), ka, va, logf)
    yb, shift1, wkv1 = rwkv7_mix(zb, shift0, wkv0, P)
    yc, conv1, h1 = rglru_mix(xc, gc, conv0, h0, P)
    x = x + jnp.concatenate([ya, yb, yc], axis=-1) @ P['w_out']
    x = x + cross_attn(rmsnorm(x, P['norm_x']), mk, mv, P)
    hid = jax.nn.relu(rmsnorm(x, P['norm_ff']) @ P['w_ff1'])
    x = x + jnp.square(hid) @ P['w_ff2']
    return x, (ka, va, logf), shift1, wkv1, conv1, h1


def setup_inputs(seed: int = 0) -> dict:
    key = jax.random.key(seed)
    ks = iter(jax.random.split(key, 64))
    nrm = lambda shape, scale=1.0: scale * jax.random.normal(next(ks), shape, jnp.float32)
    L = DEPTH
    n_pages = PAST_LEN // PAGE_SIZE
    n_used = DEC_BATCH * n_pages
    n_phys = n_used + max(1, n_used // 4)
    page_table = jax.random.permutation(next(ks), n_phys)[:n_used].reshape(DEC_BATCH, n_pages).astype(jnp.int32)
    u = jax.random.uniform(next(ks), (L, D_C), jnp.float32, 0.9, 0.999)
    a_base = u ** (1.0 / RG_C)
    rg_lambda = jnp.log(a_base) - jnp.log1p(-a_base)
    return {
        'x_prompt': nrm((BATCH, SEQ, D_MODEL)),
        'x_sample': nrm((DEC_BATCH, DEC_SEQ, D_MODEL)),
        'cache_fox_k': nrm((L, n_phys, PAGE_SIZE, H_A, DH_A)),
        'cache_fox_v': nrm((L, n_phys, PAGE_SIZE, H_A, DH_A)),
        'cache_fox_logf': jax.nn.log_sigmoid(nrm((L, n_phys, PAGE_SIZE, H_A)) + 2.0),
        'state_rwkv_shift': nrm((L, DEC_BATCH, N_B_COLS)),
        'state_rwkv_wkv': nrm((L, DEC_BATCH, H_B, DH_B, DH_B), 0.3),
        'state_rglru_conv': nrm((L, DEC_BATCH, CONV_W - 1, D_C)),
        'state_rglru_h': nrm((L, DEC_BATCH, D_C), 0.5),
        'cache_mem_k': nrm((L, DEC_BATCH, N_MEM, H_X, DH_X)),
        'cache_mem_v': nrm((L, DEC_BATCH, N_MEM, H_X, DH_X)),
        'page_table': page_table,
        'mem_prompt': nrm((BATCH, N_MEM, D_MODEL)),
        'norm_mix': 1.0 + nrm((L, D_MODEL), 0.01),
        'w_in': nrm((L, D_MODEL, N_IN), D_MODEL ** -0.5),
        'fox_bf': 2.0 + nrm((L, H_A), 0.1),
        'rw_mu': jax.random.uniform(next(ks), (L, N_B_COLS), jnp.float32),
        'rw_w0': nrm((L, D_B), 0.5),
        'rw_w2': nrm((L, R_DECAY, D_B), R_DECAY ** -0.5),
        'rw_a0': nrm((L, D_B), 0.1),
        'rw_a2': nrm((L, R_ICLR, D_B), R_ICLR ** -0.5),
        'rw_g2': nrm((L, R_GATE, D_B), R_GATE ** -0.5),
        'rw_kk': 1.0 + nrm((L, D_B), 0.1),
        'rw_ka': 1.0 + nrm((L, D_B), 0.1),
        'rw_rk': nrm((L, H_B, DH_B), 0.1),
        'rw_lnx_w': 1.0 + nrm((L, D_B), 0.1),
        'rw_lnx_b': nrm((L, D_B), 0.01),
        'rg_conv_w': nrm((L, CONV_W, D_C), CONV_W ** -0.5),
        'rg_conv_b': nrm((L, D_C), 0.01),
        'rg_wa': nrm((L, NB_C, BS_C, BS_C), BS_C ** -0.5),
        'rg_ba': nrm((L, D_C), 0.01),
        'rg_wx': nrm((L, NB_C, BS_C, BS_C), BS_C ** -0.5),
        'rg_bx': nrm((L, D_C), 0.01),
        'rg_lambda': rg_lambda,
        'w_out': nrm((L, D_MIX, D_MODEL), D_MIX ** -0.5),
        'norm_x': 1.0 + nrm((L, D_MODEL), 0.01),
        'norm_mem': 1.0 + nrm((L, D_MODEL), 0.01),
        'w_xq': nrm((L, D_MODEL, D_X), D_MODEL ** -0.5),
        'w_xk': nrm((L, D_MODEL, D_X), D_MODEL ** -0.5),
        'w_xv': nrm((L, D_MODEL, D_X), D_MODEL ** -0.5),
        'w_xo': nrm((L, D_X, D_MODEL), D_X ** -0.5),
        'norm_ff': 1.0 + nrm((L, D_MODEL), 0.01),
        'w_ff1': nrm((L, D_MODEL, D_FF), D_MODEL ** -0.5),
        'w_ff2': nrm((L, D_FF, D_MODEL), D_FF ** -0.5),
        'norm_f': 1.0 + nrm((D_MODEL,), 0.01),
    }


def reference(x_prompt, x_sample, cache_fox_k, cache_fox_v, cache_fox_logf, state_rwkv_shift, state_rwkv_wkv, state_rglru_conv, state_rglru_h, cache_mem_k, cache_mem_v, page_table, mem_prompt, norm_mix, w_in, fox_bf, rw_mu, rw_w0, rw_w2, rw_a0, rw_a2, rw_g2, rw_kk, rw_ka, rw_rk, rw_lnx_w, rw_lnx_b, rg_conv_w, rg_conv_b, rg_wa, rg_ba, rg_wx, rg_bx, rg_lambda, w_out, norm_x, norm_mem, w_xq, w_xk, w_xv, w_xo, norm_ff, w_ff1, w_ff2, norm_f):
    nb_p = x_prompt.shape[0]
    nb_s, n_pages = page_table.shape
    p_len = n_pages * PAGE_SIZE
    dt = x_prompt.dtype
    xp, xs = x_prompt, x_sample
    prompt_states, sample_states = [], []
    for l in range(DEPTH):
        P = dict(norm_mix=norm_mix[l], w_in=w_in[l], fox_bf=fox_bf[l], rw_mu=rw_mu[l], rw_w0=rw_w0[l],
                 rw_w2=rw_w2[l], rw_a0=rw_a0[l], rw_a2=rw_a2[l], rw_g2=rw_g2[l], rw_kk=rw_kk[l],
                 rw_ka=rw_ka[l], rw_rk=rw_rk[l], rw_lnx_w=rw_lnx_w[l], rw_lnx_b=rw_lnx_b[l],
                 rg_conv_w=rg_conv_w[l], rg_conv_b=rg_conv_b[l], rg_wa=rg_wa[l], rg_ba=rg_ba[l],
                 rg_wx=rg_wx[l], rg_bx=rg_bx[l], rg_lambda=rg_lambda[l], w_out=w_out[l],
                 norm_x=norm_x[l], norm_mem=norm_mem[l], w_xq=w_xq[l], w_xk=w_xk[l], w_xv=w_xv[l],
                 w_xo=w_xo[l], norm_ff=norm_ff[l], w_ff1=w_ff1[l], w_ff2=w_ff2[l])
        mk_p, mv_p = mem_kv(mem_prompt, P)
        xp, kvf, sh, wkv, conv, h = trunk_layer(
            xp, mk_p, mv_p, fox_prompt,
            jnp.zeros((nb_p, N_B_COLS), dt), jnp.zeros((nb_p, H_B, DH_B, DH_B), dt),
            jnp.zeros((nb_p, CONV_W - 1, D_C), dt), jnp.zeros((nb_p, D_C), dt), P)
        prompt_states.append(kvf + (sh, wkv, conv, h, mk_p, mv_p))
        past_k = cache_fox_k[l][page_table].reshape(nb_s, p_len, H_A, DH_A)
        past_v = cache_fox_v[l][page_table].reshape(nb_s, p_len, H_A, DH_A)
        past_f = cache_fox_logf[l][page_table].reshape(nb_s, p_len, H_A)
        fox_s = functools.partial(fox_sample, past_k=past_k, past_v=past_v, past_logf=past_f)
        xs, kvf, sh, wkv, conv, h = trunk_layer(
            xs, cache_mem_k[l], cache_mem_v[l], fox_s,
            state_rwkv_shift[l], state_rwkv_wkv[l], state_rglru_conv[l], state_rglru_h[l], P)
        sample_states.append(kvf + (sh, wkv, conv, h))
    (fox_k_p, fox_v_p, fox_logf_p, rwkv_shift_p, rwkv_wkv_p, rglru_conv_p, rglru_h_p,
     mem_k_p, mem_v_p) = [jnp.stack(s) for s in zip(*prompt_states)]
    (fox_k_s, fox_v_s, fox_logf_s, rwkv_shift_s, rwkv_wkv_s, rglru_conv_s,
     rglru_h_s) = [jnp.stack(s) for s in zip(*sample_states)]
    y_prompt = rmsnorm(xp, norm_f)
    y_sample = rmsnorm(xs, norm_f)
    return (y_prompt, y_sample, fox_k_p, fox_v_p, fox_logf_p, rwkv_shift_p, rwkv_wkv_p, rglru_conv_p, rglru_h_p, mem_k_p, mem_v_p, fox_k_s, fox_v_s, fox_logf_s, rwkv_shift_s, rwkv_wkv_s, rglru_conv_s, rglru_h_s)
```

```python
import functools

import numpy as np
import jax
import jax.numpy as jnp
from jax import lax
from jax.experimental import pallas as pl
from jax.experimental.pallas import tpu as pltpu

F32 = jnp.float32
BF16 = jnp.bfloat16

NORM_EPS = 1e-6
LNX_EPS = 64e-5
RG_C = 8.0
NEG_INF = -1e30

LANES = 128
SUBLANES = 8
VMEM_CAP_BYTES = 60000 * 1024
RWKV_CHUNK = 64


def _cparams(semantics, est_bytes):
    limit = int(min(max(2 * est_bytes + (8 << 20), 24 << 20), VMEM_CAP_BYTES))
    return pltpu.CompilerParams(dimension_semantics=semantics, vmem_limit_bytes=limit)


def _split_bf16(x, parts):
    out = []
    r = x
    for i in range(parts):
        h = r.astype(BF16)
        out.append(h)
        if i + 1 < parts:
            r = r - h.astype(F32)
    return out


def _dot_nn(a, b):
    return jnp.dot(a, b, preferred_element_type=F32)


def _dot_nt(a, b):
    return lax.dot_general(a, b, (((1,), (1,)), ((), ())), preferred_element_type=F32)


def _dot_x_exact(x, w_exact, parts=3):
    return sum(_dot_nn(p, w_exact) for p in _split_bf16(x, parts))


def _dot_exact_x(w_exact, x, parts=3):
    return sum(_dot_nn(w_exact, p) for p in _split_bf16(x, parts))


def _dot3(a, b, nt=False):
    f = _dot_nt if nt else _dot_nn
    ah, al = _split_bf16(a, 2)
    bh, bl = _split_bf16(b, 2)
    return f(ah, bh) + f(ah, bl) + f(al, bh)


def _softplus(x):
    return jnp.maximum(x, 0.0) + jnp.log1p(jnp.exp(-jnp.abs(x)))


def _log_sigmoid(x):
    return -_softplus(-x)


def _gelu_tanh(x):
    c = np.float32(np.sqrt(2.0 / np.pi))
    return 0.5 * x * (1.0 + jnp.tanh(c * (x + 0.044715 * (x * x * x))))


def _mm_kernel(*refs, n_seg, has_gain, has_res, epilogue):
    xs = refs[:n_seg]
    ws = refs[n_seg:2 * n_seg]
    pos = 2 * n_seg
    g_ref = res_ref = None
    if has_gain:
        g_ref = refs[pos]
        pos += 1
    if has_res:
        res_ref = refs[pos]
        pos += 1
    o_ref = refs[pos]
    if has_gain:
        xn_ref = refs[pos + 1]

        @pl.when(pl.program_id(1) == 0)
        def _():
            x = xs[0][...]
            ms = jnp.mean(x * x, axis=-1, keepdims=True)
            xn_ref[...] = (x * lax.rsqrt(ms + NORM_EPS) * g_ref[...]).astype(BF16)

        acc = _dot_nn(xn_ref[...], ws[0][...])
    else:
        acc = _dot_nn(xs[0][...], ws[0][...])
        for x_ref, w_ref in zip(xs[1:], ws[1:]):
            acc = acc + _dot_nn(x_ref[...], w_ref[...])
    if epilogue == "relu2":
        r = jnp.maximum(acc, 0.0)
        acc = r * r
    if has_res:
        acc = acc + res_ref[...]
    o_ref[...] = acc.astype(o_ref.dtype)


def _matmul(xs, ws, *, gain=None, residual=None, epilogue="none", out_dtype=F32, tm=512, tn=512, name="mm"):
    m = xs[0].shape[0]
    n = ws[0][0].shape[1]
    tm = min(tm, m)
    tn = min(tn, n)
    while n % tn:
        tn //= 2
    assert m % tm == 0 and tn % LANES == 0
    in_specs, args = [], []
    est = 0
    for x in xs:
        k = x.shape[1]
        in_specs.append(pl.BlockSpec((tm, k), lambda i, j: (i, 0)))
        args.append(x)
        est += 2 * tm * k * x.dtype.itemsize
    for (w, k, rb) in ws:
        in_specs.append(pl.BlockSpec((k, tn), lambda i, j, rb=rb: (rb, j)))
        args.append(w)
        est += 2 * k * tn * w.dtype.itemsize
    scratch = []
    if gain is not None:
        k = xs[0].shape[1]
        in_specs.append(pl.BlockSpec((1, k), lambda i, j: (0, 0)))
        args.append(gain.reshape(1, k).astype(F32))
        scratch.append(pltpu.VMEM((tm, k), BF16))
        est += tm * k * 2
    if residual is not None:
        in_specs.append(pl.BlockSpec((tm, tn), lambda i, j: (i, j)))
        args.append(residual)
        est += 2 * tm * tn * 4
    est += 3 * tm * tn * 4
    kern = functools.partial(_mm_kernel, n_seg=len(xs), has_gain=gain is not None,
                             has_res=residual is not None, epilogue=epilogue)
    return pl.pallas_call(
        kern,
        grid=(m // tm, n // tn),
        in_specs=in_specs,
        out_specs=pl.BlockSpec((tm, tn), lambda i, j: (i, j)),
        out_shape=jax.ShapeDtypeStruct((m, n), out_dtype),
        scratch_shapes=scratch,
        compiler_params=_cparams(("parallel", "arbitrary"), est),
        name=name,
    )(*args)


def _rmsnorm_kernel(x_ref, g_ref, o_ref):
    x = x_ref[...]
    ms = jnp.mean(x * x, axis=-1, keepdims=True)
    o_ref[...] = x * lax.rsqrt(ms + NORM_EPS) * g_ref[...]


def _rmsnorm(x, g, tm=512):
    m, d = x.shape
    tm = min(tm, m)
    return pl.pallas_call(
        _rmsnorm_kernel,
        grid=(m // tm,),
        in_specs=[pl.BlockSpec((tm, d), lambda i: (i, 0)), pl.BlockSpec((1, d), lambda i: (0, 0))],
        out_specs=pl.BlockSpec((tm, d), lambda i: (i, 0)),
        out_shape=jax.ShapeDtypeStruct((m, d), F32),
        compiler_params=_cparams(("parallel",), 4 * tm * d * 4),
        name="final_rmsnorm",
    )(x, g.reshape(1, d))


def _logf_kernel(fa_ref, bf_ref, lf_ref, c_ref, ct_ref, *, t, blk):
    row = lax.broadcasted_iota(jnp.int32, (blk, blk), 0)
    col = lax.broadcasted_iota(jnp.int32, (blk, blk), 1)
    tri = (col <= row).astype(BF16)
    carry = jnp.zeros((1, LANES), F32)
    for b in range(t // blk):
        sl = pl.ds(b * blk, blk)
        lf = _log_sigmoid(fa_ref[sl, :] + bf_ref[...])
        lf_ref[sl, :] = lf
        c = _dot_exact_x(tri, lf) + carry
        c_ref[sl, :] = c
        ct_ref[:, sl] = c.T
        carry = c[blk - 1:blk, :]


def _logf_cumsum(z, fa_blk, bf_pad, n, t):
    blk = min(t, 256)
    kern = functools.partial(_logf_kernel, t=t, blk=blk)
    return pl.pallas_call(
        kern,
        grid=(n,),
        in_specs=[pl.BlockSpec((t, LANES), lambda i: (i, fa_blk)),
                  pl.BlockSpec((1, LANES), lambda i: (0, 0))],
        out_specs=[pl.BlockSpec((t, LANES), lambda i: (i, 0)),
                   pl.BlockSpec((t, LANES), lambda i: (i, 0)),
                   pl.BlockSpec((None, LANES, t), lambda i: (i, 0, 0))],
        out_shape=[jax.ShapeDtypeStruct((n * t, LANES), F32),
                   jax.ShapeDtypeStruct((n * t, LANES), F32),
                   jax.ShapeDtypeStruct((n, LANES, t), F32)],
        compiler_params=_cparams(("parallel",), 10 * t * LANES * 4),
        name="logf_cumsum",
    )(z, bf_pad)


def _fox_attn_kernel(q_ref, k_ref, v_ref, cq_ref, ck_ref, o_ref, m_sc, l_sc, acc_sc, *, scale, tq, tk):
    h = pl.program_id(1)
    qi = pl.program_id(2)
    ki = pl.program_id(3)

    @pl.when(ki == 0)
    def _():
        m_sc[...] = jnp.full(m_sc.shape, NEG_INF, F32)
        l_sc[...] = jnp.zeros(l_sc.shape, F32)
        acc_sc[...] = jnp.zeros(acc_sc.shape, F32)

    @pl.when(ki <= qi)
    def _():
        q = q_ref[...].astype(BF16)
        k = k_ref[...].astype(BF16)
        s = _dot_nt(q, k) * scale
        lane = lax.broadcasted_iota(jnp.int32, (tq, LANES), 1)
        cq = jnp.sum(jnp.where(lane == h, cq_ref[...], 0.0), axis=1, keepdims=True)
        s = s + cq - ck_ref[...]
        qpos = qi * tq + lax.broadcasted_iota(jnp.int32, (tq, tk), 0)
        kpos = ki * tk + lax.broadcasted_iota(jnp.int32, (tq, tk), 1)
        s = jnp.where(kpos <= qpos, s, NEG_INF)
        m_old = m_sc[...]
        m_new = jnp.maximum(m_old, jnp.max(s, axis=1, keepdims=True))
        alpha = jnp.exp(m_old - m_new)
        p = jnp.exp(s - m_new)
        l_sc[...] = alpha * l_sc[...] + jnp.sum(p, axis=1, keepdims=True)
        acc_sc[...] = alpha * acc_sc[...] + _dot_nn(p.astype(BF16), v_ref[...].astype(BF16))
        m_sc[...] = m_new

    @pl.when(ki == qi)
    def _():
        o_ref[...] = (acc_sc[...] / l_sc[...]).astype(o_ref.dtype)


def _fox_prompt(z, c_col, c_row, *, n, t, n_heads, dh, q_blk0, tq=512):
    tq = min(tq, t)
    nq = t // tq
    kern = functools.partial(_fox_attn_kernel, scale=float(dh) ** -0.5, tq=tq, tk=tq)
    return pl.pallas_call(
        kern,
        grid=(n, n_heads, nq, nq),
        in_specs=[
            pl.BlockSpec((tq, dh), lambda b, h, qi, ki: (b * nq + qi, q_blk0 + h)),
            pl.BlockSpec((tq, dh), lambda b, h, qi, ki: (b * nq + jnp.minimum(ki, qi), q_blk0 + n_heads + h)),
            pl.BlockSpec((tq, dh), lambda b, h, qi, ki: (b * nq + jnp.minimum(ki, qi), q_blk0 + 2 * n_heads + h)),
            pl.BlockSpec((tq, LANES), lambda b, h, qi, ki: (b * nq + qi, 0)),
            pl.BlockSpec((None, 1, tq), lambda b, h, qi, ki: (b * n_heads + h, 0, jnp.minimum(ki, qi))),
        ],
        out_specs=pl.BlockSpec((tq, dh), lambda b, h, qi, ki: (b * nq + qi, h)),
        out_shape=jax.ShapeDtypeStruct((n * t, n_heads * dh), BF16),
        scratch_shapes=[pltpu.VMEM((tq, 1), F32), pltpu.VMEM((tq, 1), F32), pltpu.VMEM((tq, dh), F32)],
        compiler_params=_cparams(("parallel", "parallel", "parallel", "arbitrary"),
                                 8 * tq * dh * 4 + 6 * tq * tq * 4),
        name="fox_prompt_attn",
    )(z, z, z, c_col, c_row)


def _head_valid(n_heads, rows):
    lane = lax.broadcasted_iota(jnp.int32, (n_heads, rows), 1)
    sub = lax.broadcasted_iota(jnp.int32, (n_heads, rows), 0)
    return (lane & (n_heads - 1)) == sub


def _fox_decode_kernel(pt_ref, q_ref, kn_ref, vn_ref, lfn_ref, kc_ref, vc_ref, lf_ref, g_ref, o_ref,
                       m_sc, l_sc, acc_sc, car_sc, *, n_heads, scale):
    p = pl.program_id(1)

    @pl.when(p == 0)
    def _():
        m_sc[...] = jnp.full(m_sc.shape, NEG_INF, F32)
        l_sc[...] = jnp.zeros(l_sc.shape, F32)
        acc_sc[...] = jnp.zeros(acc_sc.shape, F32)
        car_sc[...] = jnp.zeros(car_sc.shape, F32)

    rows = kc_ref.shape[0]
    s = _dot_nt(q_ref[...].astype(BF16), kc_ref[...].astype(BF16)) * scale
    f_t = lf_ref[...]
    cum = _dot_x_exact(f_t, g_ref[...])
    car = car_sc[...]
    s = jnp.where(_head_valid(n_heads, rows), s - (car + cum), NEG_INF)
    m_old = m_sc[...]
    m_new = jnp.maximum(m_old, jnp.max(s, axis=1, keepdims=True))
    alpha = jnp.exp(m_old - m_new)
    pr = jnp.exp(s - m_new)
    l_new = alpha * l_sc[...] + jnp.sum(pr, axis=1, keepdims=True)
    acc_new = alpha * acc_sc[...] + _dot_nn(pr.astype(BF16), vc_ref[...].astype(BF16))
    car_new = car + jnp.sum(f_t, axis=1, keepdims=True)
    m_sc[...] = m_new
    l_sc[...] = l_new
    acc_sc[...] = acc_new
    car_sc[...] = car_new

    @pl.when(p == pl.num_programs(1) - 1)
    def _():
        s_new = jnp.sum(q_ref[...] * kn_ref[...], axis=1, keepdims=True) * scale - (car_new + lfn_ref[:, 0:1])
        m2 = jnp.maximum(m_new, s_new)
        a2 = jnp.exp(m_new - m2)
        pn = jnp.exp(s_new - m2)
        o_ref[...] = (a2 * acc_new + pn * vn_ref[...]) / (a2 * l_new + pn)


def _fox_decode(page_table, q, k_new, v_new, lf_new, kc, vc, lf_t, g_mat, *, layer, n_phys):
    b, n_heads, dh = q.shape
    n_pages = page_table.shape[1]
    rows = kc.shape[1]
    page = lf_t.shape[2]
    base = layer * n_phys
    tok = lambda i, p, pt: (i, 0, 0)
    pg = lambda i, p, pt: (base + pt[i, p], 0, 0)
    kern = functools.partial(_fox_decode_kernel, n_heads=n_heads, scale=float(dh) ** -0.5)
    return pl.pallas_call(
        kern,
        grid_spec=pltpu.PrefetchScalarGridSpec(
            num_scalar_prefetch=1,
            grid=(b, n_pages),
            in_specs=[
                pl.BlockSpec((None, n_heads, dh), tok),
                pl.BlockSpec((None, n_heads, dh), tok),
                pl.BlockSpec((None, n_heads, dh), tok),
                pl.BlockSpec((None, n_heads, dh), tok),
                pl.BlockSpec((None, rows, dh), pg),
                pl.BlockSpec((None, rows, dh), pg),
                pl.BlockSpec((None, n_heads, page), pg),
                pl.BlockSpec((page, rows), lambda i, p, pt: (0, 0)),
            ],
            out_specs=pl.BlockSpec((None, n_heads, dh), tok),
            scratch_shapes=[pltpu.VMEM((n_heads, 1), F32), pltpu.VMEM((n_heads, 1), F32),
                            pltpu.VMEM((n_heads, dh), F32), pltpu.VMEM((n_heads, 1), F32)],
        ),
        out_shape=jax.ShapeDtypeStruct((b, n_heads, dh), F32),
        compiler_params=_cparams(("parallel", "arbitrary"), 4 * rows * dh * 4 + 2 * page * rows * 2),
        name="fox_decode_attn",
    )(page_table, q, k_new, v_new, lf_new, kc, vc, lf_t, g_mat)


def _mem_decode_kernel(q_ref, k_ref, v_ref, o_ref, *, n_heads, scale):
    rows = k_ref.shape[0]
    s = _dot_nt(q_ref[...].astype(BF16), k_ref[...].astype(BF16)) * scale
    s = jnp.where(_head_valid(n_heads, rows), s, NEG_INF)
    m = jnp.max(s, axis=1, keepdims=True)
    p = jnp.exp(s - m)
    l = jnp.sum(p, axis=1, keepdims=True)
    o_ref[...] = (_dot_nn(p.astype(BF16), v_ref[...].astype(BF16)) / l).astype(o_ref.dtype)


def _mem_decode(q, k2, v2, *, layer):
    b, n_heads, dh = q.shape
    rows = k2.shape[1]
    base = layer * b
    kern = functools.partial(_mem_decode_kernel, n_heads=n_heads, scale=float(dh) ** -0.5)
    return pl.pallas_call(
        kern,
        grid=(b,),
        in_specs=[pl.BlockSpec((None, n_heads, dh), lambda i: (i, 0, 0)),
                  pl.BlockSpec((None, rows, dh), lambda i: (base + i, 0, 0)),
                  pl.BlockSpec((None, rows, dh), lambda i: (base + i, 0, 0))],
        out_specs=pl.BlockSpec((None, n_heads, dh), lambda i: (i, 0, 0)),
        out_shape=jax.ShapeDtypeStruct((b, n_heads, dh), BF16),
        compiler_params=_cparams(("parallel",), 4 * rows * dh * 4),
        name="mem_decode_attn",
    )(q, k2, v2)


def _mem_attn_kernel(q_ref, k_ref, v_ref, o_ref, *, scale):
    s = _dot_nt(q_ref[...].astype(BF16), k_ref[...].astype(BF16)) * scale
    m = jnp.max(s, axis=1, keepdims=True)
    p = jnp.exp(s - m)
    l = jnp.sum(p, axis=1, keepdims=True)
    o_ref[...] = (_dot_nn(p.astype(BF16), v_ref[...].astype(BF16)) / l).astype(o_ref.dtype)


def _mem_attn(q, kv, *, n, t, n_mem, n_heads, dh, tq=512):
    tq = min(tq, t)
    nq = t // tq
    kern = functools.partial(_mem_attn_kernel, scale=float(dh) ** -0.5)
    return pl.pallas_call(
        kern,
        grid=(n, n_heads, nq),
        in_specs=[pl.BlockSpec((tq, dh), lambda b, h, qi: (b * nq + qi, h)),
                  pl.BlockSpec((n_mem, dh), lambda b, h, qi: (b, h)),
                  pl.BlockSpec((n_mem, dh), lambda b, h, qi: (b, n_heads + h))],
        out_specs=pl.BlockSpec((tq, dh), lambda b, h, qi: (b * nq + qi, h)),
        out_shape=jax.ShapeDtypeStruct((n * t, n_heads * dh), BF16),
        compiler_params=_cparams(("parallel", "parallel", "parallel"), 8 * tq * dh * 4 + 4 * tq * n_mem * 4),
        name="mem_attn",
    )(q, kv, kv)


def _rglru_gates(u, wa_ref, wx_ref, ba_ref, bx_ref, lam_ref):
    ub = u.astype(BF16)
    gate_a = jax.nn.sigmoid(_dot_nn(ub, wa_ref[...]) + ba_ref[...])
    gate_x = jax.nn.sigmoid(_dot_nn(ub, wx_ref[...]) + bx_ref[...])
    log_a = -RG_C * gate_a * _softplus(-lam_ref[...])
    a = jnp.exp(log_a)
    b = u * gate_x * jnp.sqrt(1.0 - jnp.exp(2.0 * log_a))
    return a, b


def _shift_rows(x, d, fill):
    t = x.shape[0]
    if d % SUBLANES == 0:
        return jnp.concatenate([jnp.full((d, x.shape[1]), fill, x.dtype), x[:t - d]], axis=0)
    rolled = pltpu.roll(x, d, 0)
    row = lax.broadcasted_iota(jnp.int32, x.shape, 0)
    return jnp.where(row < d, fill, rolled)


def _rglru_seq_kernel(xc_ref, gc_ref, cw_ref, cb_ref, wa_ref, wx_ref, ba_ref, bx_ref, lam_ref,
                      y_ref, conv_ref, h_ref, xpad_sc, *, t, conv_w):
    xpad_sc[0:SUBLANES, :] = jnp.zeros((SUBLANES, LANES), F32)
    xpad_sc[SUBLANES:, :] = xc_ref[...]
    u = cb_ref[...] + jnp.zeros((t, LANES), F32)
    for j in range(conv_w):
        u = u + xpad_sc[pl.ds(SUBLANES - (conv_w - 1) + j, t), :] * cw_ref[j:j + 1, :]
    a, b = _rglru_gates(u, wa_ref, wx_ref, ba_ref, bx_ref, lam_ref)
    d = 1
    while d < t:
        a_sh = _shift_rows(a, d, 1.0)
        b_sh = _shift_rows(b, d, 0.0)
        b = a * b_sh + b
        a = a * a_sh
        d *= 2
    y_ref[...] = (b * _gelu_tanh(gc_ref[...])).astype(y_ref.dtype)
    h_ref[...] = b[t - 1:t, :]
    conv_ref[...] = xpad_sc[pl.ds(SUBLANES + t - (conv_w - 1), conv_w - 1), :]


def _rglru_seq(z, *, n, t, d_c, xc_blk0, conv_w_arr, conv_b, wa_d, wx_d, ba, bx, lam):
    nc = d_c // LANES
    conv_w = conv_w_arr.shape[0]
    vec = lambda a: a.reshape(1, d_c)
    vspec = pl.BlockSpec((1, LANES), lambda b, c: (0, c))
    kern = functools.partial(_rglru_seq_kernel, t=t, conv_w=conv_w)
    return pl.pallas_call(
        kern,
        grid=(n, nc),
        in_specs=[pl.BlockSpec((t, LANES), lambda b, c: (b, xc_blk0 + c)),
                  pl.BlockSpec((t, LANES), lambda b, c: (b, xc_blk0 + nc + c)),
                  pl.BlockSpec((conv_w, LANES), lambda b, c: (0, c)),
                  vspec,
                  pl.BlockSpec((LANES, LANES), lambda b, c: (c, c)),
                  pl.BlockSpec((LANES, LANES), lambda b, c: (c, c)),
                  vspec, vspec, vspec],
        out_specs=[pl.BlockSpec((t, LANES), lambda b, c: (b, c)),
                   pl.BlockSpec((None, conv_w - 1, LANES), lambda b, c: (b, 0, c)),
                   pl.BlockSpec((None, 1, LANES), lambda b, c: (b, 0, c))],
        out_shape=[jax.ShapeDtypeStruct((n * t, d_c), BF16),
                   jax.ShapeDtypeStruct((n, conv_w - 1, d_c), F32),
                   jax.ShapeDtypeStruct((n, 1, d_c), F32)],
        scratch_shapes=[pltpu.VMEM((t + SUBLANES, LANES), F32)],
        compiler_params=_cparams(("parallel", "parallel"), 16 * t * LANES * 4),
        name="rglru_seq",
    )(z, z, conv_w_arr, vec(conv_b), wa_d, wx_d, vec(ba), vec(bx), vec(lam))


def _rglru_step_kernel(xc_ref, gc_ref, c0_ref, h0_ref, cw_ref, cb_ref, wa_ref, wx_ref, ba_ref, bx_ref, lam_ref,
                       y_ref, conv_ref, h_ref, *, conv_w):
    xc = xc_ref[...]
    u = cb_ref[...] + xc * cw_ref[conv_w - 1:conv_w, :]
    for j in range(conv_w - 1):
        u = u + c0_ref[:, j, :] * cw_ref[j:j + 1, :]
    a, b = _rglru_gates(u, wa_ref, wx_ref, ba_ref, bx_ref, lam_ref)
    h = a * h0_ref[...] + b
    y_ref[...] = (h * _gelu_tanh(gc_ref[...])).astype(y_ref.dtype)
    h_ref[...] = h
    for j in range(conv_w - 2):
        conv_ref[:, j, :] = c0_ref[:, j + 1, :]
    conv_ref[:, conv_w - 2, :] = xc


def _rglru_step(z, conv0, h0, *, d_c, xc_blk0, conv_w_arr, conv_b, wa_d, wx_d, ba, bx, lam):
    bsz = z.shape[0]
    conv_w = conv_w_arr.shape[0]
    ncb = d_c // LANES
    vec = lambda a: a.reshape(1, d_c)
    full = lambda shape: pl.BlockSpec(shape, lambda i: (0,) * len(shape))
    kern = functools.partial(_rglru_step_kernel, conv_w=conv_w)
    return pl.pallas_call(
        kern,
        grid=(1,),
        in_specs=[pl.BlockSpec((bsz, d_c), lambda i: (0, xc_blk0 * LANES // d_c)),
                  pl.BlockSpec((bsz, d_c), lambda i: (0, xc_blk0 * LANES // d_c + 1)),
                  full((bsz, conv_w - 1, d_c)), full((bsz, d_c)), full((conv_w, d_c)), full((1, d_c)),
                  full((d_c, d_c)), full((d_c, d_c)), full((1, d_c)), full((1, d_c)), full((1, d_c))],
        out_specs=[full((bsz, d_c)), full((bsz, conv_w - 1, d_c)), full((bsz, d_c))],
        out_shape=[jax.ShapeDtypeStruct((bsz, d_c), BF16),
                   jax.ShapeDtypeStruct((bsz, conv_w - 1, d_c), F32),
                   jax.ShapeDtypeStruct((bsz, d_c), F32)],
        compiler_params=_cparams(("arbitrary",), 16 * bsz * d_c * 4 + 4 * d_c * d_c * 2),
        name="rglru_step",
    )(z, z, conv0, h0, conv_w_arr, vec(conv_b), wa_d, wx_d, vec(ba), vec(bx), vec(lam))


def _rwkv_prep_kernel(*refs, d_b, seq_mode, blocks_per_seq):
    if seq_mode:
        zb_ref, prev_ref = refs[:2]
    else:
        zb_ref, zp_ref = refs[:2]
    (mu_ref, w0_ref, a0_ref, w2_ref, a2_ref, g2_ref, kk_ref, ka_ref, ones_ref,
     r_ref, lw_ref, km_ref, v_ref, kn_ref, bt_ref, g_ref) = refs[2:18]
    zb = zb_ref[...]
    tm = zb.shape[0]
    if seq_mode:
        sh_sc = refs[18]
        first = (pl.program_id(0) % blocks_per_seq) == 0
        prev = jnp.where(first, 0.0, prev_ref[...])
        sh_sc[0:SUBLANES, :] = prev
        sh_sc[SUBLANES:, :] = zb
        zp = sh_sc[pl.ds(SUBLANES - 1, tm), :]
    else:
        zp = zp_ref[...]
    zs = zb + mu_ref[...] * (zp - zb)
    r = zs[:, 0:d_b]
    k = zs[:, d_b:2 * d_b]
    v = zs[:, 2 * d_b:3 * d_b]
    lr = zs[:, 3 * d_b:3 * d_b + w2_ref.shape[0]]
    w_lin = _dot_nn(jnp.tanh(lr).astype(BF16), w2_ref[...])
    a_lin = _dot_nn(lr.astype(BF16), a2_ref[...])
    g = _dot_nn(jax.nn.sigmoid(lr).astype(BF16), g2_ref[...])
    w = -_softplus(-(w0_ref[...] + w_lin)) - 0.5
    a = jax.nn.sigmoid(a0_ref[...] + a_lin)
    kk = k * kk_ref[...]
    nrm2 = _dot_x_exact(kk * kk, ones_ref[...], parts=2)
    kn = kk / jnp.maximum(jnp.sqrt(nrm2), 1e-12)
    r_ref[...] = r
    lw_ref[...] = -jnp.exp(w)
    km_ref[...] = k * (1.0 + (a - 1.0) * ka_ref[...])
    v_ref[...] = v
    kn_ref[...] = kn
    bt_ref[...] = kn * a
    g_ref[...] = g


def _rwkv_prep(z, zprev, *, zb_w, d_b, mu, w0, a0, w2p, a2p, g2p, kk, ka, ones_h, seq_len, tm=256):
    m = z.shape[0]
    seq_mode = zprev is None
    tm = min(tm, seq_len if seq_mode else m)
    assert m % tm == 0
    row = lambda i: (i, 0)
    cst = lambda i: (0, 0)
    if seq_mode:
        per8 = tm // SUBLANES
        second = pl.BlockSpec((SUBLANES, zb_w), lambda i: (jnp.maximum(i * per8 - 1, 0), 0))
        second_arg = z
        scratch = [pltpu.VMEM((tm + SUBLANES, zb_w), F32)]
        bps = seq_len // tm
    else:
        second = pl.BlockSpec((tm, zb_w), row)
        second_arg = zprev
        scratch = []
        bps = 1
    lrw = w2p.shape[0]
    kern = functools.partial(_rwkv_prep_kernel, d_b=d_b, seq_mode=seq_mode, blocks_per_seq=bps)
    out = jax.ShapeDtypeStruct((m, d_b), F32)
    return pl.pallas_call(
        kern,
        grid=(m // tm,),
        in_specs=[pl.BlockSpec((tm, zb_w), row), second,
                  pl.BlockSpec((1, zb_w), cst), pl.BlockSpec((1, d_b), cst), pl.BlockSpec((1, d_b), cst),
                  pl.BlockSpec((lrw, d_b), cst), pl.BlockSpec((lrw, d_b), cst), pl.BlockSpec((lrw, d_b), cst),
                  pl.BlockSpec((1, d_b), cst), pl.BlockSpec((1, d_b), cst), pl.BlockSpec((d_b, d_b), cst)],
        out_specs=[pl.BlockSpec((tm, d_b), row)] * 7,
        out_shape=[out] * 7,
        scratch_shapes=scratch,
        compiler_params=_cparams(("parallel",), 6 * tm * zb_w * 4 + 30 * tm * d_b * 4),
        name="rwkv_prep",
    )(z, second_arg, mu, w0, a0, w2p, a2p, g2p, kk, ka, ones_h)


def _rwkv_chunk_kernel(r_ref, lw_ref, km_ref, v_ref, kn_ref, bt_ref, y_ref, hout_ref, h_sc, *, c, dh):
    ci = pl.program_id(2)

    @pl.when(ci == 0)
    def _():
        h_sc[...] = jnp.zeros(h_sc.shape, F32)

    c2 = 2 * c
    lw = lw_ref[...]
    row = lax.broadcasted_iota(jnp.int32, (c, c), 0)
    col = lax.broadcasted_iota(jnp.int32, (c, c), 1)
    tri = (col <= row).astype(BF16)
    lc = _dot_exact_x(tri, lw)
    p = jnp.exp(lc)
    pinv = jnp.exp(-lc)
    abar = -kn_ref[...] * jnp.exp(lc - lw)
    rbar = r_ref[...] * p
    btt = bt_ref[...] * pinv
    kt = km_ref[...] * pinv
    v = v_ref[...]
    bk = jnp.concatenate([btt, kt], axis=0)
    h = h_sc[...]
    ah = _dot3(abar, h)
    rh = _dot3(rbar, h)
    lane = lax.broadcasted_iota(jnp.int32, (c, LANES), 1)
    t_idx = lax.broadcasted_iota(jnp.int32, (c, c2), 0)
    j_idx = lax.broadcasted_iota(jnp.int32, (c, c2), 1) & (c - 1)
    zero_rows = jnp.zeros((c, c2), F32)
    us, ys = [], []
    for head in range(2):
        mx = (lane < dh) if head == 0 else (lane >= dh)
        ma = jnp.where(j_idx < t_idx, _dot3(jnp.where(mx, abar, 0.0), bk, nt=True), 0.0)
        p_top = ma
        u = ah + _dot3(p_top, jnp.concatenate([ah, v], axis=0))
        steps = int(np.log2(c2)) - 1
        for _ in range(steps):
            p_top = _dot3(p_top, jnp.concatenate([p_top, zero_rows], axis=0))
            u = u + _dot3(p_top, jnp.concatenate([u, v], axis=0))
        mr = jnp.where(j_idx <= t_idx, _dot3(jnp.where(mx, rbar, 0.0), bk, nt=True), 0.0)
        ys.append(rh + _dot3(mr, jnp.concatenate([u, v], axis=0)))
        us.append(u)
    ma_l = lane < dh
    u = jnp.where(ma_l, us[0], us[1])
    y_ref[...] = jnp.where(ma_l, ys[0], ys[1])
    pc = p[c - 1:c, :]
    r128 = lax.broadcasted_iota(jnp.int32, (LANES, LANES), 0)
    c128 = lax.broadcasted_iota(jnp.int32, (LANES, LANES), 1)
    dg = jnp.where(r128 == c128, jnp.broadcast_to(pc, (LANES, LANES)), 0.0)
    lhs_t = jnp.concatenate([btt * pc, kt * pc, dg], axis=0)
    rhs = jnp.concatenate([u, v, h], axis=0)
    h_new = _dot3(lhs_t.T, rhs)
    same_head = (r128 < dh) == (c128 < dh)
    h_new = jnp.where(same_head, h_new, 0.0)
    h_sc[...] = h_new

    @pl.when(ci == pl.num_programs(2) - 1)
    def _():
        hout_ref[...] = h_new


def _rwkv_chunk(r, lw, km, v, kn, bt, *, n, t, d_b, dh):
    c = RWKV_CHUNK
    assert t % c == 0 and 2 * dh == LANES and 2 * c == LANES
    npair = d_b // LANES
    nchunk = t // c
    tok = pl.BlockSpec((c, LANES), lambda b, pr, ci: (b * nchunk + ci, pr))
    kern = functools.partial(_rwkv_chunk_kernel, c=c, dh=dh)
    return pl.pallas_call(
        kern,
        grid=(n, npair, nchunk),
        in_specs=[tok] * 6,
        out_specs=[tok, pl.BlockSpec((None, None, LANES, LANES), lambda b, pr, ci: (b, pr, 0, 0))],
        out_shape=[jax.ShapeDtypeStruct((n * t, d_b), F32),
                   jax.ShapeDtypeStruct((n, npair, LANES, LANES), F32)],
        scratch_shapes=[pltpu.VMEM((LANES, LANES), F32)],
        compiler_params=_cparams(("parallel", "parallel", "arbitrary"), 64 * LANES * LANES * 4),
        name="rwkv_chunk",
    )(r, lw, km, v, kn, bt)


def _rwkv_step_kernel(r_ref, lw_ref, km_ref, v_ref, kn_ref, bt_ref, s_ref, ones_ref, y_ref, sout_ref,
                      *, bb, npair, dh):
    i2 = (lax.broadcasted_iota(jnp.int32, (dh, LANES), 1) & (dh - 1)) == \
        lax.broadcasted_iota(jnp.int32, (dh, LANES), 0)
    ones = ones_ref[...]
    for b in range(bb):
        for pr in range(npair):
            sl = slice(pr * LANES, (pr + 1) * LANES)
            rb = slice(b, b + 1)
            w_row = jnp.exp(lw_ref[rb, sl])
            s = s_ref[b, pr]
            sa = _dot_x_exact(s * (-kn_ref[rb, sl]), ones)
            vcol = _dot_x_exact(jnp.where(i2, v_ref[rb, sl], 0.0), ones)
            s_new = s * w_row + sa * bt_ref[rb, sl] + vcol * km_ref[rb, sl]
            yb = _dot_x_exact(s_new * r_ref[rb, sl], ones)
            y_ref[rb, sl] = jnp.sum(jnp.where(i2, yb, 0.0), axis=0, keepdims=True)
            sout_ref[b, pr] = s_new


def _rwkv_step(r, lw, km, v, kn, bt, s2, ones_pair, *, dh, bb=8):
    bsz, d_b = r.shape
    npair = d_b // LANES
    bb = min(bb, bsz)
    tok = pl.BlockSpec((bb, d_b), lambda i: (i, 0))
    st = pl.BlockSpec((bb, npair, dh, LANES), lambda i: (i, 0, 0, 0))
    kern = functools.partial(_rwkv_step_kernel, bb=bb, npair=npair, dh=dh)
    return pl.pallas_call(
        kern,
        grid=(bsz // bb,),
        in_specs=[tok] * 6 + [st, pl.BlockSpec((LANES, LANES), lambda i: (0, 0))],
        out_specs=[tok, st],
        out_shape=[jax.ShapeDtypeStruct((bsz, d_b), F32),
                   jax.ShapeDtypeStruct((bsz, npair, dh, LANES), F32)],
        compiler_params=_cparams(("parallel",), 8 * bb * npair * dh * LANES * 4),
        name="rwkv_step",
    )(r, lw, km, v, kn, bt, s2, ones_pair)


def _rwkv_post_kernel(y_ref, r_ref, km_ref, v_ref, g_ref, lw_ref, lb_ref, rk_ref, ones_ref, o_ref, *, dh):
    ones = ones_ref[...]
    y = y_ref[...]
    inv = 1.0 / dh
    mu = _dot_x_exact(y, ones, parts=3) * inv
    d = y - mu
    var = _dot_x_exact(d * d, ones, parts=2) * inv
    yn = d * lax.rsqrt(var + LNX_EPS) * lw_ref[...] + lb_ref[...]
    bonus = _dot_x_exact(r_ref[...] * km_ref[...] * rk_ref[...], ones, parts=3) * v_ref[...]
    o_ref[...] = ((yn + bonus) * g_ref[...]).astype(o_ref.dtype)


def _rwkv_post(y, r, km, v, g, *, lnx_w, lnx_b, rk, ones_h, dh, tm=256):
    m, d_b = y.shape
    tm = min(tm, m)
    row = pl.BlockSpec((tm, d_b), lambda i: (i, 0))
    vec = pl.BlockSpec((1, d_b), lambda i: (0, 0))
    kern = functools.partial(_rwkv_post_kernel, dh=dh)
    return pl.pallas_call(
        kern,
        grid=(m // tm,),
        in_specs=[row] * 5 + [vec] * 3 + [pl.BlockSpec((d_b, d_b), lambda i: (0, 0))],
        out_specs=row,
        out_shape=jax.ShapeDtypeStruct((m, d_b), BF16),
        compiler_params=_cparams(("parallel",), 30 * tm * d_b * 4),
        name="rwkv_post",
    )(y, r, km, v, g, lnx_w, lnx_b, rk, ones_h)


def _round_up(x, m):
    return (x + m - 1) // m * m


def _block_ones(size, blk):
    idx = np.arange(size) // blk
    return jnp.asarray(idx[:, None] == idx[None, :], dtype=BF16)


def _block_diag(w):
    nb, bs, _ = w.shape
    eye = jnp.eye(nb, dtype=w.dtype)
    return (eye[:, None, :, None] * w[:, :, None, :]).reshape(nb * bs, nb * bs)


def kernel(x_prompt, x_sample, cache_fox_k, cache_fox_v, cache_fox_logf, state_rwkv_shift, state_rwkv_wkv, state_rglru_conv, state_rglru_h, cache_mem_k, cache_mem_v, page_table, mem_prompt, norm_mix, w_in, fox_bf, rw_mu, rw_w0, rw_w2, rw_a0, rw_a2, rw_g2, rw_kk, rw_ka, rw_rk, rw_lnx_w, rw_lnx_b, rg_conv_w, rg_conv_b, rg_wa, rg_ba, rg_wx, rg_bx, rg_lambda, w_out, norm_x, norm_mem, w_xq, w_xk, w_xv, w_xo, norm_ff, w_ff1, w_ff2, norm_f):
    nb_p, t_p, d_model = x_prompt.shape
    nb_s = x_sample.shape[0]
    depth, n_phys, page, h_a, dh_a = cache_fox_k.shape
    d_a = h_a * dh_a
    n_b_cols = state_rwkv_shift.shape[-1]
    _, _, h_b, dh_b, _ = state_rwkv_wkv.shape
    d_b = h_b * dh_b
    d_c = state_rglru_h.shape[-1]
    n_mem, h_x, dh_x = cache_mem_k.shape[2:]
    d_x = h_x * dh_x
    r_dec, r_icl, r_gate = rw_w2.shape[1], rw_a2.shape[1], rw_g2.shape[1]
    lr_w = r_dec + r_icl + r_gate
    zb_w = _round_up(n_b_cols + h_a, max(d_c, LANES))
    fa_blk = n_b_cols // LANES
    q_blk0 = zb_w // LANES
    xc_blk0 = (zb_w + 3 * d_a) // LANES
    assert n_b_cols % LANES == 0 and d_a % LANES == 0 and d_c % LANES == 0 and h_a <= LANES
    assert (zb_w + 3 * d_a) % d_c == 0 and n_b_cols == 3 * d_b + lr_w and (h_a & (h_a - 1)) == 0 and (h_x & (h_x - 1)) == 0

    splits = np.cumsum([d_a, d_a, d_a, h_a, n_b_cols, d_c])
    ones_h = _block_ones(d_b, dh_b)
    ones_pair = _block_ones(LANES, dh_b)
    g_mat = jnp.asarray(np.arange(page)[:, None] <= (np.arange(page * h_a)[None, :] // h_a), dtype=BF16)

    kc = cache_fox_k.reshape(depth * n_phys, page * h_a, dh_a)
    vc = cache_fox_v.reshape(depth * n_phys, page * h_a, dh_a)
    lf_t = jnp.swapaxes(cache_fox_logf, 2, 3).reshape(depth * n_phys, h_a, page)
    mk2 = cache_mem_k.reshape(depth * nb_s, n_mem * h_x, dh_x)
    mv2 = cache_mem_v.reshape(depth * nb_s, n_mem * h_x, dh_x)

    xp = x_prompt.reshape(nb_p * t_p, d_model)
    xs = x_sample.reshape(nb_s, d_model)
    memf = mem_prompt.reshape(nb_p * n_mem, d_model)
    row1 = lambda a: a.reshape(1, -1).astype(F32)
    pad_cols = lambda a, w: jnp.pad(a, ((0, 0), (0, w - a.shape[1])))

    p_states, s_states = [], []
    for l in range(depth):
        wq, wk, wv, wf, wzb, wxc, wgc = jnp.split(w_in[l], splits, axis=1)
        w_in_r = jnp.concatenate(
            [wzb, wf, jnp.zeros((d_model, zb_w - n_b_cols - h_a), F32), wq, wk, wv, wxc, wgc], axis=1).astype(BF16)
        bf_pad = pad_cols(row1(fox_bf[l]), LANES)
        mu_pad = pad_cols(row1(rw_mu[l]), zb_w)
        zrow = lambda r0, w, rows: jnp.pad(w, ((r0, lr_w - r0 - rows), (0, 0))).astype(BF16)
        w2p = zrow(0, rw_w2[l], r_dec)
        a2p = zrow(r_dec, rw_a2[l], r_icl)
        g2p = zrow(r_dec + r_icl, rw_g2[l], r_gate)
        wa_d = _block_diag(rg_wa[l]).astype(BF16)
        wx_d = _block_diag(rg_wx[l]).astype(BF16)
        w_out_b = w_out[l].astype(BF16)
        w_xq_b = w_xq[l].astype(BF16)
        w_xkv_b = jnp.concatenate([w_xk[l], w_xv[l]], axis=1).astype(BF16)
        w_xo_b = w_xo[l].astype(BF16)
        w_ff1_b = w_ff1[l].astype(BF16)
        w_ff2_b = w_ff2[l].astype(BF16)
        rwkv_par = dict(zb_w=zb_w, d_b=d_b, mu=mu_pad, w0=row1(rw_w0[l]), a0=row1(rw_a0[l]), w2p=w2p, a2p=a2p,
                        g2p=g2p, kk=row1(rw_kk[l]), ka=row1(rw_ka[l]), ones_h=ones_h)
        post_par = dict(lnx_w=row1(rw_lnx_w[l]), lnx_b=row1(rw_lnx_b[l]), rk=row1(rw_rk[l]), ones_h=ones_h, dh=dh_b)
        rg_par = dict(d_c=d_c, xc_blk0=xc_blk0, conv_w_arr=rg_conv_w[l], conv_b=rg_conv_b[l], wa_d=wa_d, wx_d=wx_d,
                      ba=rg_ba[l], bx=rg_bx[l], lam=rg_lambda[l])

        def tail(x, ya, yb, yc, attend, tm):
            segs = [(w_out_b, d_a, 0), (w_out_b, d_b, d_a // d_b), (w_out_b, d_c, (d_a + d_b) // d_c)]
            x = _matmul([ya, yb, yc], segs, residual=x, tm=tm, name="mix_out_proj")
            q = _matmul([x], [(w_xq_b, d_model, 0)], gain=norm_x[l], tm=tm, name="mem_q_proj")
            o = attend(q)
            x = _matmul([o], [(w_xo_b, d_x, 0)], residual=x, tm=tm, name="mem_out_proj")
            hid = _matmul([x], [(w_ff1_b, d_model, 0)], gain=norm_ff[l], epilogue="relu2", out_dtype=BF16, tm=tm,
                          name="ff_up")
            return _matmul([hid], [(w_ff2_b, w_ff2_b.shape[0], 0)], residual=x, tm=tm, name="ff_down")

        m_p = nb_p * t_p
        mkv = _matmul([memf], [(w_xkv_b, d_model, 0)], gain=norm_mem[l], tm=512, name="mem_kv_proj")
        z = _matmul([xp], [(w_in_r, d_model, 0)], gain=norm_mix[l], tm=1024, name="mix_in_proj")
        lf, c_col, c_t = _logf_cumsum(z, fa_blk, bf_pad, nb_p, t_p)
        c_row = c_t[:, :h_a, :].reshape(nb_p * h_a, 1, t_p)
        ya = _fox_prompt(z, c_col, c_row, n=nb_p, t=t_p, n_heads=h_a, dh=dh_a, q_blk0=q_blk0)
        r, lw, km, v, kn, bt, g = _rwkv_prep(z, None, seq_len=t_p, **rwkv_par)
        y, h_pair = _rwkv_chunk(r, lw, km, v, kn, bt, n=nb_p, t=t_p, d_b=d_b, dh=dh_b)
        yb = _rwkv_post(y, r, km, v, g, **post_par)
        yc, conv1, h1 = _rglru_seq(z, n=nb_p, t=t_p, **rg_par)
        xp = tail(xp, ya, yb, yc,
                  lambda q: _mem_attn(q, mkv, n=nb_p, t=t_p, n_mem=n_mem, n_heads=h_x, dh=dh_x), 512)
        z3 = z.reshape(nb_p, t_p, -1)
        hp = h_pair.reshape(nb_p, d_b // LANES, 2, dh_b, 2, dh_b)
        wkv = jnp.stack([hp[:, :, 0, :, 0, :], hp[:, :, 1, :, 1, :]], axis=2)
        wkv = jnp.swapaxes(wkv, -1, -2).reshape(nb_p, h_b, dh_b, dh_b)
        p_states.append((
            z3[:, :, zb_w + d_a:zb_w + 2 * d_a].reshape(nb_p, t_p, h_a, dh_a),
            z3[:, :, zb_w + 2 * d_a:zb_w + 3 * d_a].reshape(nb_p, t_p, h_a, dh_a),
            lf.reshape(nb_p, t_p, LANES)[:, :, :h_a],
            z3[:, t_p - 1, :n_b_cols],
            wkv,
            conv1,
            h1.reshape(nb_p, d_c),
            mkv[:, :d_x].reshape(nb_p, n_mem, h_x, dh_x),
            mkv[:, d_x:].reshape(nb_p, n_mem, h_x, dh_x),
        ))

        zs = _matmul([xs], [(w_in_r, d_model, 0)], gain=norm_mix[l], tm=nb_s, name="mix_in_proj_s")
        lf_s, _, _ = _logf_cumsum(zs, fa_blk, bf_pad, 1, nb_s)
        hd = lambda a: a.reshape(nb_s, h_a, dh_a)
        q_s = hd(zs[:, zb_w:zb_w + d_a])
        k_s = hd(zs[:, zb_w + d_a:zb_w + 2 * d_a])
        v_s = hd(zs[:, zb_w + 2 * d_a:zb_w + 3 * d_a])
        lfn = jnp.broadcast_to(lf_s[:, :h_a, None], (nb_s, h_a, dh_a))
        ya_s = _fox_decode(page_table, q_s, k_s, v_s, lfn, kc, vc, lf_t, g_mat, layer=l, n_phys=n_phys)
        ya_s = ya_s.reshape(nb_s, d_a).astype(BF16)
        zprev = pad_cols(state_rwkv_shift[l], zb_w)
        r, lw, km, v, kn, bt, g = _rwkv_prep(zs, zprev, seq_len=1, **rwkv_par)
        s2 = state_rwkv_wkv[l].reshape(nb_s, d_b // LANES, 2, dh_b, dh_b)
        s2 = jnp.swapaxes(s2, 2, 3).reshape(nb_s, d_b // LANES, dh_b, LANES)
        y, s2n = _rwkv_step(r, lw, km, v, kn, bt, s2, ones_pair, dh=dh_b)
        wkv_s = jnp.swapaxes(s2n.reshape(nb_s, d_b // LANES, dh_b, 2, dh_b), 2, 3).reshape(nb_s, h_b, dh_b, dh_b)
        yb_s = _rwkv_post(y, r, km, v, g, **post_par)
        yc_s, conv1_s, h1_s = _rglru_step(zs, state_rglru_conv[l], state_rglru_h[l], **rg_par)
        xs = tail(xs, ya_s, yb_s, yc_s,
                  lambda q: _mem_decode(q.reshape(nb_s, h_x, dh_x), mk2, mv2, layer=l).reshape(nb_s, d_x), nb_s)
        s_states.append((
            k_s.reshape(nb_s, 1, h_a, dh_a),
            v_s.reshape(nb_s, 1, h_a, dh_a),
            lf_s[:, :h_a].reshape(nb_s, 1, h_a),
            zs[:, :n_b_cols],
            wkv_s,
            conv1_s,
            h1_s,
        ))

    y_prompt = _rmsnorm(xp, norm_f).reshape(nb_p, t_p, d_model)
    y_sample = _rmsnorm(xs, norm_f).reshape(nb_s, 1, d_model)
    p_out = [jnp.stack(s) for s in zip(*p_states)]
    s_out = [jnp.stack(s) for s in zip(*s_states)]
    return (y_prompt, y_sample, *p_out, *s_out)
```

```python
import functools

import numpy as np
import jax
import jax.numpy as jnp
from jax import lax
from jax.experimental import pallas as pl
from jax.experimental.pallas import tpu as pltpu

F32 = jnp.float32
BF16 = jnp.bfloat16

NORM_EPS = 1e-6
LNX_EPS = 64e-5
RG_C = 8.0
NEG_INF = -1e30

LANES = 128
SUBLANES = 8
VMEM_CAP_BYTES = 60000 * 1024
RWKV_CHUNK = 64


def _cparams(semantics, est_bytes):
    limit = int(min(max(2 * est_bytes + (8 << 20), 24 << 20), VMEM_CAP_BYTES))
    return pltpu.CompilerParams(dimension_semantics=semantics, vmem_limit_bytes=limit)


def _split_bf16(x, parts):
    out = []
    r = x
    for i in range(parts):
        h = r.astype(BF16)
        out.append(h)
        if i + 1 < parts:
            r = r - h.astype(F32)
    return out


def _dot_nn(a, b):
    return jnp.dot(a, b, preferred_element_type=F32)


def _dot_nt(a, b):
    return lax.dot_general(a, b, (((1,), (1,)), ((), ())), preferred_element_type=F32)


def _dot_x_exact(x, w_exact, parts=3):
    return sum(_dot_nn(p, w_exact) for p in _split_bf16(x, parts))


def _dot_exact_x(w_exact, x, parts=3):
    return sum(_dot_nn(w_exact, p) for p in _split_bf16(x, parts))


def _dot3(a, b, nt=False):
    f = _dot_nt if nt else _dot_nn
    ah, al = _split_bf16(a, 2)
    bh, bl = _split_bf16(b, 2)
    return f(ah, bh) + f(ah, bl) + f(al, bh)


def _softplus(x):
    return jnp.maximum(x, 0.0) + jnp.log1p(jnp.exp(-jnp.abs(x)))


def _log_sigmoid(x):
    return -_softplus(-x)


def _gelu_tanh(x):
    c = np.float32(np.sqrt(2.0 / np.pi))
    return 0.5 * x * (1.0 + jnp.tanh(c * (x + 0.044715 * (x * x * x))))


def _mm_kernel(*refs, n_seg, has_gain, has_res, epilogue):
    xs = refs[:n_seg]
    ws = refs[n_seg:2 * n_seg]
    pos = 2 * n_seg
    g_ref = res_ref = None
    if has_gain:
        g_ref = refs[pos]
        pos += 1
    if has_res:
        res_ref = refs[pos]
        pos += 1
    o_ref = refs[pos]
    if has_gain:
        xn_ref = refs[pos + 1]

        @pl.when(pl.program_id(1) == 0)
        def _():
            x = xs[0][...]
            ms = jnp.mean(x * x, axis=-1, keepdims=True)
            xn_ref[...] = (x * lax.rsqrt(ms + NORM_EPS) * g_ref[...]).astype(BF16)

        acc = _dot_nn(xn_ref[...], ws[0][...])
    else:
        acc = _dot_nn(xs[0][...], ws[0][...])
        for x_ref, w_ref in zip(xs[1:], ws[1:]):
            acc = acc + _dot_nn(x_ref[...], w_ref[...])
    if epilogue == "relu2":
        r = jnp.maximum(acc, 0.0)
        acc = r * r
    if has_res:
        acc = acc + res_ref[...]
    o_ref[...] = acc.astype(o_ref.dtype)


def _matmul(xs, ws, *, gain=None, residual=None, epilogue="none", out_dtype=F32, tm=512, tn=512, name="mm"):
    m = xs[0].shape[0]
    n = ws[0][0].shape[1]
    tm = min(tm, m)
    tn = min(tn, n)
    while n % tn:
        tn //= 2
    assert m % tm == 0 and tn % LANES == 0
    in_specs, args = [], []
    est = 0
    for x in xs:
        k = x.shape[1]
        in_specs.append(pl.BlockSpec((tm, k), lambda i, j: (i, 0)))
        args.append(x)
        est += 2 * tm * k * x.dtype.itemsize
    for (w, k, rb) in ws:
        in_specs.append(pl.BlockSpec((k, tn), lambda i, j, rb=rb: (rb, j)))
        args.append(w)
        est += 2 * k * tn * w.dtype.itemsize
    scratch = []
    if gain is not None:
        k = xs[0].shape[1]
        in_specs.append(pl.BlockSpec((1, k), lambda i, j: (0, 0)))
        args.append(gain.reshape(1, k).astype(F32))
        scratch.append(pltpu.VMEM((tm, k), BF16))
        est += tm * k * 2
    if residual is not None:
        in_specs.append(pl.BlockSpec((tm, tn), lambda i, j: (i, j)))
        args.append(residual)
        est += 2 * tm * tn * 4
    est += 3 * tm * tn * 4
    kern = functools.partial(_mm_kernel, n_seg=len(xs), has_gain=gain is not None,
                             has_res=residual is not None, epilogue=epilogue)
    return pl.pallas_call(
        kern,
        grid=(m // tm, n // tn),
        in_specs=in_specs,
        out_specs=pl.BlockSpec((tm, tn), lambda i, j: (i, j)),
        out_shape=jax.ShapeDtypeStruct((m, n), out_dtype),
        scratch_shapes=scratch,
        compiler_params=_cparams(("parallel", "arbitrary"), est),
        name=name,
    )(*args)


def _rmsnorm_kernel(x_ref, g_ref, o_ref):
    x = x_ref[...]
    ms = jnp.mean(x * x, axis=-1, keepdims=True)
    o_ref[...] = x * lax.rsqrt(ms + NORM_EPS) * g_ref[...]


def _rmsnorm(x, g, tm=512):
    m, d = x.shape
    tm = min(tm, m)
    return pl.pallas_call(
        _rmsnorm_kernel,
        grid=(m // tm,),
        in_specs=[pl.BlockSpec((tm, d), lambda i: (i, 0)), pl.BlockSpec((1, d), lambda i: (0, 0))],
        out_specs=pl.BlockSpec((tm, d), lambda i: (i, 0)),
        out_shape=jax.ShapeDtypeStruct((m, d), F32),
        compiler_params=_cparams(("parallel",), 4 * tm * d * 4),
        name="final_rmsnorm",
    )(x, g.reshape(1, d))


def _logf_kernel(fa_ref, bf_ref, lf_ref, c_ref, ct_ref, *, t, blk):
    row = lax.broadcasted_iota(jnp.int32, (blk, blk), 0)
    col = lax.broadcasted_iota(jnp.int32, (blk, blk), 1)
    tri = (col <= row).astype(BF16)
    carry = jnp.zeros((1, LANES), F32)
    for b in range(t // blk):
        sl = pl.ds(b * blk, blk)
        lf = _log_sigmoid(fa_ref[sl, :] + bf_ref[...])
        lf_ref[sl, :] = lf
        c = _dot_exact_x(tri, lf) + carry
        c_ref[sl, :] = c
        ct_ref[:, sl] = c.T
        carry = c[blk - 1:blk, :]


def _logf_cumsum(z, fa_blk, bf_pad, n, t):
    blk = min(t, 256)
    kern = functools.partial(_logf_kernel, t=t, blk=blk)
    return pl.pallas_call(
        kern,
        grid=(n,),
        in_specs=[pl.BlockSpec((t, LANES), lambda i: (i, fa_blk)),
                  pl.BlockSpec((1, LANES), lambda i: (0, 0))],
        out_specs=[pl.BlockSpec((t, LANES), lambda i: (i, 0)),
                   pl.BlockSpec((t, LANES), lambda i: (i, 0)),
                   pl.BlockSpec((None, LANES, t), lambda i: (i, 0, 0))],
        out_shape=[jax.ShapeDtypeStruct((n * t, LANES), F32),
                   jax.ShapeDtypeStruct((n * t, LANES), F32),
                   jax.ShapeDtypeStruct((n, LANES, t), F32)],
        compiler_params=_cparams(("parallel",), 10 * t * LANES * 4),
        name="logf_cumsum",
    )(z, bf_pad)


def _fox_attn_kernel(q_ref, k_ref, v_ref, cq_ref, ck_ref, o_ref, m_sc, l_sc, acc_sc, *, scale, tq, dh, hb):
    hg = pl.program_id(1)
    qi = pl.program_id(2)
    m_sc[...] = jnp.full(m_sc.shape, NEG_INF, F32)
    l_sc[...] = jnp.zeros(l_sc.shape, F32)
    acc_sc[...] = jnp.zeros(acc_sc.shape, F32)
    lane = lax.broadcasted_iota(jnp.int32, (tq, LANES), 1)
    cols = [slice(j * dh, (j + 1) * dh) for j in range(hb)]
    qs = [q_ref[:, cols[j]].astype(BF16) for j in range(hb)]
    cqs = [jnp.sum(jnp.where(lane == hg * hb + j, cq_ref[...], 0.0), axis=1, keepdims=True) for j in range(hb)]

    def block(ki, diagonal):
        rows = pl.ds(pl.multiple_of(ki * tq, tq), tq)
        ss = [_dot_nt(qs[j], k_ref[rows, cols[j]].astype(BF16)) for j in range(hb)]
        ps, alphas = [], []
        for j in range(hb):
            s = ss[j] * scale + cqs[j] - ck_ref[j, ki]
            if diagonal:
                causal = (lax.broadcasted_iota(jnp.int32, (tq, tq), 1)
                          <= lax.broadcasted_iota(jnp.int32, (tq, tq), 0))
                s = jnp.where(causal, s, NEG_INF)
            m_old = m_sc[j]
            m_new = jnp.maximum(m_old, jnp.max(s, axis=1, keepdims=True))
            alpha = jnp.exp(m_old - m_new)
            p = jnp.exp(s - m_new)
            l_sc[j] = alpha * l_sc[j] + jnp.sum(p, axis=1, keepdims=True)
            m_sc[j] = m_new
            ps.append(p.astype(BF16))
            alphas.append(alpha)
        pvs = [_dot_nn(ps[j], v_ref[rows, cols[j]].astype(BF16)) for j in range(hb)]
        for j in range(hb):
            acc_sc[j] = alphas[j] * acc_sc[j] + pvs[j]

    def body(ki, carry):
        block(ki, False)
        return carry

    lax.fori_loop(0, qi, body, 0)
    block(qi, True)
    for j in range(hb):
        o_ref[:, cols[j]] = (acc_sc[j] / l_sc[j]).astype(o_ref.dtype)


def _fox_prompt(z, c_col, c_row, *, n, t, n_heads, dh, q_blk0, tq=512, hb=4):
    tq = min(tq, t)
    nq = t // tq
    hb = min(hb, n_heads)
    assert n_heads % hb == 0 and q_blk0 * LANES % (hb * dh) == 0
    ng = n_heads // hb
    g0 = q_blk0 * LANES // (hb * dh)
    kern = functools.partial(_fox_attn_kernel, scale=float(dh) ** -0.5, tq=tq, dh=dh, hb=hb)
    return pl.pallas_call(
        kern,
        grid=(n, ng, nq),
        in_specs=[
            pl.BlockSpec((tq, hb * dh), lambda b, g, qi: (b * nq + qi, g0 + g)),
            pl.BlockSpec((t, hb * dh), lambda b, g, qi: (b, g0 + ng + g)),
            pl.BlockSpec((t, hb * dh), lambda b, g, qi: (b, g0 + 2 * ng + g)),
            pl.BlockSpec((tq, LANES), lambda b, g, qi: (b * nq + qi, 0)),
            pl.BlockSpec((hb, nq, 1, tq), lambda b, g, qi: (b * ng + g, 0, 0, 0)),
        ],
        out_specs=pl.BlockSpec((tq, hb * dh), lambda b, g, qi: (b * nq + qi, g)),
        out_shape=jax.ShapeDtypeStruct((n * t, n_heads * dh), BF16),
        scratch_shapes=[pltpu.VMEM((hb, tq, 1), F32), pltpu.VMEM((hb, tq, 1), F32),
                        pltpu.VMEM((hb, tq, dh), F32)],
        compiler_params=_cparams(("parallel", "parallel", "arbitrary"),
                                 hb * (4 * t * dh * 4 + 8 * tq * dh * 4 + 6 * tq * tq * 4)),
        name="fox_prompt_attn",
    )(z, z, z, c_col, c_row.reshape(n * n_heads, nq, 1, tq))


def _head_valid(n_heads, rows):
    lane = lax.broadcasted_iota(jnp.int32, (n_heads, rows), 1)
    sub = lax.broadcasted_iota(jnp.int32, (n_heads, rows), 0)
    return (lane & (n_heads - 1)) == sub


def _fox_decode_kernel(pt_ref, q_ref, kn_ref, vn_ref, lfn_ref, *rest, n_heads, scale, g):
    kcs, vcs, lfs = rest[0:g], rest[g:2 * g], rest[2 * g:3 * g]
    g_ref, o_ref, m_sc, l_sc, acc_sc, car_sc = rest[3 * g:]
    p = pl.program_id(1)

    @pl.when(p == 0)
    def _():
        m_sc[...] = jnp.full(m_sc.shape, NEG_INF, F32)
        l_sc[...] = jnp.zeros(l_sc.shape, F32)
        acc_sc[...] = jnp.zeros(acc_sc.shape, F32)
        car_sc[...] = jnp.zeros(car_sc.shape, F32)

    rows = kcs[0].shape[0]
    qb = q_ref[...].astype(BF16)
    valid = _head_valid(n_heads, rows)
    f_all = jnp.concatenate([lf[...] for lf in lfs], axis=0)
    cum_all = _dot_x_exact(f_all, g_ref[...])
    car = car_sc[...]
    ss = []
    for j in range(g):
        s = _dot_nt(qb, kcs[j][...].astype(BF16)) * scale
        ss.append(jnp.where(valid, s - (car + cum_all[j * n_heads:(j + 1) * n_heads]), NEG_INF))
        car = car + jnp.sum(lfs[j][...], axis=1, keepdims=True)
    car_new = car
    m_old = m_sc[...]
    m_new = m_old
    for s in ss:
        m_new = jnp.maximum(m_new, jnp.max(s, axis=1, keepdims=True))
    alpha = jnp.exp(m_old - m_new)
    l_new = alpha * l_sc[...]
    acc_new = alpha * acc_sc[...]
    for j in range(g):
        pr = jnp.exp(ss[j] - m_new)
        l_new = l_new + jnp.sum(pr, axis=1, keepdims=True)
        acc_new = acc_new + _dot_nn(pr.astype(BF16), vcs[j][...].astype(BF16))
    m_sc[...] = m_new
    l_sc[...] = l_new
    acc_sc[...] = acc_new
    car_sc[...] = car_new

    @pl.when(p == pl.num_programs(1) - 1)
    def _():
        s_new = jnp.sum(q_ref[...] * kn_ref[...], axis=1, keepdims=True) * scale - (car_new + lfn_ref[:, 0:1])
        m2 = jnp.maximum(m_new, s_new)
        a2 = jnp.exp(m_new - m2)
        pn = jnp.exp(s_new - m2)
        o_ref[...] = (a2 * acc_new + pn * vn_ref[...]) / (a2 * l_new + pn)


def _fox_decode(page_table, q, k_new, v_new, lf_new, kc, vc, lf_t, g_mat, *, layer, n_phys, g=4):
    b, n_heads, dh = q.shape
    n_pages = page_table.shape[1]
    g = min(g, n_pages)
    assert n_pages % g == 0
    rows = kc.shape[1]
    page = lf_t.shape[2]
    base = layer * n_phys
    tok = pl.BlockSpec((None, n_heads, dh), lambda i, p, pt: (i, 0, 0))
    pg = lambda j: (lambda i, p, pt: (base + pt[i, p * g + j], 0, 0))
    kern = functools.partial(_fox_decode_kernel, n_heads=n_heads, scale=float(dh) ** -0.5, g=g)
    return pl.pallas_call(
        kern,
        grid_spec=pltpu.PrefetchScalarGridSpec(
            num_scalar_prefetch=1,
            grid=(b, n_pages // g),
            in_specs=([tok] * 4
                      + [pl.BlockSpec((None, rows, dh), pg(j)) for j in range(g)]
                      + [pl.BlockSpec((None, rows, dh), pg(j)) for j in range(g)]
                      + [pl.BlockSpec((None, n_heads, page), pg(j)) for j in range(g)]
                      + [pl.BlockSpec((page, rows), lambda i, p, pt: (0, 0))]),
            out_specs=tok,
            scratch_shapes=[pltpu.VMEM((n_heads, 1), F32), pltpu.VMEM((n_heads, 1), F32),
                            pltpu.VMEM((n_heads, dh), F32), pltpu.VMEM((n_heads, 1), F32)],
        ),
        out_shape=jax.ShapeDtypeStruct((b, n_heads, dh), F32),
        compiler_params=_cparams(("parallel", "arbitrary"), g * 4 * rows * dh * 4 + 2 * page * rows * 2),
        name="fox_decode_attn",
    )(page_table, q, k_new, v_new, lf_new, *([kc] * g), *([vc] * g), *([lf_t] * g), g_mat)


def _mem_decode_kernel(q_ref, k_ref, v_ref, o_ref, *, n_heads, scale):
    rows = k_ref.shape[0]
    s = _dot_nt(q_ref[...].astype(BF16), k_ref[...].astype(BF16)) * scale
    s = jnp.where(_head_valid(n_heads, rows), s, NEG_INF)
    m = jnp.max(s, axis=1, keepdims=True)
    p = jnp.exp(s - m)
    l = jnp.sum(p, axis=1, keepdims=True)
    o_ref[...] = (_dot_nn(p.astype(BF16), v_ref[...].astype(BF16)) / l).astype(o_ref.dtype)


def _mem_decode(q, k2, v2, *, layer):
    b, n_heads, dh = q.shape
    rows = k2.shape[1]
    base = layer * b
    kern = functools.partial(_mem_decode_kernel, n_heads=n_heads, scale=float(dh) ** -0.5)
    return pl.pallas_call(
        kern,
        grid=(b,),
        in_specs=[pl.BlockSpec((None, n_heads, dh), lambda i: (i, 0, 0)),
                  pl.BlockSpec((None, rows, dh), lambda i: (base + i, 0, 0)),
                  pl.BlockSpec((None, rows, dh), lambda i: (base + i, 0, 0))],
        out_specs=pl.BlockSpec((None, n_heads, dh), lambda i: (i, 0, 0)),
        out_shape=jax.ShapeDtypeStruct((b, n_heads, dh), BF16),
        compiler_params=_cparams(("parallel",), 4 * rows * dh * 4),
        name="mem_decode_attn",
    )(q, k2, v2)


def _mem_attn_kernel(q_ref, k_ref, v_ref, o_ref, *, scale):
    s = _dot_nt(q_ref[...].astype(BF16), k_ref[...].astype(BF16)) * scale
    m = jnp.max(s, axis=1, keepdims=True)
    p = jnp.exp(s - m)
    l = jnp.sum(p, axis=1, keepdims=True)
    o_ref[...] = (_dot_nn(p.astype(BF16), v_ref[...].astype(BF16)) / l).astype(o_ref.dtype)


def _mem_attn(q, kv, *, n, t, n_mem, n_heads, dh, tq=512):
    tq = min(tq, t)
    nq = t // tq
    kern = functools.partial(_mem_attn_kernel, scale=float(dh) ** -0.5)
    return pl.pallas_call(
        kern,
        grid=(n, n_heads, nq),
        in_specs=[pl.BlockSpec((tq, dh), lambda b, h, qi: (b * nq + qi, h)),
                  pl.BlockSpec((n_mem, dh), lambda b, h, qi: (b, h)),
                  pl.BlockSpec((n_mem, dh), lambda b, h, qi: (b, n_heads + h))],
        out_specs=pl.BlockSpec((tq, dh), lambda b, h, qi: (b * nq + qi, h)),
        out_shape=jax.ShapeDtypeStruct((n * t, n_heads * dh), BF16),
        compiler_params=_cparams(("parallel", "parallel", "parallel"), 8 * tq * dh * 4 + 4 * tq * n_mem * 4),
        name="mem_attn",
    )(q, kv, kv)


def _rglru_gates(u, wa_ref, wx_ref, ba_ref, bx_ref, lam_ref):
    ub = u.astype(BF16)
    gate_a = jax.nn.sigmoid(_dot_nn(ub, wa_ref[...]) + ba_ref[...])
    gate_x = jax.nn.sigmoid(_dot_nn(ub, wx_ref[...]) + bx_ref[...])
    log_a = -RG_C * gate_a * _softplus(-lam_ref[...])
    a = jnp.exp(log_a)
    b = u * gate_x * jnp.sqrt(1.0 - jnp.exp(2.0 * log_a))
    return a, b


def _shift_rows(x, d, fill):
    t = x.shape[0]
    if d % SUBLANES == 0:
        return jnp.concatenate([jnp.full((d, x.shape[1]), fill, x.dtype), x[:t - d]], axis=0)
    rolled = pltpu.roll(x, d, 0)
    row = lax.broadcasted_iota(jnp.int32, x.shape, 0)
    return jnp.where(row < d, fill, rolled)


def _rglru_seq_kernel(xc_ref, gc_ref, cw_ref, cb_ref, wa_ref, wx_ref, ba_ref, bx_ref, lam_ref,
                      y_ref, conv_ref, h_ref, xpad_sc, *, t, conv_w):
    xpad_sc[0:SUBLANES, :] = jnp.zeros((SUBLANES, LANES), F32)
    xpad_sc[SUBLANES:, :] = xc_ref[...]
    u = cb_ref[...] + jnp.zeros((t, LANES), F32)
    for j in range(conv_w):
        u = u + xpad_sc[pl.ds(SUBLANES - (conv_w - 1) + j, t), :] * cw_ref[j:j + 1, :]
    a, b = _rglru_gates(u, wa_ref, wx_ref, ba_ref, bx_ref, lam_ref)
    d = 1
    while d < t:
        a_sh = _shift_rows(a, d, 1.0)
        b_sh = _shift_rows(b, d, 0.0)
        b = a * b_sh + b
        a = a * a_sh
        d *= 2
    y_ref[...] = (b * _gelu_tanh(gc_ref[...])).astype(y_ref.dtype)
    h_ref[...] = b[t - 1:t, :]
    conv_ref[...] = xpad_sc[pl.ds(SUBLANES + t - (conv_w - 1), conv_w - 1), :]


def _rglru_seq(z, *, n, t, d_c, xc_blk0, conv_w_arr, conv_b, wa_d, wx_d, ba, bx, lam):
    nc = d_c // LANES
    conv_w = conv_w_arr.shape[0]
    vec = lambda a: a.reshape(1, d_c)
    vspec = pl.BlockSpec((1, LANES), lambda b, c: (0, c))
    kern = functools.partial(_rglru_seq_kernel, t=t, conv_w=conv_w)
    return pl.pallas_call(
        kern,
        grid=(n, nc),
        in_specs=[pl.BlockSpec((t, LANES), lambda b, c: (b, xc_blk0 + c)),
                  pl.BlockSpec((t, LANES), lambda b, c: (b, xc_blk0 + nc + c)),
                  pl.BlockSpec((conv_w, LANES), lambda b, c: (0, c)),
                  vspec,
                  pl.BlockSpec((LANES, LANES), lambda b, c: (c, c)),
                  pl.BlockSpec((LANES, LANES), lambda b, c: (c, c)),
                  vspec, vspec, vspec],
        out_specs=[pl.BlockSpec((t, LANES), lambda b, c: (b, c)),
                   pl.BlockSpec((None, conv_w - 1, LANES), lambda b, c: (b, 0, c)),
                   pl.BlockSpec((None, 1, LANES), lambda b, c: (b, 0, c))],
        out_shape=[jax.ShapeDtypeStruct((n * t, d_c), BF16),
                   jax.ShapeDtypeStruct((n, conv_w - 1, d_c), F32),
                   jax.ShapeDtypeStruct((n, 1, d_c), F32)],
        scratch_shapes=[pltpu.VMEM((t + SUBLANES, LANES), F32)],
        compiler_params=_cparams(("parallel", "parallel"), 16 * t * LANES * 4),
        name="rglru_seq",
    )(z, z, conv_w_arr, vec(conv_b), wa_d, wx_d, vec(ba), vec(bx), vec(lam))


def _rglru_step_kernel(xc_ref, gc_ref, c0_ref, h0_ref, cw_ref, cb_ref, wa_ref, wx_ref, ba_ref, bx_ref, lam_ref,
                       y_ref, conv_ref, h_ref, *, conv_w):
    xc = xc_ref[...]
    u = cb_ref[...] + xc * cw_ref[conv_w - 1:conv_w, :]
    for j in range(conv_w - 1):
        u = u + c0_ref[:, j, :] * cw_ref[j:j + 1, :]
    a, b = _rglru_gates(u, wa_ref, wx_ref, ba_ref, bx_ref, lam_ref)
    h = a * h0_ref[...] + b
    y_ref[...] = (h * _gelu_tanh(gc_ref[...])).astype(y_ref.dtype)
    h_ref[...] = h
    for j in range(conv_w - 2):
        conv_ref[:, j, :] = c0_ref[:, j + 1, :]
    conv_ref[:, conv_w - 2, :] = xc


def _rglru_step(z, conv0, h0, *, d_c, xc_blk0, conv_w_arr, conv_b, wa_d, wx_d, ba, bx, lam):
    bsz = z.shape[0]
    conv_w = conv_w_arr.shape[0]
    ncb = d_c // LANES
    vec = lambda a: a.reshape(1, d_c)
    full = lambda shape: pl.BlockSpec(shape, lambda i: (0,) * len(shape))
    kern = functools.partial(_rglru_step_kernel, conv_w=conv_w)
    return pl.pallas_call(
        kern,
        grid=(1,),
        in_specs=[pl.BlockSpec((bsz, d_c), lambda i: (0, xc_blk0 * LANES // d_c)),
                  pl.BlockSpec((bsz, d_c), lambda i: (0, xc_blk0 * LANES // d_c + 1)),
                  full((bsz, conv_w - 1, d_c)), full((bsz, d_c)), full((conv_w, d_c)), full((1, d_c)),
                  full((d_c, d_c)), full((d_c, d_c)), full((1, d_c)), full((1, d_c)), full((1, d_c))],
        out_specs=[full((bsz, d_c)), full((bsz, conv_w - 1, d_c)), full((bsz, d_c))],
        out_shape=[jax.ShapeDtypeStruct((bsz, d_c), BF16),
                   jax.ShapeDtypeStruct((bsz, conv_w - 1, d_c), F32),
                   jax.ShapeDtypeStruct((bsz, d_c), F32)],
        compiler_params=_cparams(("arbitrary",), 16 * bsz * d_c * 4 + 4 * d_c * d_c * 2),
        name="rglru_step",
    )(z, z, conv0, h0, conv_w_arr, vec(conv_b), wa_d, wx_d, vec(ba), vec(bx), vec(lam))


def _rwkv_prep_kernel(*refs, d_b, seq_mode, blocks_per_seq):
    if seq_mode:
        zb_ref, prev_ref = refs[:2]
    else:
        zb_ref, zp_ref = refs[:2]
    (mu_ref, w0_ref, a0_ref, w2_ref, a2_ref, g2_ref, kk_ref, ka_ref, ones_ref,
     r_ref, lw_ref, km_ref, v_ref, kn_ref, bt_ref, g_ref) = refs[2:18]
    zb = zb_ref[...]
    tm = zb.shape[0]
    if seq_mode:
        sh_sc = refs[18]
        first = (pl.program_id(0) % blocks_per_seq) == 0
        prev = jnp.where(first, 0.0, prev_ref[...])
        sh_sc[0:SUBLANES, :] = prev
        sh_sc[SUBLANES:, :] = zb
        zp = sh_sc[pl.ds(SUBLANES - 1, tm), :]
    else:
        zp = zp_ref[...]
    zs = zb + mu_ref[...] * (zp - zb)
    r = zs[:, 0:d_b]
    k = zs[:, d_b:2 * d_b]
    v = zs[:, 2 * d_b:3 * d_b]
    lr = zs[:, 3 * d_b:3 * d_b + w2_ref.shape[0]]
    w_lin = _dot_nn(jnp.tanh(lr).astype(BF16), w2_ref[...])
    a_lin = _dot_nn(lr.astype(BF16), a2_ref[...])
    g = _dot_nn(jax.nn.sigmoid(lr).astype(BF16), g2_ref[...])
    w = -_softplus(-(w0_ref[...] + w_lin)) - 0.5
    a = jax.nn.sigmoid(a0_ref[...] + a_lin)
    kk = k * kk_ref[...]
    nrm2 = _dot_x_exact(kk * kk, ones_ref[...], parts=2)
    kn = kk / jnp.maximum(jnp.sqrt(nrm2), 1e-12)
    r_ref[...] = r
    lw_ref[...] = -jnp.exp(w)
    km_ref[...] = k * (1.0 + (a - 1.0) * ka_ref[...])
    v_ref[...] = v
    kn_ref[...] = kn
    bt_ref[...] = kn * a
    g_ref[...] = g


def _rwkv_prep(z, zprev, *, zb_w, d_b, mu, w0, a0, w2p, a2p, g2p, kk, ka, ones_h, seq_len, tm=256):
    m = z.shape[0]
    seq_mode = zprev is None
    tm = min(tm, seq_len if seq_mode else m)
    assert m % tm == 0
    row = lambda i: (i, 0)
    cst = lambda i: (0, 0)
    if seq_mode:
        per8 = tm // SUBLANES
        second = pl.BlockSpec((SUBLANES, zb_w), lambda i: (jnp.maximum(i * per8 - 1, 0), 0))
        second_arg = z
        scratch = [pltpu.VMEM((tm + SUBLANES, zb_w), F32)]
        bps = seq_len // tm
    else:
        second = pl.BlockSpec((tm, zb_w), row)
        second_arg = zprev
        scratch = []
        bps = 1
    lrw = w2p.shape[0]
    kern = functools.partial(_rwkv_prep_kernel, d_b=d_b, seq_mode=seq_mode, blocks_per_seq=bps)
    out = jax.ShapeDtypeStruct((m, d_b), F32)
    return pl.pallas_call(
        kern,
        grid=(m // tm,),
        in_specs=[pl.BlockSpec((tm, zb_w), row), second,
                  pl.BlockSpec((1, zb_w), cst), pl.BlockSpec((1, d_b), cst), pl.BlockSpec((1, d_b), cst),
                  pl.BlockSpec((lrw, d_b), cst), pl.BlockSpec((lrw, d_b), cst), pl.BlockSpec((lrw, d_b), cst),
                  pl.BlockSpec((1, d_b), cst), pl.BlockSpec((1, d_b), cst), pl.BlockSpec((d_b, d_b), cst)],
        out_specs=[pl.BlockSpec((tm, d_b), row)] * 7,
        out_shape=[out] * 7,
        scratch_shapes=scratch,
        compiler_params=_cparams(("parallel",), 6 * tm * zb_w * 4 + 30 * tm * d_b * 4),
        name="rwkv_prep",
    )(z, second_arg, mu, w0, a0, w2p, a2p, g2p, kk, ka, ones_h)


def _mm_p(a, b, passes, nt=False):
    if passes == 1:
        f = _dot_nt if nt else _dot_nn
        return f(a.astype(BF16), b.astype(BF16))
    return _dot3(a, b, nt=nt)


def _rwkv_chunk_kernel(*refs, c, dh, pb, nbb, passes):
    ins = [[ref.at[j] for ref in refs[:6]] for j in range(nbb)]
    y_refs = [refs[6].at[j] for j in range(nbb)]
    hout_ref, h_sc = refs[7:]
    ci = pl.program_id(2)

    @pl.when(ci == 0)
    def _():
        h_sc[...] = jnp.zeros(h_sc.shape, F32)

    c2 = 2 * c
    lane = lax.broadcasted_iota(jnp.int32, (c, LANES), 1)
    lane2 = lax.broadcasted_iota(jnp.int32, (c2, LANES), 1)
    t_idx = lax.broadcasted_iota(jnp.int32, (c, c2), 0)
    j_idx = lax.broadcasted_iota(jnp.int32, (c, c2), 1) & (c - 1)
    r128 = lax.broadcasted_iota(jnp.int32, (LANES, LANES), 0)
    c128 = lax.broadcasted_iota(jnp.int32, (LANES, LANES), 1)
    same_head = (r128 < dh) == (c128 < dh)
    zero_rows = jnp.zeros((c, c2), F32)
    head_masks = [lane2 < dh, lane2 >= dh]
    cat = lambda a, b: jnp.concatenate([a, b], axis=0)
    mm = functools.partial(_mm_p, passes=passes)

    units = []
    for j in range(nbb):
        r_ref, lw_ref, km_ref, v_ref, kn_ref, bt_ref = ins[j]
        lw_all = lw_ref[...]
        lc_all = lw_all
        d = 1
        while d < c:
            lc_all = lc_all + _shift_rows(lc_all, d, 0.0)
            d *= 2
        p_all = jnp.exp(lc_all)
        pinv_all = jnp.exp(-lc_all)
        pprev_all = jnp.exp(lc_all - lw_all)
        for pr in range(pb):
            sl = slice(pr * LANES, (pr + 1) * LANES)
            p = p_all[:, sl]
            pinv = pinv_all[:, sl]
            units.append(dict(
                ar=cat(-kn_ref[:, sl] * pprev_all[:, sl], r_ref[:, sl] * p),
                btt=bt_ref[:, sl] * pinv, kt=km_ref[:, sl] * pinv, v=v_ref[:, sl], pc=p[c - 1:c, :],
                h=h_sc[j * pb + pr], y_ref=y_refs[j], sl=sl))
    for un in units:
        un["bk"] = cat(un["btt"], un["kt"])
    arhs = [mm(un["ar"], un["h"]) for un in units]
    chains = []
    for un, arh in zip(units, arhs):
        for head in range(2):
            chains.append(dict(un=un, ah=arh[:c], rh=arh[c:], head=head))
    gs = [mm(jnp.where(head_masks[ch["head"]], ch["un"]["ar"], 0.0), ch["un"]["bk"], nt=True) for ch in chains]
    for ch, g in zip(chains, gs):
        ch["p_top"] = jnp.where(j_idx < t_idx, g[:c], 0.0)
        ch["mr"] = jnp.where(j_idx <= t_idx, g[c:], 0.0)
    upds = [mm(ch["p_top"], cat(ch["ah"], ch["un"]["v"])) for ch in chains]
    for ch, up in zip(chains, upds):
        ch["u"] = ch["ah"] + up
    for _ in range(int(np.log2(c2)) - 1):
        sq = [mm(ch["p_top"], cat(ch["p_top"], zero_rows)) for ch in chains]
        for ch, s in zip(chains, sq):
            ch["p_top"] = s
        upds = [mm(ch["p_top"], cat(ch["u"], ch["un"]["v"])) for ch in chains]
        for ch, up in zip(chains, upds):
            ch["u"] = ch["u"] + up
    yparts = [mm(ch["mr"], cat(ch["u"], ch["un"]["v"])) for ch in chains]
    ma_l = lane < dh
    for i, un in enumerate(units):
        c0, c1 = chains[2 * i], chains[2 * i + 1]
        un["u"] = jnp.where(ma_l, c0["u"], c1["u"])
        un["y_ref"][:, un["sl"]] = jnp.where(ma_l, c0["rh"] + yparts[2 * i], c1["rh"] + yparts[2 * i + 1])
    upds = [mm(cat(un["btt"] * un["pc"], un["kt"] * un["pc"]).T, cat(un["u"], un["v"])) for un in units]
    h_news = []
    for un, upd in zip(units, upds):
        pcol = jnp.broadcast_to(un["pc"], (SUBLANES, LANES)).T[:, 0:1]
        h_news.append(jnp.where(same_head, un["h"] * pcol + upd, 0.0))
    for i, h_new in enumerate(h_news):
        h_sc[i] = h_new

    @pl.when(ci == pl.num_programs(2) - 1)
    def _():
        for i, h_new in enumerate(h_news):
            hout_ref[i // pb, i % pb] = h_new


def _rwkv_chunk(r, lw, km, v, kn, bt, *, n, t, d_b, dh, pb=4, nbb=2, passes=1):
    c = RWKV_CHUNK
    assert t % c == 0 and 2 * dh == LANES and 2 * c == LANES
    npair = d_b // LANES
    pb = min(pb, npair)
    nbb = min(nbb, n)
    assert npair % pb == 0 and n % nbb == 0
    nchunk = t // c
    tok = pl.BlockSpec((nbb, c, pb * LANES), lambda b, pg, ci: (b, ci, pg))
    kern = functools.partial(_rwkv_chunk_kernel, c=c, dh=dh, pb=pb, nbb=nbb, passes=passes)
    seq = lambda a: a.reshape(n, t, d_b)
    y, h_pair = pl.pallas_call(
        kern,
        grid=(n // nbb, npair // pb, nchunk),
        in_specs=[tok] * 6,
        out_specs=[tok, pl.BlockSpec((nbb, pb, LANES, LANES), lambda b, pg, ci: (b, pg, 0, 0))],
        out_shape=[jax.ShapeDtypeStruct((n, t, d_b), F32),
                   jax.ShapeDtypeStruct((n, npair, LANES, LANES), F32)],
        scratch_shapes=[pltpu.VMEM((nbb * pb, LANES, LANES), F32)],
        compiler_params=_cparams(("parallel", "parallel", "arbitrary"), 64 * nbb * pb * LANES * LANES * 4),
        name="rwkv_chunk",
    )(seq(r), seq(lw), seq(km), seq(v), seq(kn), seq(bt))
    return y.reshape(n * t, d_b), h_pair


def _rwkv_step_kernel(r_ref, lw_ref, km_ref, v_ref, kn_ref, bt_ref, s_ref, ones_ref, y_ref, sout_ref,
                      *, bb, npair, dh):
    i2 = (lax.broadcasted_iota(jnp.int32, (dh, LANES), 1) & (dh - 1)) == \
        lax.broadcasted_iota(jnp.int32, (dh, LANES), 0)
    ones = ones_ref[...]
    for b in range(bb):
        for pr in range(npair):
            sl = slice(pr * LANES, (pr + 1) * LANES)
            rb = slice(b, b + 1)
            w_row = jnp.exp(lw_ref[rb, sl])
            s = s_ref[b, pr]
            sa = _dot_x_exact(s * (-kn_ref[rb, sl]), ones)
            vcol = _dot_x_exact(jnp.where(i2, v_ref[rb, sl], 0.0), ones)
            s_new = s * w_row + sa * bt_ref[rb, sl] + vcol * km_ref[rb, sl]
            yb = _dot_x_exact(s_new * r_ref[rb, sl], ones)
            y_ref[rb, sl] = jnp.sum(jnp.where(i2, yb, 0.0), axis=0, keepdims=True)
            sout_ref[b, pr] = s_new


def _rwkv_step(r, lw, km, v, kn, bt, s2, ones_pair, *, dh, bb=8):
    bsz, d_b = r.shape
    npair = d_b // LANES
    bb = min(bb, bsz)
    tok = pl.BlockSpec((bb, d_b), lambda i: (i, 0))
    st = pl.BlockSpec((bb, npair, dh, LANES), lambda i: (i, 0, 0, 0))
    kern = functools.partial(_rwkv_step_kernel, bb=bb, npair=npair, dh=dh)
    return pl.pallas_call(
        kern,
        grid=(bsz // bb,),
        in_specs=[tok] * 6 + [st, pl.BlockSpec((LANES, LANES), lambda i: (0, 0))],
        out_specs=[tok, st],
        out_shape=[jax.ShapeDtypeStruct((bsz, d_b), F32),
                   jax.ShapeDtypeStruct((bsz, npair, dh, LANES), F32)],
        compiler_params=_cparams(("parallel",), 8 * bb * npair * dh * LANES * 4),
        name="rwkv_step",
    )(r, lw, km, v, kn, bt, s2, ones_pair)


def _rwkv_post_kernel(y_ref, r_ref, km_ref, v_ref, g_ref, lw_ref, lb_ref, rk_ref, ones_ref, o_ref, *, dh):
    ones = ones_ref[...]
    y = y_ref[...]
    inv = 1.0 / dh
    mu = _dot_x_exact(y, ones, parts=3) * inv
    d = y - mu
    var = _dot_x_exact(d * d, ones, parts=2) * inv
    yn = d * lax.rsqrt(var + LNX_EPS) * lw_ref[...] + lb_ref[...]
    bonus = _dot_x_exact(r_ref[...] * km_ref[...] * rk_ref[...], ones, parts=3) * v_ref[...]
    o_ref[...] = ((yn + bonus) * g_ref[...]).astype(o_ref.dtype)


def _rwkv_post(y, r, km, v, g, *, lnx_w, lnx_b, rk, ones_h, dh, tm=256):
    m, d_b = y.shape
    tm = min(tm, m)
    row = pl.BlockSpec((tm, d_b), lambda i: (i, 0))
    vec = pl.BlockSpec((1, d_b), lambda i: (0, 0))
    kern = functools.partial(_rwkv_post_kernel, dh=dh)
    return pl.pallas_call(
        kern,
        grid=(m // tm,),
        in_specs=[row] * 5 + [vec] * 3 + [pl.BlockSpec((d_b, d_b), lambda i: (0, 0))],
        out_specs=row,
        out_shape=jax.ShapeDtypeStruct((m, d_b), BF16),
        compiler_params=_cparams(("parallel",), 30 * tm * d_b * 4),
        name="rwkv_post",
    )(y, r, km, v, g, lnx_w, lnx_b, rk, ones_h)


def _round_up(x, m):
    return (x + m - 1) // m * m


def _block_ones(size, blk):
    idx = np.arange(size) // blk
    return jnp.asarray(idx[:, None] == idx[None, :], dtype=BF16)


def _block_diag(w):
    nb, bs, _ = w.shape
    eye = jnp.eye(nb, dtype=w.dtype)
    return (eye[:, None, :, None] * w[:, :, None, :]).reshape(nb * bs, nb * bs)


def kernel(x_prompt, x_sample, cache_fox_k, cache_fox_v, cache_fox_logf, state_rwkv_shift, state_rwkv_wkv, state_rglru_conv, state_rglru_h, cache_mem_k, cache_mem_v, page_table, mem_prompt, norm_mix, w_in, fox_bf, rw_mu, rw_w0, rw_w2, rw_a0, rw_a2, rw_g2, rw_kk, rw_ka, rw_rk, rw_lnx_w, rw_lnx_b, rg_conv_w, rg_conv_b, rg_wa, rg_ba, rg_wx, rg_bx, rg_lambda, w_out, norm_x, norm_mem, w_xq, w_xk, w_xv, w_xo, norm_ff, w_ff1, w_ff2, norm_f):
    nb_p, t_p, d_model = x_prompt.shape
    nb_s = x_sample.shape[0]
    depth, n_phys, page, h_a, dh_a = cache_fox_k.shape
    d_a = h_a * dh_a
    n_b_cols = state_rwkv_shift.shape[-1]
    _, _, h_b, dh_b, _ = state_rwkv_wkv.shape
    d_b = h_b * dh_b
    d_c = state_rglru_h.shape[-1]
    n_mem, h_x, dh_x = cache_mem_k.shape[2:]
    d_x = h_x * dh_x
    r_dec, r_icl, r_gate = rw_w2.shape[1], rw_a2.shape[1], rw_g2.shape[1]
    lr_w = r_dec + r_icl + r_gate
    zb_w = _round_up(n_b_cols + h_a, max(d_c, LANES))
    fa_blk = n_b_cols // LANES
    q_blk0 = zb_w // LANES
    xc_blk0 = (zb_w + 3 * d_a) // LANES
    assert n_b_cols % LANES == 0 and d_a % LANES == 0 and d_c % LANES == 0 and h_a <= LANES
    assert (zb_w + 3 * d_a) % d_c == 0 and n_b_cols == 3 * d_b + lr_w and (h_a & (h_a - 1)) == 0 and (h_x & (h_x - 1)) == 0

    splits = np.cumsum([d_a, d_a, d_a, h_a, n_b_cols, d_c])
    ones_h = _block_ones(d_b, dh_b)
    ones_pair = _block_ones(LANES, dh_b)
    g_mat = jnp.asarray(np.arange(page)[:, None] <= (np.arange(page * h_a)[None, :] // h_a), dtype=BF16)

    kc = cache_fox_k.reshape(depth * n_phys, page * h_a, dh_a)
    vc = cache_fox_v.reshape(depth * n_phys, page * h_a, dh_a)
    lf_t = jnp.swapaxes(cache_fox_logf, 2, 3).reshape(depth * n_phys, h_a, page)
    mk2 = cache_mem_k.reshape(depth * nb_s, n_mem * h_x, dh_x)
    mv2 = cache_mem_v.reshape(depth * nb_s, n_mem * h_x, dh_x)

    xp = x_prompt.reshape(nb_p * t_p, d_model)
    xs = x_sample.reshape(nb_s, d_model)
    memf = mem_prompt.reshape(nb_p * n_mem, d_model)
    row1 = lambda a: a.reshape(1, -1).astype(F32)
    pad_cols = lambda a, w: jnp.pad(a, ((0, 0), (0, w - a.shape[1])))

    p_states, s_states = [], []
    for l in range(depth):
        wq, wk, wv, wf, wzb, wxc, wgc = jnp.split(w_in[l], splits, axis=1)
        w_in_r = jnp.concatenate(
            [wzb, wf, jnp.zeros((d_model, zb_w - n_b_cols - h_a), F32), wq, wk, wv, wxc, wgc], axis=1).astype(BF16)
        bf_pad = pad_cols(row1(fox_bf[l]), LANES)
        mu_pad = pad_cols(row1(rw_mu[l]), zb_w)
        zrow = lambda r0, w, rows: jnp.pad(w, ((r0, lr_w - r0 - rows), (0, 0))).astype(BF16)
        w2p = zrow(0, rw_w2[l], r_dec)
        a2p = zrow(r_dec, rw_a2[l], r_icl)
        g2p = zrow(r_dec + r_icl, rw_g2[l], r_gate)
        wa_d = _block_diag(rg_wa[l]).astype(BF16)
        wx_d = _block_diag(rg_wx[l]).astype(BF16)
        w_out_b = w_out[l].astype(BF16)
        w_xq_b = w_xq[l].astype(BF16)
        w_xkv_b = jnp.concatenate([w_xk[l], w_xv[l]], axis=1).astype(BF16)
        w_xo_b = w_xo[l].astype(BF16)
        w_ff1_b = w_ff1[l].astype(BF16)
        w_ff2_b = w_ff2[l].astype(BF16)
        rwkv_par = dict(zb_w=zb_w, d_b=d_b, mu=mu_pad, w0=row1(rw_w0[l]), a0=row1(rw_a0[l]), w2p=w2p, a2p=a2p,
                        g2p=g2p, kk=row1(rw_kk[l]), ka=row1(rw_ka[l]), ones_h=ones_h)
        post_par = dict(lnx_w=row1(rw_lnx_w[l]), lnx_b=row1(rw_lnx_b[l]), rk=row1(rw_rk[l]), ones_h=ones_h, dh=dh_b)
        rg_par = dict(d_c=d_c, xc_blk0=xc_blk0, conv_w_arr=rg_conv_w[l], conv_b=rg_conv_b[l], wa_d=wa_d, wx_d=wx_d,
                      ba=rg_ba[l], bx=rg_bx[l], lam=rg_lambda[l])

        def tail(x, ya, yb, yc, attend, tm):
            segs = [(w_out_b, d_a, 0), (w_out_b, d_b, d_a // d_b), (w_out_b, d_c, (d_a + d_b) // d_c)]
            x = _matmul([ya, yb, yc], segs, residual=x, tm=tm, name="mix_out_proj")
            q = _matmul([x], [(w_xq_b, d_model, 0)], gain=norm_x[l], tm=tm, name="mem_q_proj")
            o = attend(q)
            x = _matmul([o], [(w_xo_b, d_x, 0)], residual=x, tm=tm, name="mem_out_proj")
            hid = _matmul([x], [(w_ff1_b, d_model, 0)], gain=norm_ff[l], epilogue="relu2", out_dtype=BF16, tm=tm,
                          name="ff_up")
            return _matmul([hid], [(w_ff2_b, w_ff2_b.shape[0], 0)], residual=x, tm=tm, name="ff_down")

        m_p = nb_p * t_p
        mkv = _matmul([memf], [(w_xkv_b, d_model, 0)], gain=norm_mem[l], tm=512, name="mem_kv_proj")
        z = _matmul([xp], [(w_in_r, d_model, 0)], gain=norm_mix[l], tm=1024, name="mix_in_proj")
        lf, c_col, c_t = _logf_cumsum(z, fa_blk, bf_pad, nb_p, t_p)
        c_row = c_t[:, :h_a, :].reshape(nb_p * h_a, t_p)
        ya = _fox_prompt(z, c_col, c_row, n=nb_p, t=t_p, n_heads=h_a, dh=dh_a, q_blk0=q_blk0)
        r, lw, km, v, kn, bt, g = _rwkv_prep(z, None, seq_len=t_p, **rwkv_par)
        y, h_pair = _rwkv_chunk(r, lw, km, v, kn, bt, n=nb_p, t=t_p, d_b=d_b, dh=dh_b)
        yb = _rwkv_post(y, r, km, v, g, **post_par)
        yc, conv1, h1 = _rglru_seq(z, n=nb_p, t=t_p, **rg_par)
        xp = tail(xp, ya, yb, yc,
                  lambda q: _mem_attn(q, mkv, n=nb_p, t=t_p, n_mem=n_mem, n_heads=h_x, dh=dh_x), 512)
        z3 = z.reshape(nb_p, t_p, -1)
        hp = h_pair.reshape(nb_p, d_b // LANES, 2, dh_b, 2, dh_b)
        wkv = jnp.stack([hp[:, :, 0, :, 0, :], hp[:, :, 1, :, 1, :]], axis=2)
        wkv = jnp.swapaxes(wkv, -1, -2).reshape(nb_p, h_b, dh_b, dh_b)
        p_states.append((
            z3[:, :, zb_w + d_a:zb_w + 2 * d_a].reshape(nb_p, t_p, h_a, dh_a),
            z3[:, :, zb_w + 2 * d_a:zb_w + 3 * d_a].reshape(nb_p, t_p, h_a, dh_a),
            lf.reshape(nb_p, t_p, LANES)[:, :, :h_a],
            z3[:, t_p - 1, :n_b_cols],
            wkv,
            conv1,
            h1.reshape(nb_p, d_c),
            mkv[:, :d_x].reshape(nb_p, n_mem, h_x, dh_x),
            mkv[:, d_x:].reshape(nb_p, n_mem, h_x, dh_x),
        ))

        zs = _matmul([xs], [(w_in_r, d_model, 0)], gain=norm_mix[l], tm=nb_s, name="mix_in_proj_s")
        lf_s, _, _ = _logf_cumsum(zs, fa_blk, bf_pad, 1, nb_s)
        hd = lambda a: a.reshape(nb_s, h_a, dh_a)
        q_s = hd(zs[:, zb_w:zb_w + d_a])
        k_s = hd(zs[:, zb_w + d_a:zb_w + 2 * d_a])
        v_s = hd(zs[:, zb_w + 2 * d_a:zb_w + 3 * d_a])
        lfn = jnp.broadcast_to(lf_s[:, :h_a, None], (nb_s, h_a, dh_a))
        ya_s = _fox_decode(page_table, q_s, k_s, v_s, lfn, kc, vc, lf_t, g_mat, layer=l, n_phys=n_phys)
        ya_s = ya_s.reshape(nb_s, d_a).astype(BF16)
        zprev = pad_cols(state_rwkv_shift[l], zb_w)
        r, lw, km, v, kn, bt, g = _rwkv_prep(zs, zprev, seq_len=1, **rwkv_par)
        s2 = state_rwkv_wkv[l].reshape(nb_s, d_b // LANES, 2, dh_b, dh_b)
        s2 = jnp.swapaxes(s2, 2, 3).reshape(nb_s, d_b // LANES, dh_b, LANES)
        y, s2n = _rwkv_step(r, lw, km, v, kn, bt, s2, ones_pair, dh=dh_b)
        wkv_s = jnp.swapaxes(s2n.reshape(nb_s, d_b // LANES, dh_b, 2, dh_b), 2, 3).reshape(nb_s, h_b, dh_b, dh_b)
        yb_s = _rwkv_post(y, r, km, v, g, **post_par)
        yc_s, conv1_s, h1_s = _rglru_step(zs, state_rglru_conv[l], state_rglru_h[l], **rg_par)
        xs = tail(xs, ya_s, yb_s, yc_s,
                  lambda q: _mem_decode(q.reshape(nb_s, h_x, dh_x), mk2, mv2, layer=l).reshape(nb_s, d_x), nb_s)
        s_states.append((
            k_s.reshape(nb_s, 1, h_a, dh_a),
            v_s.reshape(nb_s, 1, h_a, dh_a),
            lf_s[:, :h_a].reshape(nb_s, 1, h_a),
            zs[:, :n_b_cols],
            wkv_s,
            conv1_s,
            h1_s,
        ))

    y_prompt = _rmsnorm(xp, norm_f).reshape(nb_p, t_p, d_model)
    y_sample = _rmsnorm(xs, norm_f).reshape(nb_s, 1, d_model)
    p_out = [jnp.stack(s) for s in zip(*p_states)]
    s_out = [jnp.stack(s) for s in zip(*s_states)]
    return (y_prompt, y_sample, *p_out, *s_out)
```

```python
import functools

import numpy as np
import jax
import jax.numpy as jnp
from jax import lax
from jax.experimental import pallas as pl
from jax.experimental.pallas import tpu as pltpu

F32 = jnp.float32
BF16 = jnp.bfloat16

NORM_EPS = 1e-6
LNX_EPS = 64e-5
RG_C = 8.0
NEG_INF = -1e30

LANES = 128
SUBLANES = 8
VMEM_CAP_BYTES = 60000 * 1024
RWKV_CHUNK = 64


def _cparams(semantics, est_bytes):
    limit = int(min(max(2 * est_bytes + (8 << 20), 24 << 20), VMEM_CAP_BYTES))
    return pltpu.CompilerParams(dimension_semantics=semantics, vmem_limit_bytes=limit)


def _split_bf16(x, parts):
    out = []
    r = x
    for i in range(parts):
        h = r.astype(BF16)
        out.append(h)
        if i + 1 < parts:
            r = r - h.astype(F32)
    return out


def _dot_nn(a, b):
    return jnp.dot(a, b, preferred_element_type=F32)


def _dot_nt(a, b):
    return lax.dot_general(a, b, (((1,), (1,)), ((), ())), preferred_element_type=F32)


def _dot_x_exact(x, w_exact, parts=3):
    return sum(_dot_nn(p, w_exact) for p in _split_bf16(x, parts))


def _dot_exact_x(w_exact, x, parts=3):
    return sum(_dot_nn(w_exact, p) for p in _split_bf16(x, parts))


def _dot3(a, b, nt=False):
    f = _dot_nt if nt else _dot_nn
    ah, al = _split_bf16(a, 2)
    bh, bl = _split_bf16(b, 2)
    return f(ah, bh) + f(ah, bl) + f(al, bh)


def _softplus(x):
    return jnp.maximum(x, 0.0) + jnp.log1p(jnp.exp(-jnp.abs(x)))


def _log_sigmoid(x):
    return -_softplus(-x)


def _gelu_tanh(x):
    c = np.float32(np.sqrt(2.0 / np.pi))
    return 0.5 * x * (1.0 + jnp.tanh(c * (x + 0.044715 * (x * x * x))))


def _mm_kernel(*refs, n_seg, has_gain, has_res, epilogue):
    xs = refs[:n_seg]
    ws = refs[n_seg:2 * n_seg]
    pos = 2 * n_seg
    g_ref = res_ref = None
    if has_gain:
        g_ref = refs[pos]
        pos += 1
    if has_res:
        res_ref = refs[pos]
        pos += 1
    o_ref = refs[pos]
    if has_gain:
        xn_ref = refs[pos + 1]

        @pl.when(pl.program_id(1) == 0)
        def _():
            x = xs[0][...]
            ms = jnp.mean(x * x, axis=-1, keepdims=True)
            xn_ref[...] = (x * lax.rsqrt(ms + NORM_EPS) * g_ref[...]).astype(BF16)

        acc = _dot_nn(xn_ref[...], ws[0][...])
    else:
        acc = _dot_nn(xs[0][...], ws[0][...])
        for x_ref, w_ref in zip(xs[1:], ws[1:]):
            acc = acc + _dot_nn(x_ref[...], w_ref[...])
    if epilogue == "relu2":
        r = jnp.maximum(acc, 0.0)
        acc = r * r
    if has_res:
        acc = acc + res_ref[...]
    o_ref[...] = acc.astype(o_ref.dtype)


def _matmul(xs, ws, *, gain=None, residual=None, epilogue="none", out_dtype=F32, tm=512, tn=512, name="mm"):
    m = xs[0].shape[0]
    w0 = ws[0][0]
    n = (w0[0] if isinstance(w0, tuple) else w0).shape[-1]
    tm = min(tm, m)
    tn = min(tn, n)
    while n % tn:
        tn //= 2
    assert m % tm == 0 and tn % LANES == 0
    in_specs, args = [], []
    est = 0
    for x in xs:
        k = x.shape[1]
        in_specs.append(pl.BlockSpec((tm, k), lambda i, j: (i, 0)))
        args.append(x)
        est += 2 * tm * k * x.dtype.itemsize
    for (w, k, rb) in ws:
        if isinstance(w, tuple):
            w, layer = w
            in_specs.append(pl.BlockSpec((None, k, tn), lambda i, j, rb=rb, layer=layer: (layer, rb, j)))
        else:
            in_specs.append(pl.BlockSpec((k, tn), lambda i, j, rb=rb: (rb, j)))
        args.append(w)
        est += 2 * k * tn * w.dtype.itemsize
    scratch = []
    if gain is not None:
        k = xs[0].shape[1]
        in_specs.append(pl.BlockSpec((1, k), lambda i, j: (0, 0)))
        args.append(gain.reshape(1, k).astype(F32))
        scratch.append(pltpu.VMEM((tm, k), BF16))
        est += tm * k * 2
    if residual is not None:
        in_specs.append(pl.BlockSpec((tm, tn), lambda i, j: (i, j)))
        args.append(residual)
        est += 2 * tm * tn * 4
    est += 3 * tm * tn * 4
    kern = functools.partial(_mm_kernel, n_seg=len(xs), has_gain=gain is not None,
                             has_res=residual is not None, epilogue=epilogue)
    return pl.pallas_call(
        kern,
        grid=(m // tm, n // tn),
        in_specs=in_specs,
        out_specs=pl.BlockSpec((tm, tn), lambda i, j: (i, j)),
        out_shape=jax.ShapeDtypeStruct((m, n), out_dtype),
        scratch_shapes=scratch,
        compiler_params=_cparams(("parallel", "arbitrary"), est),
        name=name,
    )(*args)


def _rmsnorm_kernel(x_ref, g_ref, o_ref):
    x = x_ref[...]
    ms = jnp.mean(x * x, axis=-1, keepdims=True)
    o_ref[...] = x * lax.rsqrt(ms + NORM_EPS) * g_ref[...]


def _rmsnorm(x, g, tm=512):
    m, d = x.shape
    tm = min(tm, m)
    return pl.pallas_call(
        _rmsnorm_kernel,
        grid=(m // tm,),
        in_specs=[pl.BlockSpec((tm, d), lambda i: (i, 0)), pl.BlockSpec((1, d), lambda i: (0, 0))],
        out_specs=pl.BlockSpec((tm, d), lambda i: (i, 0)),
        out_shape=jax.ShapeDtypeStruct((m, d), F32),
        compiler_params=_cparams(("parallel",), 4 * tm * d * 4),
        name="final_rmsnorm",
    )(x, g.reshape(1, d))


def _logf_kernel(fa_ref, bf_ref, lf_ref, c_ref, ct_ref, *, t, blk):
    row = lax.broadcasted_iota(jnp.int32, (blk, blk), 0)
    col = lax.broadcasted_iota(jnp.int32, (blk, blk), 1)
    tri = (col <= row).astype(BF16)
    carry = jnp.zeros((1, LANES), F32)
    for b in range(t // blk):
        sl = pl.ds(b * blk, blk)
        lf = _log_sigmoid(fa_ref[sl, :] + bf_ref[...])
        lf_ref[sl, :] = lf
        c = _dot_exact_x(tri, lf) + carry
        c_ref[sl, :] = c
        ct_ref[:, sl] = c.T
        carry = c[blk - 1:blk, :]


def _logf_cumsum(z, fa_blk, bf_pad, n, t):
    blk = min(t, 256)
    kern = functools.partial(_logf_kernel, t=t, blk=blk)
    return pl.pallas_call(
        kern,
        grid=(n,),
        in_specs=[pl.BlockSpec((t, LANES), lambda i: (i, fa_blk)),
                  pl.BlockSpec((1, LANES), lambda i: (0, 0))],
        out_specs=[pl.BlockSpec((t, LANES), lambda i: (i, 0)),
                   pl.BlockSpec((t, LANES), lambda i: (i, 0)),
                   pl.BlockSpec((None, LANES, t), lambda i: (i, 0, 0))],
        out_shape=[jax.ShapeDtypeStruct((n * t, LANES), F32),
                   jax.ShapeDtypeStruct((n * t, LANES), F32),
                   jax.ShapeDtypeStruct((n, LANES, t), F32)],
        compiler_params=_cparams(("parallel",), 10 * t * LANES * 4),
        name="logf_cumsum",
    )(z, bf_pad)


def _fox_attn_kernel(q_ref, k_ref, v_ref, cq_ref, ck_ref, o_ref, m_sc, l_sc, acc_sc, *, scale, tq, dh, hb):
    hg = pl.program_id(1)
    qi = pl.program_id(2)
    m_sc[...] = jnp.full(m_sc.shape, NEG_INF, F32)
    l_sc[...] = jnp.zeros(l_sc.shape, F32)
    acc_sc[...] = jnp.zeros(acc_sc.shape, F32)
    lane = lax.broadcasted_iota(jnp.int32, (tq, LANES), 1)
    cols = [slice(j * dh, (j + 1) * dh) for j in range(hb)]
    qs = [q_ref[:, cols[j]].astype(BF16) for j in range(hb)]
    cqs = [jnp.sum(jnp.where(lane == hg * hb + j, cq_ref[...], 0.0), axis=1, keepdims=True) for j in range(hb)]

    def block(ki, diagonal):
        rows = pl.ds(pl.multiple_of(ki * tq, tq), tq)
        ss = [_dot_nt(qs[j], k_ref[rows, cols[j]].astype(BF16)) for j in range(hb)]
        ps, alphas = [], []
        for j in range(hb):
            s = ss[j] * scale + cqs[j] - ck_ref[j, ki]
            if diagonal:
                causal = (lax.broadcasted_iota(jnp.int32, (tq, tq), 1)
                          <= lax.broadcasted_iota(jnp.int32, (tq, tq), 0))
                s = jnp.where(causal, s, NEG_INF)
            m_old = m_sc[j]
            m_new = jnp.maximum(m_old, jnp.max(s, axis=1, keepdims=True))
            alpha = jnp.exp(m_old - m_new)
            p = jnp.exp(s - m_new)
            l_sc[j] = alpha * l_sc[j] + jnp.sum(p, axis=1, keepdims=True)
            m_sc[j] = m_new
            ps.append(p.astype(BF16))
            alphas.append(alpha)
        pvs = [_dot_nn(ps[j], v_ref[rows, cols[j]].astype(BF16)) for j in range(hb)]
        for j in range(hb):
            acc_sc[j] = alphas[j] * acc_sc[j] + pvs[j]

    def body(ki, carry):
        block(ki, False)
        return carry

    lax.fori_loop(0, qi, body, 0)
    block(qi, True)
    for j in range(hb):
        o_ref[:, cols[j]] = (acc_sc[j] / l_sc[j]).astype(o_ref.dtype)


def _fox_prompt(z, c_col, c_row, *, n, t, n_heads, dh, q_blk0, tq=512, hb=4):
    tq = min(tq, t)
    nq = t // tq
    hb = min(hb, n_heads)
    assert n_heads % hb == 0 and q_blk0 * LANES % (hb * dh) == 0
    ng = n_heads // hb
    g0 = q_blk0 * LANES // (hb * dh)
    kern = functools.partial(_fox_attn_kernel, scale=float(dh) ** -0.5, tq=tq, dh=dh, hb=hb)
    return pl.pallas_call(
        kern,
        grid=(n, ng, nq),
        in_specs=[
            pl.BlockSpec((tq, hb * dh), lambda b, g, qi: (b * nq + qi, g0 + g)),
            pl.BlockSpec((t, hb * dh), lambda b, g, qi: (b, g0 + ng + g)),
            pl.BlockSpec((t, hb * dh), lambda b, g, qi: (b, g0 + 2 * ng + g)),
            pl.BlockSpec((tq, LANES), lambda b, g, qi: (b * nq + qi, 0)),
            pl.BlockSpec((hb, nq, 1, tq), lambda b, g, qi: (b * ng + g, 0, 0, 0)),
        ],
        out_specs=pl.BlockSpec((tq, hb * dh), lambda b, g, qi: (b * nq + qi, g)),
        out_shape=jax.ShapeDtypeStruct((n * t, n_heads * dh), BF16),
        scratch_shapes=[pltpu.VMEM((hb, tq, 1), F32), pltpu.VMEM((hb, tq, 1), F32),
                        pltpu.VMEM((hb, tq, dh), F32)],
        compiler_params=_cparams(("parallel", "parallel", "arbitrary"),
                                 hb * (4 * t * dh * 4 + 8 * tq * dh * 4 + 6 * tq * tq * 4)),
        name="fox_prompt_attn",
    )(z, z, z, c_col, c_row.reshape(n * n_heads, nq, 1, tq))


def _head_valid(n_heads, rows):
    lane = lax.broadcasted_iota(jnp.int32, (n_heads, rows), 1)
    sub = lax.broadcasted_iota(jnp.int32, (n_heads, rows), 0)
    return (lane & (n_heads - 1)) == sub


def _fox_decode_kernel(pt_ref, q_ref, kn_ref, vn_ref, lfn_ref, *rest, n_heads, scale, g):
    kcs, vcs, lfs = rest[0:g], rest[g:2 * g], rest[2 * g:3 * g]
    g_ref, o_ref, m_sc, l_sc, acc_sc, car_sc = rest[3 * g:]
    p = pl.program_id(1)

    @pl.when(p == 0)
    def _():
        m_sc[...] = jnp.full(m_sc.shape, NEG_INF, F32)
        l_sc[...] = jnp.zeros(l_sc.shape, F32)
        acc_sc[...] = jnp.zeros(acc_sc.shape, F32)
        car_sc[...] = jnp.zeros(car_sc.shape, F32)

    rows = kcs[0].shape[0]
    qb = q_ref[...].astype(BF16)
    valid = _head_valid(n_heads, rows)
    f_all = jnp.concatenate([lf[...] for lf in lfs], axis=0)
    cum_all = _dot_x_exact(f_all, g_ref[...])
    car = car_sc[...]
    ss = []
    for j in range(g):
        s = _dot_nt(qb, kcs[j][...].astype(BF16)) * scale
        ss.append(jnp.where(valid, s - (car + cum_all[j * n_heads:(j + 1) * n_heads]), NEG_INF))
        car = car + jnp.sum(lfs[j][...], axis=1, keepdims=True)
    car_new = car
    m_old = m_sc[...]
    m_new = m_old
    for s in ss:
        m_new = jnp.maximum(m_new, jnp.max(s, axis=1, keepdims=True))
    alpha = jnp.exp(m_old - m_new)
    l_new = alpha * l_sc[...]
    acc_new = alpha * acc_sc[...]
    for j in range(g):
        pr = jnp.exp(ss[j] - m_new)
        l_new = l_new + jnp.sum(pr, axis=1, keepdims=True)
        acc_new = acc_new + _dot_nn(pr.astype(BF16), vcs[j][...].astype(BF16))
    m_sc[...] = m_new
    l_sc[...] = l_new
    acc_sc[...] = acc_new
    car_sc[...] = car_new

    @pl.when(p == pl.num_programs(1) - 1)
    def _():
        s_new = jnp.sum(q_ref[...] * kn_ref[...], axis=1, keepdims=True) * scale - (car_new + lfn_ref[:, 0:1])
        m2 = jnp.maximum(m_new, s_new)
        a2 = jnp.exp(m_new - m2)
        pn = jnp.exp(s_new - m2)
        o_ref[...] = (a2 * acc_new + pn * vn_ref[...]) / (a2 * l_new + pn)


def _fox_decode(page_table, q, k_new, v_new, lf_new, kc, vc, lf_t, g_mat, *, layer, n_phys, g=8):
    b, n_heads, dh = q.shape
    n_pages = page_table.shape[1]
    g = min(g, n_pages)
    assert n_pages % g == 0
    rows = kc.shape[1]
    page = lf_t.shape[2]
    base = layer * n_phys
    tok = pl.BlockSpec((None, n_heads, dh), lambda i, p, pt: (i, 0, 0))
    pg = lambda j: (lambda i, p, pt: (base + pt[i, p * g + j], 0, 0))
    kern = functools.partial(_fox_decode_kernel, n_heads=n_heads, scale=float(dh) ** -0.5, g=g)
    return pl.pallas_call(
        kern,
        grid_spec=pltpu.PrefetchScalarGridSpec(
            num_scalar_prefetch=1,
            grid=(b, n_pages // g),
            in_specs=([tok] * 4
                      + [pl.BlockSpec((None, rows, dh), pg(j)) for j in range(g)]
                      + [pl.BlockSpec((None, rows, dh), pg(j)) for j in range(g)]
                      + [pl.BlockSpec((None, n_heads, page), pg(j)) for j in range(g)]
                      + [pl.BlockSpec((page, rows), lambda i, p, pt: (0, 0))]),
            out_specs=tok,
            scratch_shapes=[pltpu.VMEM((n_heads, 1), F32), pltpu.VMEM((n_heads, 1), F32),
                            pltpu.VMEM((n_heads, dh), F32), pltpu.VMEM((n_heads, 1), F32)],
        ),
        out_shape=jax.ShapeDtypeStruct((b, n_heads, dh), F32),
        compiler_params=_cparams(("parallel", "arbitrary"), g * 4 * rows * dh * 4 + 2 * page * rows * 2),
        name="fox_decode_attn",
    )(page_table, q, k_new, v_new, lf_new, *([kc] * g), *([vc] * g), *([lf_t] * g), g_mat)


def _mem_decode_kernel(q_ref, k_ref, v_ref, o_ref, *, n_heads, scale, bb):
    rows = k_ref.shape[1]
    valid = _head_valid(n_heads, rows)
    ss = [_dot_nt(q_ref[j].astype(BF16), k_ref[j].astype(BF16)) for j in range(bb)]
    ps, ls = [], []
    for s in ss:
        s = jnp.where(valid, s * scale, NEG_INF)
        p = jnp.exp(s - jnp.max(s, axis=1, keepdims=True))
        ls.append(jnp.sum(p, axis=1, keepdims=True))
        ps.append(p.astype(BF16))
    os_ = [_dot_nn(ps[j], v_ref[j].astype(BF16)) for j in range(bb)]
    for j in range(bb):
        o_ref[j] = (os_[j] / ls[j]).astype(o_ref.dtype)


def _mem_decode(q, k2, v2, *, layer, bb=4):
    b, n_heads, dh = q.shape
    rows = k2.shape[1]
    bb = min(bb, b)
    assert b % bb == 0
    base = layer * b // bb
    kern = functools.partial(_mem_decode_kernel, n_heads=n_heads, scale=float(dh) ** -0.5, bb=bb)
    return pl.pallas_call(
        kern,
        grid=(b // bb,),
        in_specs=[pl.BlockSpec((bb, n_heads, dh), lambda i: (i, 0, 0)),
                  pl.BlockSpec((bb, rows, dh), lambda i: (base + i, 0, 0)),
                  pl.BlockSpec((bb, rows, dh), lambda i: (base + i, 0, 0))],
        out_specs=pl.BlockSpec((bb, n_heads, dh), lambda i: (i, 0, 0)),
        out_shape=jax.ShapeDtypeStruct((b, n_heads, dh), BF16),
        compiler_params=_cparams(("parallel",), 4 * bb * rows * dh * 4),
        name="mem_decode_attn",
    )(q, k2, v2)


def _mem_attn_kernel(q_ref, k_ref, v_ref, o_ref, *, scale):
    s = _dot_nt(q_ref[...].astype(BF16), k_ref[...].astype(BF16)) * scale
    m = jnp.max(s, axis=1, keepdims=True)
    p = jnp.exp(s - m)
    l = jnp.sum(p, axis=1, keepdims=True)
    o_ref[...] = (_dot_nn(p.astype(BF16), v_ref[...].astype(BF16)) / l).astype(o_ref.dtype)


def _mem_attn(q, kv, *, n, t, n_mem, n_heads, dh, tq=512):
    tq = min(tq, t)
    nq = t // tq
    kern = functools.partial(_mem_attn_kernel, scale=float(dh) ** -0.5)
    return pl.pallas_call(
        kern,
        grid=(n, n_heads, nq),
        in_specs=[pl.BlockSpec((tq, dh), lambda b, h, qi: (b * nq + qi, h)),
                  pl.BlockSpec((n_mem, dh), lambda b, h, qi: (b, h)),
                  pl.BlockSpec((n_mem, dh), lambda b, h, qi: (b, n_heads + h))],
        out_specs=pl.BlockSpec((tq, dh), lambda b, h, qi: (b * nq + qi, h)),
        out_shape=jax.ShapeDtypeStruct((n * t, n_heads * dh), BF16),
        compiler_params=_cparams(("parallel", "parallel", "parallel"), 8 * tq * dh * 4 + 4 * tq * n_mem * 4),
        name="mem_attn",
    )(q, kv, kv)


def _rglru_gates(u, wa_ref, wx_ref, ba_ref, bx_ref, lam_ref):
    ub = u.astype(BF16)
    gate_a = jax.nn.sigmoid(_dot_nn(ub, wa_ref[...]) + ba_ref[...])
    gate_x = jax.nn.sigmoid(_dot_nn(ub, wx_ref[...]) + bx_ref[...])
    log_a = -RG_C * gate_a * _softplus(-lam_ref[...])
    a = jnp.exp(log_a)
    b = u * gate_x * jnp.sqrt(1.0 - jnp.exp(2.0 * log_a))
    return a, b


def _shift_rows(x, d, fill):
    t = x.shape[0]
    if d % SUBLANES == 0:
        return jnp.concatenate([jnp.full((d, x.shape[1]), fill, x.dtype), x[:t - d]], axis=0)
    rolled = pltpu.roll(x, d, 0)
    row = lax.broadcasted_iota(jnp.int32, x.shape, 0)
    return jnp.where(row < d, fill, rolled)


def _rglru_seq_kernel(xc_ref, gc_ref, cw_ref, cb_ref, wa_ref, wx_ref, ba_ref, bx_ref, lam_ref,
                      y_ref, conv_ref, h_ref, xpad_sc, *, t, conv_w):
    xpad_sc[0:SUBLANES, :] = jnp.zeros((SUBLANES, LANES), F32)
    xpad_sc[SUBLANES:, :] = xc_ref[...]
    u = cb_ref[...] + jnp.zeros((t, LANES), F32)
    for j in range(conv_w):
        u = u + xpad_sc[pl.ds(SUBLANES - (conv_w - 1) + j, t), :] * cw_ref[j:j + 1, :]
    a, b = _rglru_gates(u, wa_ref, wx_ref, ba_ref, bx_ref, lam_ref)
    d = 1
    while d < t:
        a_sh = _shift_rows(a, d, 1.0)
        b_sh = _shift_rows(b, d, 0.0)
        b = a * b_sh + b
        a = a * a_sh
        d *= 2
    y_ref[...] = (b * _gelu_tanh(gc_ref[...])).astype(y_ref.dtype)
    h_ref[...] = b[t - 1:t, :]
    conv_ref[...] = xpad_sc[pl.ds(SUBLANES + t - (conv_w - 1), conv_w - 1), :]


def _rglru_seq(z, *, n, t, d_c, xc_blk0, conv_w_arr, conv_b, wa_d, wx_d, ba, bx, lam):
    nc = d_c // LANES
    conv_w = conv_w_arr.shape[0]
    vec = lambda a: a.reshape(1, d_c)
    vspec = pl.BlockSpec((1, LANES), lambda b, c: (0, c))
    kern = functools.partial(_rglru_seq_kernel, t=t, conv_w=conv_w)
    return pl.pallas_call(
        kern,
        grid=(n, nc),
        in_specs=[pl.BlockSpec((t, LANES), lambda b, c: (b, xc_blk0 + c)),
                  pl.BlockSpec((t, LANES), lambda b, c: (b, xc_blk0 + nc + c)),
                  pl.BlockSpec((conv_w, LANES), lambda b, c: (0, c)),
                  vspec,
                  pl.BlockSpec((LANES, LANES), lambda b, c: (c, c)),
                  pl.BlockSpec((LANES, LANES), lambda b, c: (c, c)),
                  vspec, vspec, vspec],
        out_specs=[pl.BlockSpec((t, LANES), lambda b, c: (b, c)),
                   pl.BlockSpec((None, conv_w - 1, LANES), lambda b, c: (b, 0, c)),
                   pl.BlockSpec((None, 1, LANES), lambda b, c: (b, 0, c))],
        out_shape=[jax.ShapeDtypeStruct((n * t, d_c), BF16),
                   jax.ShapeDtypeStruct((n, conv_w - 1, d_c), F32),
                   jax.ShapeDtypeStruct((n, 1, d_c), F32)],
        scratch_shapes=[pltpu.VMEM((t + SUBLANES, LANES), F32)],
        compiler_params=_cparams(("parallel", "parallel"), 16 * t * LANES * 4),
        name="rglru_seq",
    )(z, z, conv_w_arr, vec(conv_b), wa_d, wx_d, vec(ba), vec(bx), vec(lam))


def _rglru_step_kernel(xc_ref, gc_ref, c0_ref, h0_ref, cw_ref, cb_ref, wa_ref, wx_ref, ba_ref, bx_ref, lam_ref,
                       y_ref, conv_ref, h_ref, *, conv_w):
    xc = xc_ref[...]
    u = cb_ref[...] + xc * cw_ref[conv_w - 1:conv_w, :]
    for j in range(conv_w - 1):
        u = u + c0_ref[:, j, :] * cw_ref[j:j + 1, :]
    a, b = _rglru_gates(u, wa_ref, wx_ref, ba_ref, bx_ref, lam_ref)
    h = a * h0_ref[...] + b
    y_ref[...] = (h * _gelu_tanh(gc_ref[...])).astype(y_ref.dtype)
    h_ref[...] = h
    for j in range(conv_w - 2):
        conv_ref[:, j, :] = c0_ref[:, j + 1, :]
    conv_ref[:, conv_w - 2, :] = xc


def _rglru_step(z, conv0, h0, *, d_c, xc_blk0, conv_w_arr, conv_b, wa_d, wx_d, ba, bx, lam):
    bsz = z.shape[0]
    conv_w = conv_w_arr.shape[0]
    ncb = d_c // LANES
    vec = lambda a: a.reshape(1, d_c)
    full = lambda shape: pl.BlockSpec(shape, lambda i: (0,) * len(shape))
    kern = functools.partial(_rglru_step_kernel, conv_w=conv_w)
    return pl.pallas_call(
        kern,
        grid=(1,),
        in_specs=[pl.BlockSpec((bsz, d_c), lambda i: (0, xc_blk0 * LANES // d_c)),
                  pl.BlockSpec((bsz, d_c), lambda i: (0, xc_blk0 * LANES // d_c + 1)),
                  full((bsz, conv_w - 1, d_c)), full((bsz, d_c)), full((conv_w, d_c)), full((1, d_c)),
                  full((d_c, d_c)), full((d_c, d_c)), full((1, d_c)), full((1, d_c)), full((1, d_c))],
        out_specs=[full((bsz, d_c)), full((bsz, conv_w - 1, d_c)), full((bsz, d_c))],
        out_shape=[jax.ShapeDtypeStruct((bsz, d_c), BF16),
                   jax.ShapeDtypeStruct((bsz, conv_w - 1, d_c), F32),
                   jax.ShapeDtypeStruct((bsz, d_c), F32)],
        compiler_params=_cparams(("arbitrary",), 16 * bsz * d_c * 4 + 4 * d_c * d_c * 2),
        name="rglru_step",
    )(z, z, conv0, h0, conv_w_arr, vec(conv_b), wa_d, wx_d, vec(ba), vec(bx), vec(lam))


def _rwkv_prep_kernel(*refs, d_b, seq_mode, blocks_per_seq):
    if seq_mode:
        zb_ref, prev_ref = refs[:2]
    else:
        zb_ref, zp_ref = refs[:2]
    (mu_ref, w0_ref, a0_ref, w2_ref, a2_ref, g2_ref, kk_ref, ka_ref, ones_ref,
     r_ref, lw_ref, km_ref, v_ref, kn_ref, bt_ref, g_ref) = refs[2:18]
    zb = zb_ref[...]
    tm = zb.shape[0]
    if seq_mode:
        sh_sc = refs[18]
        first = (pl.program_id(0) % blocks_per_seq) == 0
        prev = jnp.where(first, 0.0, prev_ref[...])
        sh_sc[0:SUBLANES, :] = prev
        sh_sc[SUBLANES:, :] = zb
        zp = sh_sc[pl.ds(SUBLANES - 1, tm), :]
    else:
        zp = zp_ref[...]
    zs = zb + mu_ref[...] * (zp - zb)
    r = zs[:, 0:d_b]
    k = zs[:, d_b:2 * d_b]
    v = zs[:, 2 * d_b:3 * d_b]
    lr = zs[:, 3 * d_b:3 * d_b + w2_ref.shape[0]]
    w_lin = _dot_nn(jnp.tanh(lr).astype(BF16), w2_ref[...])
    a_lin = _dot_nn(lr.astype(BF16), a2_ref[...])
    g = _dot_nn(jax.nn.sigmoid(lr).astype(BF16), g2_ref[...])
    w = -_softplus(-(w0_ref[...] + w_lin)) - 0.5
    a = jax.nn.sigmoid(a0_ref[...] + a_lin)
    kk = k * kk_ref[...]
    nrm2 = _dot_x_exact(kk * kk, ones_ref[...], parts=2)
    kn = kk / jnp.maximum(jnp.sqrt(nrm2), 1e-12)
    r_ref[...] = r
    lw_ref[...] = -jnp.exp(w)
    km_ref[...] = k * (1.0 + (a - 1.0) * ka_ref[...])
    v_ref[...] = v
    kn_ref[...] = kn
    bt_ref[...] = kn * a
    g_ref[...] = g


def _rwkv_prep(z, zprev, *, zb_w, d_b, mu, w0, a0, w2p, a2p, g2p, kk, ka, ones_h, seq_len, tm=256):
    m = z.shape[0]
    seq_mode = zprev is None
    tm = min(tm, seq_len if seq_mode else m)
    assert m % tm == 0
    row = lambda i: (i, 0)
    cst = lambda i: (0, 0)
    if seq_mode:
        per8 = tm // SUBLANES
        second = pl.BlockSpec((SUBLANES, zb_w), lambda i: (jnp.maximum(i * per8 - 1, 0), 0))
        second_arg = z
        scratch = [pltpu.VMEM((tm + SUBLANES, zb_w), F32)]
        bps = seq_len // tm
    else:
        second = pl.BlockSpec((tm, zb_w), row)
        second_arg = zprev
        scratch = []
        bps = 1
    lrw = w2p.shape[0]
    kern = functools.partial(_rwkv_prep_kernel, d_b=d_b, seq_mode=seq_mode, blocks_per_seq=bps)
    out = jax.ShapeDtypeStruct((m, d_b), F32)
    return pl.pallas_call(
        kern,
        grid=(m // tm,),
        in_specs=[pl.BlockSpec((tm, zb_w), row), second,
                  pl.BlockSpec((1, zb_w), cst), pl.BlockSpec((1, d_b), cst), pl.BlockSpec((1, d_b), cst),
                  pl.BlockSpec((lrw, d_b), cst), pl.BlockSpec((lrw, d_b), cst), pl.BlockSpec((lrw, d_b), cst),
                  pl.BlockSpec((1, d_b), cst), pl.BlockSpec((1, d_b), cst), pl.BlockSpec((d_b, d_b), cst)],
        out_specs=[pl.BlockSpec((tm, d_b), row)] * 7,
        out_shape=[out] * 7,
        scratch_shapes=scratch,
        compiler_params=_cparams(("parallel",), 6 * tm * zb_w * 4 + 30 * tm * d_b * 4),
        name="rwkv_prep",
    )(z, second_arg, mu, w0, a0, w2p, a2p, g2p, kk, ka, ones_h)


def _mm_p(a, b, passes, nt=False):
    if passes == 1:
        f = _dot_nt if nt else _dot_nn
        return f(a.astype(BF16), b.astype(BF16))
    return _dot3(a, b, nt=nt)


def _rwkv_chunk_kernel(*refs, c, dh, pb, nbb, passes):
    ins = [[ref.at[j] for ref in refs[:6]] for j in range(nbb)]
    y_refs = [refs[6].at[j] for j in range(nbb)]
    hout_ref, h_sc = refs[7:]
    ci = pl.program_id(2)

    @pl.when(ci == 0)
    def _():
        h_sc[...] = jnp.zeros(h_sc.shape, F32)

    c2 = 2 * c
    lane = lax.broadcasted_iota(jnp.int32, (c, LANES), 1)
    lane2 = lax.broadcasted_iota(jnp.int32, (c2, LANES), 1)
    t_idx = lax.broadcasted_iota(jnp.int32, (c, c2), 0)
    j_idx = lax.broadcasted_iota(jnp.int32, (c, c2), 1) & (c - 1)
    r128 = lax.broadcasted_iota(jnp.int32, (LANES, LANES), 0)
    c128 = lax.broadcasted_iota(jnp.int32, (LANES, LANES), 1)
    same_head = (r128 < dh) == (c128 < dh)
    zero_rows = jnp.zeros((c, c2), F32)
    head_masks = [lane2 < dh, lane2 >= dh]
    cat = lambda a, b: jnp.concatenate([a, b], axis=0)
    mm = functools.partial(_mm_p, passes=passes)

    units = []
    for j in range(nbb):
        r_ref, lw_ref, km_ref, v_ref, kn_ref, bt_ref = ins[j]
        lw_all = lw_ref[...]
        lc_all = lw_all
        d = 1
        while d < c:
            lc_all = lc_all + _shift_rows(lc_all, d, 0.0)
            d *= 2
        p_all = jnp.exp(lc_all)
        pinv_all = jnp.exp(-lc_all)
        pprev_all = jnp.exp(lc_all - lw_all)
        for pr in range(pb):
            sl = slice(pr * LANES, (pr + 1) * LANES)
            p = p_all[:, sl]
            pinv = pinv_all[:, sl]
            units.append(dict(
                ar=cat(-kn_ref[:, sl] * pprev_all[:, sl], r_ref[:, sl] * p),
                btt=bt_ref[:, sl] * pinv, kt=km_ref[:, sl] * pinv, v=v_ref[:, sl], pc=p[c - 1:c, :],
                h=h_sc[j * pb + pr], y_ref=y_refs[j], sl=sl))
    for un in units:
        un["bk"] = cat(un["btt"], un["kt"])
    arhs = [mm(un["ar"], un["h"]) for un in units]
    chains = []
    for un, arh in zip(units, arhs):
        for head in range(2):
            chains.append(dict(un=un, ah=arh[:c], rh=arh[c:], head=head))
    gs = [mm(jnp.where(head_masks[ch["head"]], ch["un"]["ar"], 0.0), ch["un"]["bk"], nt=True) for ch in chains]
    for ch, g in zip(chains, gs):
        ch["p_top"] = jnp.where(j_idx < t_idx, g[:c], 0.0)
        ch["mr"] = jnp.where(j_idx <= t_idx, g[c:], 0.0)
    upds = [mm(ch["p_top"], cat(ch["ah"], ch["un"]["v"])) for ch in chains]
    for ch, up in zip(chains, upds):
        ch["u"] = ch["ah"] + up
    for _ in range(int(np.log2(c2)) - 1):
        sq = [mm(ch["p_top"], cat(ch["p_top"], zero_rows)) for ch in chains]
        for ch, s in zip(chains, sq):
            ch["p_top"] = s
        upds = [mm(ch["p_top"], cat(ch["u"], ch["un"]["v"])) for ch in chains]
        for ch, up in zip(chains, upds):
            ch["u"] = ch["u"] + up
    yparts = [mm(ch["mr"], cat(ch["u"], ch["un"]["v"])) for ch in chains]
    ma_l = lane < dh
    for i, un in enumerate(units):
        c0, c1 = chains[2 * i], chains[2 * i + 1]
        un["u"] = jnp.where(ma_l, c0["u"], c1["u"])
        un["y_ref"][:, un["sl"]] = jnp.where(ma_l, c0["rh"] + yparts[2 * i], c1["rh"] + yparts[2 * i + 1])
    upds = [mm(cat(un["btt"] * un["pc"], un["kt"] * un["pc"]).T, cat(un["u"], un["v"])) for un in units]
    h_news = []
    for un, upd in zip(units, upds):
        pcol = jnp.broadcast_to(un["pc"], (SUBLANES, LANES)).T[:, 0:1]
        h_news.append(jnp.where(same_head, un["h"] * pcol + upd, 0.0))
    for i, h_new in enumerate(h_news):
        h_sc[i] = h_new

    @pl.when(ci == pl.num_programs(2) - 1)
    def _():
        for i, h_new in enumerate(h_news):
            hout_ref[i // pb, i % pb] = h_new


def _rwkv_chunk(r, lw, km, v, kn, bt, *, n, t, d_b, dh, pb=4, nbb=2, passes=1):
    c = RWKV_CHUNK
    assert t % c == 0 and 2 * dh == LANES and 2 * c == LANES
    npair = d_b // LANES
    pb = min(pb, npair)
    nbb = min(nbb, n)
    assert npair % pb == 0 and n % nbb == 0
    nchunk = t // c
    tok = pl.BlockSpec((nbb, c, pb * LANES), lambda b, pg, ci: (b, ci, pg))
    kern = functools.partial(_rwkv_chunk_kernel, c=c, dh=dh, pb=pb, nbb=nbb, passes=passes)
    seq = lambda a: a.reshape(n, t, d_b)
    y, h_pair = pl.pallas_call(
        kern,
        grid=(n // nbb, npair // pb, nchunk),
        in_specs=[tok] * 6,
        out_specs=[tok, pl.BlockSpec((nbb, pb, LANES, LANES), lambda b, pg, ci: (b, pg, 0, 0))],
        out_shape=[jax.ShapeDtypeStruct((n, t, d_b), F32),
                   jax.ShapeDtypeStruct((n, npair, LANES, LANES), F32)],
        scratch_shapes=[pltpu.VMEM((nbb * pb, LANES, LANES), F32)],
        compiler_params=_cparams(("parallel", "parallel", "arbitrary"), 64 * nbb * pb * LANES * LANES * 4),
        name="rwkv_chunk",
    )(seq(r), seq(lw), seq(km), seq(v), seq(kn), seq(bt))
    return y.reshape(n * t, d_b), h_pair


def _rwkv_step_kernel(r_ref, lw_ref, km_ref, v_ref, kn_ref, bt_ref, s_ref, ones_ref, y_ref, sout_ref,
                      *, bb, npair, dh):
    i2 = (lax.broadcasted_iota(jnp.int32, (dh, LANES), 1) & (dh - 1)) == \
        lax.broadcasted_iota(jnp.int32, (dh, LANES), 0)
    ones = ones_ref[...]
    units = [(b, pr, slice(b, b + 1), slice(pr * LANES, (pr + 1) * LANES)) for b in range(bb) for pr in range(npair)]
    ss = [s_ref[b, pr] for (b, pr, rb, sl) in units]
    sas = [_dot_x_exact(s * (-kn_ref[rb, sl]), ones) for s, (b, pr, rb, sl) in zip(ss, units)]
    vcols = [_dot_x_exact(jnp.where(i2, v_ref[rb, sl], 0.0), ones) for (b, pr, rb, sl) in units]
    s_news = [s * jnp.exp(lw_ref[rb, sl]) + sa * bt_ref[rb, sl] + vcol * km_ref[rb, sl]
              for s, sa, vcol, (b, pr, rb, sl) in zip(ss, sas, vcols, units)]
    ybs = [_dot_x_exact(s_new * r_ref[rb, sl], ones) for s_new, (b, pr, rb, sl) in zip(s_news, units)]
    for s_new, yb, (b, pr, rb, sl) in zip(s_news, ybs, units):
        y_ref[rb, sl] = jnp.sum(jnp.where(i2, yb, 0.0), axis=0, keepdims=True)
        sout_ref[b, pr] = s_new


def _rwkv_step(r, lw, km, v, kn, bt, s2, ones_pair, *, dh, bb=8):
    bsz, d_b = r.shape
    npair = d_b // LANES
    bb = min(bb, bsz)
    tok = pl.BlockSpec((bb, d_b), lambda i: (i, 0))
    st = pl.BlockSpec((bb, npair, dh, LANES), lambda i: (i, 0, 0, 0))
    kern = functools.partial(_rwkv_step_kernel, bb=bb, npair=npair, dh=dh)
    return pl.pallas_call(
        kern,
        grid=(bsz // bb,),
        in_specs=[tok] * 6 + [st, pl.BlockSpec((LANES, LANES), lambda i: (0, 0))],
        out_specs=[tok, st],
        out_shape=[jax.ShapeDtypeStruct((bsz, d_b), F32),
                   jax.ShapeDtypeStruct((bsz, npair, dh, LANES), F32)],
        compiler_params=_cparams(("parallel",), 8 * bb * npair * dh * LANES * 4),
        name="rwkv_step",
    )(r, lw, km, v, kn, bt, s2, ones_pair)


def _rwkv_post_kernel(y_ref, r_ref, km_ref, v_ref, g_ref, lw_ref, lb_ref, rk_ref, ones_ref, o_ref, *, dh):
    ones = ones_ref[...]
    y = y_ref[...]
    inv = 1.0 / dh
    mu = _dot_x_exact(y, ones, parts=3) * inv
    d = y - mu
    var = _dot_x_exact(d * d, ones, parts=2) * inv
    yn = d * lax.rsqrt(var + LNX_EPS) * lw_ref[...] + lb_ref[...]
    bonus = _dot_x_exact(r_ref[...] * km_ref[...] * rk_ref[...], ones, parts=3) * v_ref[...]
    o_ref[...] = ((yn + bonus) * g_ref[...]).astype(o_ref.dtype)


def _rwkv_post(y, r, km, v, g, *, lnx_w, lnx_b, rk, ones_h, dh, tm=256):
    m, d_b = y.shape
    tm = min(tm, m)
    row = pl.BlockSpec((tm, d_b), lambda i: (i, 0))
    vec = pl.BlockSpec((1, d_b), lambda i: (0, 0))
    kern = functools.partial(_rwkv_post_kernel, dh=dh)
    return pl.pallas_call(
        kern,
        grid=(m // tm,),
        in_specs=[row] * 5 + [vec] * 3 + [pl.BlockSpec((d_b, d_b), lambda i: (0, 0))],
        out_specs=row,
        out_shape=jax.ShapeDtypeStruct((m, d_b), BF16),
        compiler_params=_cparams(("parallel",), 30 * tm * d_b * 4),
        name="rwkv_post",
    )(y, r, km, v, g, lnx_w, lnx_b, rk, ones_h)


def _round_up(x, m):
    return (x + m - 1) // m * m


def _block_ones(size, blk):
    idx = np.arange(size) // blk
    return jnp.asarray(idx[:, None] == idx[None, :], dtype=BF16)


def _block_diag(w):
    nb, bs, _ = w.shape
    eye = jnp.eye(nb, dtype=w.dtype)
    return (eye[:, None, :, None] * w[:, :, None, :]).reshape(nb * bs, nb * bs)


def kernel(x_prompt, x_sample, cache_fox_k, cache_fox_v, cache_fox_logf, state_rwkv_shift, state_rwkv_wkv, state_rglru_conv, state_rglru_h, cache_mem_k, cache_mem_v, page_table, mem_prompt, norm_mix, w_in, fox_bf, rw_mu, rw_w0, rw_w2, rw_a0, rw_a2, rw_g2, rw_kk, rw_ka, rw_rk, rw_lnx_w, rw_lnx_b, rg_conv_w, rg_conv_b, rg_wa, rg_ba, rg_wx, rg_bx, rg_lambda, w_out, norm_x, norm_mem, w_xq, w_xk, w_xv, w_xo, norm_ff, w_ff1, w_ff2, norm_f):
    nb_p, t_p, d_model = x_prompt.shape
    nb_s = x_sample.shape[0]
    depth, n_phys, page, h_a, dh_a = cache_fox_k.shape
    d_a = h_a * dh_a
    n_b_cols = state_rwkv_shift.shape[-1]
    _, _, h_b, dh_b, _ = state_rwkv_wkv.shape
    d_b = h_b * dh_b
    d_c = state_rglru_h.shape[-1]
    n_mem, h_x, dh_x = cache_mem_k.shape[2:]
    d_x = h_x * dh_x
    r_dec, r_icl, r_gate = rw_w2.shape[1], rw_a2.shape[1], rw_g2.shape[1]
    lr_w = r_dec + r_icl + r_gate
    zb_w = _round_up(n_b_cols + h_a, max(d_c, LANES))
    fa_blk = n_b_cols // LANES
    q_blk0 = zb_w // LANES
    xc_blk0 = (zb_w + 3 * d_a) // LANES
    assert n_b_cols % LANES == 0 and d_a % LANES == 0 and d_c % LANES == 0 and h_a <= LANES
    assert (zb_w + 3 * d_a) % d_c == 0 and n_b_cols == 3 * d_b + lr_w and (h_a & (h_a - 1)) == 0 and (h_x & (h_x - 1)) == 0

    splits = np.cumsum([d_a, d_a, d_a, h_a, n_b_cols, d_c])
    ones_h = _block_ones(d_b, dh_b)
    ones_pair = _block_ones(LANES, dh_b)
    g_mat = jnp.asarray(np.arange(page)[:, None] <= (np.arange(page * h_a)[None, :] // h_a), dtype=BF16)

    kc = cache_fox_k.reshape(depth * n_phys, page * h_a, dh_a)
    vc = cache_fox_v.reshape(depth * n_phys, page * h_a, dh_a)
    lf_t = jnp.swapaxes(cache_fox_logf, 2, 3).reshape(depth * n_phys, h_a, page)
    mk2 = cache_mem_k.reshape(depth * nb_s, n_mem * h_x, dh_x)
    mv2 = cache_mem_v.reshape(depth * nb_s, n_mem * h_x, dh_x)

    xp = x_prompt.reshape(nb_p * t_p, d_model)
    xs = x_sample.reshape(nb_s, d_model)
    memf = mem_prompt.reshape(nb_p * n_mem, d_model)
    row1 = lambda a: a.reshape(1, -1).astype(F32)
    pad_cols = lambda a, w: jnp.pad(a, ((0, 0), (0, w - a.shape[1])))

    d_ff = w_ff1.shape[-1]
    stacked_bf16 = [w.astype(BF16) for w in
                    (w_out, w_xq, jnp.concatenate([w_xk, w_xv], axis=2), w_xo, w_ff1, w_ff2)]
    p_states, s_states = [], []
    for l in range(depth):
        wq, wk, wv, wf, wzb, wxc, wgc = jnp.split(w_in[l], splits, axis=1)
        w_in_r = jnp.concatenate(
            [wzb, wf, jnp.zeros((d_model, zb_w - n_b_cols - h_a), F32), wq, wk, wv, wxc, wgc], axis=1).astype(BF16)
        bf_pad = pad_cols(row1(fox_bf[l]), LANES)
        mu_pad = pad_cols(row1(rw_mu[l]), zb_w)
        zrow = lambda r0, w, rows: jnp.pad(w, ((r0, lr_w - r0 - rows), (0, 0))).astype(BF16)
        w2p = zrow(0, rw_w2[l], r_dec)
        a2p = zrow(r_dec, rw_a2[l], r_icl)
        g2p = zrow(r_dec + r_icl, rw_g2[l], r_gate)
        wa_d = _block_diag(rg_wa[l]).astype(BF16)
        wx_d = _block_diag(rg_wx[l]).astype(BF16)
        w_out_b, w_xq_b, w_xkv_b, w_xo_b, w_ff1_b, w_ff2_b = [(w, l) for w in stacked_bf16]
        rwkv_par = dict(zb_w=zb_w, d_b=d_b, mu=mu_pad, w0=row1(rw_w0[l]), a0=row1(rw_a0[l]), w2p=w2p, a2p=a2p,
                        g2p=g2p, kk=row1(rw_kk[l]), ka=row1(rw_ka[l]), ones_h=ones_h)
        post_par = dict(lnx_w=row1(rw_lnx_w[l]), lnx_b=row1(rw_lnx_b[l]), rk=row1(rw_rk[l]), ones_h=ones_h, dh=dh_b)
        rg_par = dict(d_c=d_c, xc_blk0=xc_blk0, conv_w_arr=rg_conv_w[l], conv_b=rg_conv_b[l], wa_d=wa_d, wx_d=wx_d,
                      ba=rg_ba[l], bx=rg_bx[l], lam=rg_lambda[l])

        def tail(x, ya, yb, yc, attend, tm, tm_wide):
            segs = [(w_out_b, d_a, 0), (w_out_b, d_b, d_a // d_b), (w_out_b, d_c, (d_a + d_b) // d_c)]
            x = _matmul([ya, yb, yc], segs, residual=x, tm=tm_wide, name="mix_out_proj")
            q = _matmul([x], [(w_xq_b, d_model, 0)], gain=norm_x[l], tm=tm_wide, name="mem_q_proj")
            o = attend(q)
            x = _matmul([o], [(w_xo_b, d_x, 0)], residual=x, tm=tm_wide, name="mem_out_proj")
            hid = _matmul([x], [(w_ff1_b, d_model, 0)], gain=norm_ff[l], epilogue="relu2", out_dtype=BF16,
                          tm=tm_wide, tn=1024, name="ff_up")
            return _matmul([hid], [(w_ff2_b, d_ff, 0)], residual=x, tm=tm, name="ff_down")

        m_p = nb_p * t_p
        mkv = _matmul([memf], [(w_xkv_b, d_model, 0)], gain=norm_mem[l], tm=512, name="mem_kv_proj")
        z = _matmul([xp], [(w_in_r, d_model, 0)], gain=norm_mix[l], tm=1024, tn=1024, name="mix_in_proj")
        lf, c_col, c_t = _logf_cumsum(z, fa_blk, bf_pad, nb_p, t_p)
        c_row = c_t[:, :h_a, :].reshape(nb_p * h_a, t_p)
        ya = _fox_prompt(z, c_col, c_row, n=nb_p, t=t_p, n_heads=h_a, dh=dh_a, q_blk0=q_blk0)
        r, lw, km, v, kn, bt, g = _rwkv_prep(z, None, seq_len=t_p, **rwkv_par)
        y, h_pair = _rwkv_chunk(r, lw, km, v, kn, bt, n=nb_p, t=t_p, d_b=d_b, dh=dh_b)
        yb = _rwkv_post(y, r, km, v, g, **post_par)
        yc, conv1, h1 = _rglru_seq(z, n=nb_p, t=t_p, **rg_par)
        xp = tail(xp, ya, yb, yc,
                  lambda q: _mem_attn(q, mkv, n=nb_p, t=t_p, n_mem=n_mem, n_heads=h_x, dh=dh_x), 512, 1024)
        z3 = z.reshape(nb_p, t_p, -1)
        hp = h_pair.reshape(nb_p, d_b // LANES, 2, dh_b, 2, dh_b)
        wkv = jnp.stack([hp[:, :, 0, :, 0, :], hp[:, :, 1, :, 1, :]], axis=2)
        wkv = jnp.swapaxes(wkv, -1, -2).reshape(nb_p, h_b, dh_b, dh_b)
        p_states.append((
            z3[:, :, zb_w + d_a:zb_w + 2 * d_a].reshape(nb_p, t_p, h_a, dh_a),
            z3[:, :, zb_w + 2 * d_a:zb_w + 3 * d_a].reshape(nb_p, t_p, h_a, dh_a),
            lf.reshape(nb_p, t_p, LANES)[:, :, :h_a],
            z3[:, t_p - 1, :n_b_cols],
            wkv,
            conv1,
            h1.reshape(nb_p, d_c),
            mkv[:, :d_x].reshape(nb_p, n_mem, h_x, dh_x),
            mkv[:, d_x:].reshape(nb_p, n_mem, h_x, dh_x),
        ))

        zs = _matmul([xs], [(w_in_r, d_model, 0)], gain=norm_mix[l], tm=nb_s, name="mix_in_proj_s")
        lf_s, _, _ = _logf_cumsum(zs, fa_blk, bf_pad, 1, nb_s)
        hd = lambda a: a.reshape(nb_s, h_a, dh_a)
        q_s = hd(zs[:, zb_w:zb_w + d_a])
        k_s = hd(zs[:, zb_w + d_a:zb_w + 2 * d_a])
        v_s = hd(zs[:, zb_w + 2 * d_a:zb_w + 3 * d_a])
        lfn = jnp.broadcast_to(lf_s[:, :h_a, None], (nb_s, h_a, dh_a))
        ya_s = _fox_decode(page_table, q_s, k_s, v_s, lfn, kc, vc, lf_t, g_mat, layer=l, n_phys=n_phys)
        ya_s = ya_s.reshape(nb_s, d_a).astype(BF16)
        zprev = pad_cols(state_rwkv_shift[l], zb_w)
        r, lw, km, v, kn, bt, g = _rwkv_prep(zs, zprev, seq_len=1, **rwkv_par)
        s2 = state_rwkv_wkv[l].reshape(nb_s, d_b // LANES, 2, dh_b, dh_b)
        s2 = jnp.swapaxes(s2, 2, 3).reshape(nb_s, d_b // LANES, dh_b, LANES)
        y, s2n = _rwkv_step(r, lw, km, v, kn, bt, s2, ones_pair, dh=dh_b)
        wkv_s = jnp.swapaxes(s2n.reshape(nb_s, d_b // LANES, dh_b, 2, dh_b), 2, 3).reshape(nb_s, h_b, dh_b, dh_b)
        yb_s = _rwkv_post(y, r, km, v, g, **post_par)
        yc_s, conv1_s, h1_s = _rglru_step(zs, state_rglru_conv[l], state_rglru_h[l], **rg_par)
        xs = tail(xs, ya_s, yb_s, yc_s,
                  lambda q: _mem_decode(q.reshape(nb_s, h_x, dh_x), mk2, mv2, layer=l).reshape(nb_s, d_x),
                  nb_s, nb_s)
        s_states.append((
            k_s.reshape(nb_s, 1, h_a, dh_a),
            v_s.reshape(nb_s, 1, h_a, dh_a),
            lf_s[:, :h_a].reshape(nb_s, 1, h_a),
            zs[:, :n_b_cols],
            wkv_s,
            conv1_s,
            h1_s,
        ))

    y_prompt = _rmsnorm(xp, norm_f).reshape(nb_p, t_p, d_model)
    y_sample = _rmsnorm(xs, norm_f).reshape(nb_s, 1, d_model)
    p_out = [jnp.stack(s) for s in zip(*p_states)]
    s_out = [jnp.stack(s) for s in zip(*s_states)]
    return (y_prompt, y_sample, *p_out, *s_out)
```

```python
import functools

import numpy as np
import jax
import jax.numpy as jnp
from jax import lax
from jax.experimental import pallas as pl
from jax.experimental.pallas import tpu as pltpu

F32 = jnp.float32
BF16 = jnp.bfloat16

NORM_EPS = 1e-6
LNX_EPS = 64e-5
RG_C = 8.0
NEG_INF = -1e30

LANES = 128
SUBLANES = 8
VMEM_CAP_BYTES = 60000 * 1024
RWKV_CHUNK = 64


def _cparams(semantics, est_bytes):
    limit = int(min(max(2 * est_bytes + (8 << 20), 24 << 20), VMEM_CAP_BYTES))
    return pltpu.CompilerParams(dimension_semantics=semantics, vmem_limit_bytes=limit)


def _split_bf16(x, parts):
    out = []
    r = x
    for i in range(parts):
        h = r.astype(BF16)
        out.append(h)
        if i + 1 < parts:
            r = r - h.astype(F32)
    return out


def _dot_nn(a, b):
    return jnp.dot(a, b, preferred_element_type=F32)


def _dot_nt(a, b):
    return lax.dot_general(a, b, (((1,), (1,)), ((), ())), preferred_element_type=F32)


def _dot_x_exact(x, w_exact, parts=3):
    return sum(_dot_nn(p, w_exact) for p in _split_bf16(x, parts))


def _dot_exact_x(w_exact, x, parts=3):
    return sum(_dot_nn(w_exact, p) for p in _split_bf16(x, parts))


def _dot3(a, b, nt=False):
    f = _dot_nt if nt else _dot_nn
    ah, al = _split_bf16(a, 2)
    bh, bl = _split_bf16(b, 2)
    return f(ah, bh) + f(ah, bl) + f(al, bh)


def _softplus(x):
    return jnp.maximum(x, 0.0) + jnp.log1p(jnp.exp(-jnp.abs(x)))


def _log_sigmoid(x):
    return -_softplus(-x)


def _gelu_tanh(x):
    c = np.float32(np.sqrt(2.0 / np.pi))
    return 0.5 * x * (1.0 + jnp.tanh(c * (x + 0.044715 * (x * x * x))))


def _mm_kernel(*refs, n_seg, has_gain, has_res, epilogue):
    xs = refs[:n_seg]
    ws = refs[n_seg:2 * n_seg]
    pos = 2 * n_seg
    g_ref = res_ref = None
    if has_gain:
        g_ref = refs[pos]
        pos += 1
    if has_res:
        res_ref = refs[pos]
        pos += 1
    o_ref = refs[pos]
    if has_gain:
        xn_ref = refs[pos + 1]

        @pl.when(pl.program_id(1) == 0)
        def _():
            x = xs[0][...]
            ms = jnp.mean(x * x, axis=-1, keepdims=True)
            xn_ref[...] = (x * lax.rsqrt(ms + NORM_EPS) * g_ref[...]).astype(BF16)

        acc = _dot_nn(xn_ref[...], ws[0][...])
    else:
        acc = _dot_nn(xs[0][...], ws[0][...])
        for x_ref, w_ref in zip(xs[1:], ws[1:]):
            acc = acc + _dot_nn(x_ref[...], w_ref[...])
    if epilogue == "relu2":
        r = jnp.maximum(acc, 0.0)
        acc = r * r
    if has_res:
        acc = acc + res_ref[...]
    o_ref[...] = acc.astype(o_ref.dtype)


def _matmul(xs, ws, *, gain=None, residual=None, epilogue="none", out_dtype=F32, tm=512, tn=512, name="mm"):
    m = xs[0].shape[0]
    w0 = ws[0][0]
    n = (w0[0] if isinstance(w0, tuple) else w0).shape[-1]
    tm = min(tm, m)
    tn = min(tn, n)
    while n % tn:
        tn //= 2
    assert m % tm == 0 and tn % LANES == 0
    in_specs, args = [], []
    est = 0
    for x in xs:
        k = x.shape[1]
        in_specs.append(pl.BlockSpec((tm, k), lambda i, j: (i, 0)))
        args.append(x)
        est += 2 * tm * k * x.dtype.itemsize
    for (w, k, rb) in ws:
        if isinstance(w, tuple):
            w, layer = w
            in_specs.append(pl.BlockSpec((None, k, tn), lambda i, j, rb=rb, layer=layer: (layer, rb, j)))
        else:
            in_specs.append(pl.BlockSpec((k, tn), lambda i, j, rb=rb: (rb, j)))
        args.append(w)
        est += 2 * k * tn * w.dtype.itemsize
    scratch = []
    if gain is not None:
        k = xs[0].shape[1]
        in_specs.append(pl.BlockSpec((1, k), lambda i, j: (0, 0)))
        args.append(gain.reshape(1, k).astype(F32))
        scratch.append(pltpu.VMEM((tm, k), BF16))
        est += tm * k * 2
    if residual is not None:
        in_specs.append(pl.BlockSpec((tm, tn), lambda i, j: (i, j)))
        args.append(residual)
        est += 2 * tm * tn * 4
    est += 3 * tm * tn * 4
    kern = functools.partial(_mm_kernel, n_seg=len(xs), has_gain=gain is not None,
                             has_res=residual is not None, epilogue=epilogue)
    return pl.pallas_call(
        kern,
        grid=(m // tm, n // tn),
        in_specs=in_specs,
        out_specs=pl.BlockSpec((tm, tn), lambda i, j: (i, j)),
        out_shape=jax.ShapeDtypeStruct((m, n), out_dtype),
        scratch_shapes=scratch,
        compiler_params=_cparams(("parallel", "arbitrary"), est),
        name=name,
    )(*args)


def _rmsnorm_kernel(x_ref, g_ref, o_ref):
    x = x_ref[...]
    ms = jnp.mean(x * x, axis=-1, keepdims=True)
    o_ref[...] = x * lax.rsqrt(ms + NORM_EPS) * g_ref[...]


def _rmsnorm(x, g, tm=512):
    m, d = x.shape
    tm = min(tm, m)
    return pl.pallas_call(
        _rmsnorm_kernel,
        grid=(m // tm,),
        in_specs=[pl.BlockSpec((tm, d), lambda i: (i, 0)), pl.BlockSpec((1, d), lambda i: (0, 0))],
        out_specs=pl.BlockSpec((tm, d), lambda i: (i, 0)),
        out_shape=jax.ShapeDtypeStruct((m, d), F32),
        compiler_params=_cparams(("parallel",), 4 * tm * d * 4),
        name="final_rmsnorm",
    )(x, g.reshape(1, d))


def _logf_kernel(fa_ref, bf_ref, lf_ref, ct_ref, *, t, blk):
    row = lax.broadcasted_iota(jnp.int32, (blk, blk), 0)
    col = lax.broadcasted_iota(jnp.int32, (blk, blk), 1)
    tri = (col <= row).astype(BF16)
    carry = jnp.zeros((1, LANES), F32)
    for b in range(t // blk):
        sl = pl.ds(b * blk, blk)
        lf = _log_sigmoid(fa_ref[sl, :] + bf_ref[...])
        lf_ref[sl, :] = lf
        c = _dot_exact_x(tri, lf) + carry
        ct_ref[:, sl] = c.T
        carry = c[blk - 1:blk, :]


def _logf_cumsum(z, fa_blk, bf_pad, n, t):
    blk = min(t, 256)
    kern = functools.partial(_logf_kernel, t=t, blk=blk)
    return pl.pallas_call(
        kern,
        grid=(n,),
        in_specs=[pl.BlockSpec((t, LANES), lambda i: (i, fa_blk)),
                  pl.BlockSpec((1, LANES), lambda i: (0, 0))],
        out_specs=[pl.BlockSpec((t, LANES), lambda i: (i, 0)),
                   pl.BlockSpec((None, LANES, t), lambda i: (i, 0, 0))],
        out_shape=[jax.ShapeDtypeStruct((n * t, LANES), F32),
                   jax.ShapeDtypeStruct((n, LANES, t), F32)],
        compiler_params=_cparams(("parallel",), 10 * t * LANES * 4),
        name="logf_cumsum",
    )(z, bf_pad)


def _fox_attn_kernel(q_ref, k_ref, v_ref, ck_ref, o_ref, m_sc, l_sc, acc_sc, *, scale, tq, dh, hb):
    qi = pl.program_id(2)
    m_sc[...] = jnp.full(m_sc.shape, NEG_INF, F32)
    l_sc[...] = jnp.zeros(l_sc.shape, F32)
    acc_sc[...] = jnp.zeros(acc_sc.shape, F32)
    cols = [slice(j * dh, (j + 1) * dh) for j in range(hb)]
    qs = [q_ref[:, cols[j]].astype(BF16) for j in range(hb)]

    def block(ki, diagonal):
        rows = pl.ds(pl.multiple_of(ki * tq, tq), tq)
        ss = [_dot_nt(qs[j], k_ref[rows, cols[j]].astype(BF16)) for j in range(hb)]
        ps, alphas = [], []
        for j in range(hb):
            s = ss[j] * scale - ck_ref[j, ki]
            if diagonal:
                causal = (lax.broadcasted_iota(jnp.int32, (tq, tq), 1)
                          <= lax.broadcasted_iota(jnp.int32, (tq, tq), 0))
                s = jnp.where(causal, s, NEG_INF)
            m_old = m_sc[j]
            m_new = jnp.maximum(m_old, jnp.max(s, axis=1, keepdims=True))
            alpha = jnp.exp(m_old - m_new)
            p = jnp.exp(s - m_new)
            l_sc[j] = alpha * l_sc[j] + jnp.sum(p, axis=1, keepdims=True)
            m_sc[j] = m_new
            ps.append(p.astype(BF16))
            alphas.append(alpha)
        pvs = [_dot_nn(ps[j], v_ref[rows, cols[j]].astype(BF16)) for j in range(hb)]
        for j in range(hb):
            acc_sc[j] = alphas[j] * acc_sc[j] + pvs[j]

    def body(ki, carry):
        block(ki, False)
        return carry

    lax.fori_loop(0, qi, body, 0)
    block(qi, True)
    for j in range(hb):
        o_ref[:, cols[j]] = (acc_sc[j] / l_sc[j]).astype(o_ref.dtype)


def _fox_prompt(z, c_row, *, n, t, n_heads, dh, q_blk0, tq=512, hb=8):
    tq = min(tq, t)
    nq = t // tq
    hb = min(hb, n_heads)
    assert n_heads % hb == 0 and q_blk0 * LANES % (hb * dh) == 0
    ng = n_heads // hb
    g0 = q_blk0 * LANES // (hb * dh)
    kern = functools.partial(_fox_attn_kernel, scale=float(dh) ** -0.5, tq=tq, dh=dh, hb=hb)
    return pl.pallas_call(
        kern,
        grid=(n, ng, nq),
        in_specs=[
            pl.BlockSpec((tq, hb * dh), lambda b, g, qi: (b * nq + qi, g0 + g)),
            pl.BlockSpec((t, hb * dh), lambda b, g, qi: (b, g0 + ng + g)),
            pl.BlockSpec((t, hb * dh), lambda b, g, qi: (b, g0 + 2 * ng + g)),
            pl.BlockSpec((hb, nq, 1, tq), lambda b, g, qi: (b * ng + g, 0, 0, 0)),
        ],
        out_specs=pl.BlockSpec((tq, hb * dh), lambda b, g, qi: (b * nq + qi, g)),
        out_shape=jax.ShapeDtypeStruct((n * t, n_heads * dh), BF16),
        scratch_shapes=[pltpu.VMEM((hb, tq, 1), F32), pltpu.VMEM((hb, tq, 1), F32),
                        pltpu.VMEM((hb, tq, dh), F32)],
        compiler_params=_cparams(("parallel", "parallel", "arbitrary"),
                                 hb * (4 * t * dh * 4 + 8 * tq * dh * 4 + 6 * tq * tq * 4)),
        name="fox_prompt_attn",
    )(z, z, z, c_row.reshape(n * n_heads, nq, 1, tq))


def _head_valid(n_heads, rows):
    lane = lax.broadcasted_iota(jnp.int32, (n_heads, rows), 1)
    sub = lax.broadcasted_iota(jnp.int32, (n_heads, rows), 0)
    return (lane & (n_heads - 1)) == sub


def _fox_decode_kernel(pt_ref, q_ref, kn_ref, vn_ref, lfn_ref, *rest, n_heads, scale, g):
    kcs, vcs, lfs = rest[0:g], rest[g:2 * g], rest[2 * g:3 * g]
    g_ref, o_ref, m_sc, l_sc, acc_sc, car_sc = rest[3 * g:]
    p = pl.program_id(1)

    @pl.when(p == 0)
    def _():
        m_sc[...] = jnp.full(m_sc.shape, NEG_INF, F32)
        l_sc[...] = jnp.zeros(l_sc.shape, F32)
        acc_sc[...] = jnp.zeros(acc_sc.shape, F32)
        car_sc[...] = jnp.zeros(car_sc.shape, F32)

    rows = kcs[0].shape[0]
    qb = q_ref[...].astype(BF16)
    valid = _head_valid(n_heads, rows)
    f_all = jnp.concatenate([lf[...] for lf in lfs], axis=0)
    cum_all = _dot_x_exact(f_all, g_ref[...])
    car = car_sc[...]
    ss = []
    for j in range(g):
        s = _dot_nt(qb, kcs[j][...].astype(BF16)) * scale
        ss.append(jnp.where(valid, s - (car + cum_all[j * n_heads:(j + 1) * n_heads]), NEG_INF))
        car = car + jnp.sum(lfs[j][...], axis=1, keepdims=True)
    car_new = car
    m_old = m_sc[...]
    m_new = m_old
    for s in ss:
        m_new = jnp.maximum(m_new, jnp.max(s, axis=1, keepdims=True))
    alpha = jnp.exp(m_old - m_new)
    l_new = alpha * l_sc[...]
    acc_new = alpha * acc_sc[...]
    for j in range(g):
        pr = jnp.exp(ss[j] - m_new)
        l_new = l_new + jnp.sum(pr, axis=1, keepdims=True)
        acc_new = acc_new + _dot_nn(pr.astype(BF16), vcs[j][...].astype(BF16))
    m_sc[...] = m_new
    l_sc[...] = l_new
    acc_sc[...] = acc_new
    car_sc[...] = car_new

    @pl.when(p == pl.num_programs(1) - 1)
    def _():
        s_new = jnp.sum(q_ref[...] * kn_ref[...], axis=1, keepdims=True) * scale - (car_new + lfn_ref[:, 0:1])
        m2 = jnp.maximum(m_new, s_new)
        a2 = jnp.exp(m_new - m2)
        pn = jnp.exp(s_new - m2)
        o_ref[...] = (a2 * acc_new + pn * vn_ref[...]) / (a2 * l_new + pn)


def _fox_decode(page_table, q, k_new, v_new, lf_new, kc, vc, lf_t, g_mat, *, layer, n_phys, g=8):
    b, n_heads, dh = q.shape
    n_pages = page_table.shape[1]
    g = min(g, n_pages)
    assert n_pages % g == 0
    rows = kc.shape[1]
    page = lf_t.shape[2]
    base = layer * n_phys
    tok = pl.BlockSpec((None, n_heads, dh), lambda i, p, pt: (i, 0, 0))
    pg = lambda j: (lambda i, p, pt: (base + pt[i, p * g + j], 0, 0))
    kern = functools.partial(_fox_decode_kernel, n_heads=n_heads, scale=float(dh) ** -0.5, g=g)
    return pl.pallas_call(
        kern,
        grid_spec=pltpu.PrefetchScalarGridSpec(
            num_scalar_prefetch=1,
            grid=(b, n_pages // g),
            in_specs=([tok] * 4
                      + [pl.BlockSpec((None, rows, dh), pg(j)) for j in range(g)]
                      + [pl.BlockSpec((None, rows, dh), pg(j)) for j in range(g)]
                      + [pl.BlockSpec((None, n_heads, page), pg(j)) for j in range(g)]
                      + [pl.BlockSpec((page, rows), lambda i, p, pt: (0, 0))]),
            out_specs=tok,
            scratch_shapes=[pltpu.VMEM((n_heads, 1), F32), pltpu.VMEM((n_heads, 1), F32),
                            pltpu.VMEM((n_heads, dh), F32), pltpu.VMEM((n_heads, 1), F32)],
        ),
        out_shape=jax.ShapeDtypeStruct((b, n_heads, dh), F32),
        compiler_params=_cparams(("parallel", "arbitrary"), g * 4 * rows * dh * 4 + 2 * page * rows * 2),
        name="fox_decode_attn",
    )(page_table, q, k_new, v_new, lf_new, *([kc] * g), *([vc] * g), *([lf_t] * g), g_mat)


def _mem_decode_kernel(q_ref, k_ref, v_ref, o_ref, *, n_heads, scale, bb):
    rows = k_ref.shape[1]
    valid = _head_valid(n_heads, rows)
    ss = [_dot_nt(q_ref[j].astype(BF16), k_ref[j].astype(BF16)) for j in range(bb)]
    ps, ls = [], []
    for s in ss:
        s = jnp.where(valid, s * scale, NEG_INF)
        p = jnp.exp(s - jnp.max(s, axis=1, keepdims=True))
        ls.append(jnp.sum(p, axis=1, keepdims=True))
        ps.append(p.astype(BF16))
    os_ = [_dot_nn(ps[j], v_ref[j].astype(BF16)) for j in range(bb)]
    for j in range(bb):
        o_ref[j] = (os_[j] / ls[j]).astype(o_ref.dtype)


def _mem_decode(q, k2, v2, *, layer, bb=4):
    b, n_heads, dh = q.shape
    rows = k2.shape[1]
    bb = min(bb, b)
    assert b % bb == 0
    base = layer * b // bb
    kern = functools.partial(_mem_decode_kernel, n_heads=n_heads, scale=float(dh) ** -0.5, bb=bb)
    return pl.pallas_call(
        kern,
        grid=(b // bb,),
        in_specs=[pl.BlockSpec((bb, n_heads, dh), lambda i: (i, 0, 0)),
                  pl.BlockSpec((bb, rows, dh), lambda i: (base + i, 0, 0)),
                  pl.BlockSpec((bb, rows, dh), lambda i: (base + i, 0, 0))],
        out_specs=pl.BlockSpec((bb, n_heads, dh), lambda i: (i, 0, 0)),
        out_shape=jax.ShapeDtypeStruct((b, n_heads, dh), BF16),
        compiler_params=_cparams(("parallel",), 4 * bb * rows * dh * 4),
        name="mem_decode_attn",
    )(q, k2, v2)


def _mem_attn_kernel(q_ref, k_ref, v_ref, o_ref, *, scale):
    s = _dot_nt(q_ref[...].astype(BF16), k_ref[...].astype(BF16)) * scale
    m = jnp.max(s, axis=1, keepdims=True)
    p = jnp.exp(s - m)
    l = jnp.sum(p, axis=1, keepdims=True)
    o_ref[...] = (_dot_nn(p.astype(BF16), v_ref[...].astype(BF16)) / l).astype(o_ref.dtype)


def _mem_attn(q, kv, *, n, t, n_mem, n_heads, dh, tq=512):
    tq = min(tq, t)
    nq = t // tq
    kern = functools.partial(_mem_attn_kernel, scale=float(dh) ** -0.5)
    return pl.pallas_call(
        kern,
        grid=(n, n_heads, nq),
        in_specs=[pl.BlockSpec((tq, dh), lambda b, h, qi: (b * nq + qi, h)),
                  pl.BlockSpec((n_mem, dh), lambda b, h, qi: (b, h)),
                  pl.BlockSpec((n_mem, dh), lambda b, h, qi: (b, n_heads + h))],
        out_specs=pl.BlockSpec((tq, dh), lambda b, h, qi: (b * nq + qi, h)),
        out_shape=jax.ShapeDtypeStruct((n * t, n_heads * dh), BF16),
        compiler_params=_cparams(("parallel", "parallel", "parallel"), 8 * tq * dh * 4 + 4 * tq * n_mem * 4),
        name="mem_attn",
    )(q, kv, kv)


def _rglru_gates(u, wa_ref, wx_ref, ba_ref, bx_ref, lam_ref):
    ub = u.astype(BF16)
    gate_a = jax.nn.sigmoid(_dot_nn(ub, wa_ref[...]) + ba_ref[...])
    gate_x = jax.nn.sigmoid(_dot_nn(ub, wx_ref[...]) + bx_ref[...])
    log_a = -RG_C * gate_a * _softplus(-lam_ref[...])
    a = jnp.exp(log_a)
    b = u * gate_x * jnp.sqrt(1.0 - jnp.exp(2.0 * log_a))
    return a, b


def _shift_rows(x, d, fill):
    t = x.shape[0]
    if d % SUBLANES == 0:
        return jnp.concatenate([jnp.full((d, x.shape[1]), fill, x.dtype), x[:t - d]], axis=0)
    rolled = pltpu.roll(x, d, 0)
    row = lax.broadcasted_iota(jnp.int32, x.shape, 0)
    return jnp.where(row < d, fill, rolled)


def _rglru_seq_kernel(xc_ref, gc_ref, cw_ref, cb_ref, wa_ref, wx_ref, ba_ref, bx_ref, lam_ref,
                      y_ref, conv_ref, h_ref, xpad_sc, *, t, conv_w):
    xpad_sc[0:SUBLANES, :] = jnp.zeros((SUBLANES, LANES), F32)
    xpad_sc[SUBLANES:, :] = xc_ref[...]
    u = cb_ref[...] + jnp.zeros((t, LANES), F32)
    for j in range(conv_w):
        u = u + xpad_sc[pl.ds(SUBLANES - (conv_w - 1) + j, t), :] * cw_ref[j:j + 1, :]
    a, b = _rglru_gates(u, wa_ref, wx_ref, ba_ref, bx_ref, lam_ref)
    d = 1
    while d < t:
        a_sh = _shift_rows(a, d, 1.0)
        b_sh = _shift_rows(b, d, 0.0)
        b = a * b_sh + b
        a = a * a_sh
        d *= 2
    y_ref[...] = (b * _gelu_tanh(gc_ref[...])).astype(y_ref.dtype)
    h_ref[...] = b[t - 1:t, :]
    conv_ref[...] = xpad_sc[pl.ds(SUBLANES + t - (conv_w - 1), conv_w - 1), :]


def _rglru_seq(z, *, n, t, d_c, xc_blk0, conv_w_arr, conv_b, wa_d, wx_d, ba, bx, lam):
    nc = d_c // LANES
    conv_w = conv_w_arr.shape[0]
    vec = lambda a: a.reshape(1, d_c)
    vspec = pl.BlockSpec((1, LANES), lambda b, c: (0, c))
    kern = functools.partial(_rglru_seq_kernel, t=t, conv_w=conv_w)
    return pl.pallas_call(
        kern,
        grid=(n, nc),
        in_specs=[pl.BlockSpec((t, LANES), lambda b, c: (b, xc_blk0 + c)),
                  pl.BlockSpec((t, LANES), lambda b, c: (b, xc_blk0 + nc + c)),
                  pl.BlockSpec((conv_w, LANES), lambda b, c: (0, c)),
                  vspec,
                  pl.BlockSpec((LANES, LANES), lambda b, c: (c, c)),
                  pl.BlockSpec((LANES, LANES), lambda b, c: (c, c)),
                  vspec, vspec, vspec],
        out_specs=[pl.BlockSpec((t, LANES), lambda b, c: (b, c)),
                   pl.BlockSpec((None, conv_w - 1, LANES), lambda b, c: (b, 0, c)),
                   pl.BlockSpec((None, 1, LANES), lambda b, c: (b, 0, c))],
        out_shape=[jax.ShapeDtypeStruct((n * t, d_c), BF16),
                   jax.ShapeDtypeStruct((n, conv_w - 1, d_c), F32),
                   jax.ShapeDtypeStruct((n, 1, d_c), F32)],
        scratch_shapes=[pltpu.VMEM((t + SUBLANES, LANES), F32)],
        compiler_params=_cparams(("parallel", "parallel"), 16 * t * LANES * 4),
        name="rglru_seq",
    )(z, z, conv_w_arr, vec(conv_b), wa_d, wx_d, vec(ba), vec(bx), vec(lam))


def _rglru_step_kernel(xc_ref, gc_ref, c0_ref, h0_ref, cw_ref, cb_ref, wa_ref, wx_ref, ba_ref, bx_ref, lam_ref,
                       y_ref, conv_ref, h_ref, *, conv_w):
    xc = xc_ref[...]
    u = cb_ref[...] + xc * cw_ref[conv_w - 1:conv_w, :]
    for j in range(conv_w - 1):
        u = u + c0_ref[:, j, :] * cw_ref[j:j + 1, :]
    a, b = _rglru_gates(u, wa_ref, wx_ref, ba_ref, bx_ref, lam_ref)
    h = a * h0_ref[...] + b
    y_ref[...] = (h * _gelu_tanh(gc_ref[...])).astype(y_ref.dtype)
    h_ref[...] = h
    for j in range(conv_w - 2):
        conv_ref[:, j, :] = c0_ref[:, j + 1, :]
    conv_ref[:, conv_w - 2, :] = xc


def _rglru_step(z, conv0, h0, *, d_c, xc_blk0, conv_w_arr, conv_b, wa_d, wx_d, ba, bx, lam):
    bsz = z.shape[0]
    conv_w = conv_w_arr.shape[0]
    ncb = d_c // LANES
    vec = lambda a: a.reshape(1, d_c)
    full = lambda shape: pl.BlockSpec(shape, lambda i: (0,) * len(shape))
    kern = functools.partial(_rglru_step_kernel, conv_w=conv_w)
    return pl.pallas_call(
        kern,
        grid=(1,),
        in_specs=[pl.BlockSpec((bsz, d_c), lambda i: (0, xc_blk0 * LANES // d_c)),
                  pl.BlockSpec((bsz, d_c), lambda i: (0, xc_blk0 * LANES // d_c + 1)),
                  full((bsz, conv_w - 1, d_c)), full((bsz, d_c)), full((conv_w, d_c)), full((1, d_c)),
                  full((d_c, d_c)), full((d_c, d_c)), full((1, d_c)), full((1, d_c)), full((1, d_c))],
        out_specs=[full((bsz, d_c)), full((bsz, conv_w - 1, d_c)), full((bsz, d_c))],
        out_shape=[jax.ShapeDtypeStruct((bsz, d_c), BF16),
                   jax.ShapeDtypeStruct((bsz, conv_w - 1, d_c), F32),
                   jax.ShapeDtypeStruct((bsz, d_c), F32)],
        compiler_params=_cparams(("arbitrary",), 16 * bsz * d_c * 4 + 4 * d_c * d_c * 2),
        name="rglru_step",
    )(z, z, conv0, h0, conv_w_arr, vec(conv_b), wa_d, wx_d, vec(ba), vec(bx), vec(lam))


def _rwkv_prep_kernel(*refs, d_b, seq_mode, blocks_per_seq):
    if seq_mode:
        zb_ref, prev_ref = refs[:2]
    else:
        zb_ref, zp_ref = refs[:2]
    (mu_ref, w0_ref, a0_ref, w2_ref, a2_ref, g2_ref, kk_ref, ka_ref, ones_ref,
     r_ref, lw_ref, km_ref, v_ref, kn_ref, bt_ref, g_ref) = refs[2:18]
    zb = zb_ref[...]
    tm = zb.shape[0]
    if seq_mode:
        sh_sc = refs[18]
        first = (pl.program_id(0) % blocks_per_seq) == 0
        prev = jnp.where(first, 0.0, prev_ref[...])
        sh_sc[0:SUBLANES, :] = prev
        sh_sc[SUBLANES:, :] = zb
        zp = sh_sc[pl.ds(SUBLANES - 1, tm), :]
    else:
        zp = zp_ref[...]
    zs = zb + mu_ref[...] * (zp - zb)
    r = zs[:, 0:d_b]
    k = zs[:, d_b:2 * d_b]
    v = zs[:, 2 * d_b:3 * d_b]
    lr = zs[:, 3 * d_b:3 * d_b + w2_ref.shape[0]]
    w_lin = _dot_nn(jnp.tanh(lr).astype(BF16), w2_ref[...])
    a_lin = _dot_nn(lr.astype(BF16), a2_ref[...])
    g = _dot_nn(jax.nn.sigmoid(lr).astype(BF16), g2_ref[...])
    w = -_softplus(-(w0_ref[...] + w_lin)) - 0.5
    a = jax.nn.sigmoid(a0_ref[...] + a_lin)
    kk = k * kk_ref[...]
    nrm2 = _dot_x_exact(kk * kk, ones_ref[...], parts=2)
    kn = kk / jnp.maximum(jnp.sqrt(nrm2), 1e-12)
    r_ref[...] = r
    lw_ref[...] = -jnp.exp(w)
    km_ref[...] = k * (1.0 + (a - 1.0) * ka_ref[...])
    v_ref[...] = v
    kn_ref[...] = kn
    bt_ref[...] = kn * a
    g_ref[...] = g


def _rwkv_prep(z, zprev, *, zb_w, d_b, mu, w0, a0, w2p, a2p, g2p, kk, ka, ones_h, seq_len, tm=256):
    m = z.shape[0]
    seq_mode = zprev is None
    tm = min(tm, seq_len if seq_mode else m)
    assert m % tm == 0
    row = lambda i: (i, 0)
    cst = lambda i: (0, 0)
    if seq_mode:
        per8 = tm // SUBLANES
        second = pl.BlockSpec((SUBLANES, zb_w), lambda i: (jnp.maximum(i * per8 - 1, 0), 0))
        second_arg = z
        scratch = [pltpu.VMEM((tm + SUBLANES, zb_w), F32)]
        bps = seq_len // tm
    else:
        second = pl.BlockSpec((tm, zb_w), row)
        second_arg = zprev
        scratch = []
        bps = 1
    lrw = w2p.shape[0]
    kern = functools.partial(_rwkv_prep_kernel, d_b=d_b, seq_mode=seq_mode, blocks_per_seq=bps)
    out = jax.ShapeDtypeStruct((m, d_b), F32)
    return pl.pallas_call(
        kern,
        grid=(m // tm,),
        in_specs=[pl.BlockSpec((tm, zb_w), row), second,
                  pl.BlockSpec((1, zb_w), cst), pl.BlockSpec((1, d_b), cst), pl.BlockSpec((1, d_b), cst),
                  pl.BlockSpec((lrw, d_b), cst), pl.BlockSpec((lrw, d_b), cst), pl.BlockSpec((lrw, d_b), cst),
                  pl.BlockSpec((1, d_b), cst), pl.BlockSpec((1, d_b), cst), pl.BlockSpec((d_b, d_b), cst)],
        out_specs=[pl.BlockSpec((tm, d_b), row)] * 7,
        out_shape=[out] * 7,
        scratch_shapes=scratch,
        compiler_params=_cparams(("parallel",), 6 * tm * zb_w * 4 + 30 * tm * d_b * 4),
        name="rwkv_prep",
    )(z, second_arg, mu, w0, a0, w2p, a2p, g2p, kk, ka, ones_h)


def _mm_p(a, b, passes, nt=False):
    if passes == 1:
        f = _dot_nt if nt else _dot_nn
        return f(a.astype(BF16), b.astype(BF16))
    return _dot3(a, b, nt=nt)


def _rwkv_chunk_kernel(*refs, c, dh, pb, nbb, passes):
    ins = [[ref.at[j] for ref in refs[:6]] for j in range(nbb)]
    y_refs = [refs[6].at[j] for j in range(nbb)]
    hout_ref, h_sc = refs[7:]
    ci = pl.program_id(2)

    @pl.when(ci == 0)
    def _():
        h_sc[...] = jnp.zeros(h_sc.shape, F32)

    c2 = 2 * c
    lane = lax.broadcasted_iota(jnp.int32, (c, LANES), 1)
    lane2 = lax.broadcasted_iota(jnp.int32, (c2, LANES), 1)
    t_idx = lax.broadcasted_iota(jnp.int32, (c, c2), 0)
    j_idx = lax.broadcasted_iota(jnp.int32, (c, c2), 1) & (c - 1)
    r128 = lax.broadcasted_iota(jnp.int32, (LANES, LANES), 0)
    c128 = lax.broadcasted_iota(jnp.int32, (LANES, LANES), 1)
    same_head = (r128 < dh) == (c128 < dh)
    zero_rows = jnp.zeros((c, c2), F32)
    head_masks = [lane2 < dh, lane2 >= dh]
    cat = lambda a, b: jnp.concatenate([a, b], axis=0)
    mm = functools.partial(_mm_p, passes=passes)

    units = []
    for j in range(nbb):
        r_ref, lw_ref, km_ref, v_ref, kn_ref, bt_ref = ins[j]
        lw_all = lw_ref[...]
        lc_all = lw_all
        d = 1
        while d < c:
            lc_all = lc_all + _shift_rows(lc_all, d, 0.0)
            d *= 2
        p_all = jnp.exp(lc_all)
        pinv_all = jnp.exp(-lc_all)
        pprev_all = jnp.exp(lc_all - lw_all)
        for pr in range(pb):
            sl = slice(pr * LANES, (pr + 1) * LANES)
            p = p_all[:, sl]
            pinv = pinv_all[:, sl]
            units.append(dict(
                ar=cat(-kn_ref[:, sl] * pprev_all[:, sl], r_ref[:, sl] * p),
                btt=bt_ref[:, sl] * pinv, kt=km_ref[:, sl] * pinv, v=v_ref[:, sl], pc=p[c - 1:c, :],
                h=h_sc[j * pb + pr], y_ref=y_refs[j], sl=sl))
    for un in units:
        un["bk"] = cat(un["btt"], un["kt"])
    arhs = [mm(un["ar"], un["h"]) for un in units]
    chains = []
    for un, arh in zip(units, arhs):
        for head in range(2):
            chains.append(dict(un=un, ah=arh[:c], rh=arh[c:], head=head))
    gs = [mm(jnp.where(head_masks[ch["head"]], ch["un"]["ar"], 0.0), ch["un"]["bk"], nt=True) for ch in chains]
    for ch, g in zip(chains, gs):
        ch["p_top"] = jnp.where(j_idx < t_idx, g[:c], 0.0)
        ch["mr"] = jnp.where(j_idx <= t_idx, g[c:], 0.0)
    upds = [mm(ch["p_top"], cat(ch["ah"], ch["un"]["v"])) for ch in chains]
    for ch, up in zip(chains, upds):
        ch["u"] = ch["ah"] + up
    for _ in range(int(np.log2(c2)) - 1):
        sq = [mm(ch["p_top"], cat(ch["p_top"], zero_rows)) for ch in chains]
        for ch, s in zip(chains, sq):
            ch["p_top"] = s
        upds = [mm(ch["p_top"], cat(ch["u"], ch["un"]["v"])) for ch in chains]
        for ch, up in zip(chains, upds):
            ch["u"] = ch["u"] + up
    yparts = [mm(ch["mr"], cat(ch["u"], ch["un"]["v"])) for ch in chains]
    ma_l = lane < dh
    for i, un in enumerate(units):
        c0, c1 = chains[2 * i], chains[2 * i + 1]
        un["u"] = jnp.where(ma_l, c0["u"], c1["u"])
        un["y_ref"][:, un["sl"]] = jnp.where(ma_l, c0["rh"] + yparts[2 * i], c1["rh"] + yparts[2 * i + 1])
    upds = [mm(cat(un["btt"] * un["pc"], un["kt"] * un["pc"]).T, cat(un["u"], un["v"])) for un in units]
    h_news = []
    for un, upd in zip(units, upds):
        pcol = jnp.broadcast_to(un["pc"], (SUBLANES, LANES)).T[:, 0:1]
        h_news.append(jnp.where(same_head, un["h"] * pcol + upd, 0.0))
    for i, h_new in enumerate(h_news):
        h_sc[i] = h_new

    @pl.when(ci == pl.num_programs(2) - 1)
    def _():
        for i, h_new in enumerate(h_news):
            hout_ref[i // pb, i % pb] = h_new


def _rwkv_chunk(r, lw, km, v, kn, bt, *, n, t, d_b, dh, pb=4, nbb=2, passes=1):
    c = RWKV_CHUNK
    assert t % c == 0 and 2 * dh == LANES and 2 * c == LANES
    npair = d_b // LANES
    pb = min(pb, npair)
    nbb = min(nbb, n)
    assert npair % pb == 0 and n % nbb == 0
    nchunk = t // c
    tok = pl.BlockSpec((nbb, c, pb * LANES), lambda b, pg, ci: (b, ci, pg))
    kern = functools.partial(_rwkv_chunk_kernel, c=c, dh=dh, pb=pb, nbb=nbb, passes=passes)
    seq = lambda a: a.reshape(n, t, d_b)
    y, h_pair = pl.pallas_call(
        kern,
        grid=(n // nbb, npair // pb, nchunk),
        in_specs=[tok] * 6,
        out_specs=[tok, pl.BlockSpec((nbb, pb, LANES, LANES), lambda b, pg, ci: (b, pg, 0, 0))],
        out_shape=[jax.ShapeDtypeStruct((n, t, d_b), F32),
                   jax.ShapeDtypeStruct((n, npair, LANES, LANES), F32)],
        scratch_shapes=[pltpu.VMEM((nbb * pb, LANES, LANES), F32)],
        compiler_params=_cparams(("parallel", "parallel", "arbitrary"), 64 * nbb * pb * LANES * LANES * 4),
        name="rwkv_chunk",
    )(seq(r), seq(lw), seq(km), seq(v), seq(kn), seq(bt))
    return y.reshape(n * t, d_b), h_pair


def _rwkv_step_kernel(r_ref, lw_ref, km_ref, v_ref, kn_ref, bt_ref, s_ref, ones_ref, y_ref, sout_ref,
                      *, bb, npair, dh):
    i2 = (lax.broadcasted_iota(jnp.int32, (dh, LANES), 1) & (dh - 1)) == \
        lax.broadcasted_iota(jnp.int32, (dh, LANES), 0)
    ones = ones_ref[...]
    units = [(b, pr, slice(b, b + 1), slice(pr * LANES, (pr + 1) * LANES)) for b in range(bb) for pr in range(npair)]
    ss = [s_ref[b, pr] for (b, pr, rb, sl) in units]
    sas = [_dot_x_exact(s * (-kn_ref[rb, sl]), ones) for s, (b, pr, rb, sl) in zip(ss, units)]
    vcols = [_dot_x_exact(jnp.where(i2, v_ref[rb, sl], 0.0), ones) for (b, pr, rb, sl) in units]
    s_news = [s * jnp.exp(lw_ref[rb, sl]) + sa * bt_ref[rb, sl] + vcol * km_ref[rb, sl]
              for s, sa, vcol, (b, pr, rb, sl) in zip(ss, sas, vcols, units)]
    ybs = [_dot_x_exact(s_new * r_ref[rb, sl], ones) for s_new, (b, pr, rb, sl) in zip(s_news, units)]
    for s_new, yb, (b, pr, rb, sl) in zip(s_news, ybs, units):
        y_ref[rb, sl] = jnp.sum(jnp.where(i2, yb, 0.0), axis=0, keepdims=True)
        sout_ref[b, pr] = s_new


def _rwkv_step(r, lw, km, v, kn, bt, s2, ones_pair, *, dh, bb=8):
    bsz, d_b = r.shape
    npair = d_b // LANES
    bb = min(bb, bsz)
    tok = pl.BlockSpec((bb, d_b), lambda i: (i, 0))
    st = pl.BlockSpec((bb, npair, dh, LANES), lambda i: (i, 0, 0, 0))
    kern = functools.partial(_rwkv_step_kernel, bb=bb, npair=npair, dh=dh)
    return pl.pallas_call(
        kern,
        grid=(bsz // bb,),
        in_specs=[tok] * 6 + [st, pl.BlockSpec((LANES, LANES), lambda i: (0, 0))],
        out_specs=[tok, st],
        out_shape=[jax.ShapeDtypeStruct((bsz, d_b), F32),
                   jax.ShapeDtypeStruct((bsz, npair, dh, LANES), F32)],
        compiler_params=_cparams(("parallel",), 8 * bb * npair * dh * LANES * 4),
        name="rwkv_step",
    )(r, lw, km, v, kn, bt, s2, ones_pair)


def _rwkv_post_kernel(y_ref, r_ref, km_ref, v_ref, g_ref, lw_ref, lb_ref, rk_ref, ones_ref, o_ref, *, dh):
    ones = ones_ref[...]
    y = y_ref[...]
    inv = 1.0 / dh
    mu = _dot_x_exact(y, ones, parts=3) * inv
    d = y - mu
    var = _dot_x_exact(d * d, ones, parts=2) * inv
    yn = d * lax.rsqrt(var + LNX_EPS) * lw_ref[...] + lb_ref[...]
    bonus = _dot_x_exact(r_ref[...] * km_ref[...] * rk_ref[...], ones, parts=3) * v_ref[...]
    o_ref[...] = ((yn + bonus) * g_ref[...]).astype(o_ref.dtype)


def _rwkv_post(y, r, km, v, g, *, lnx_w, lnx_b, rk, ones_h, dh, tm=256):
    m, d_b = y.shape
    tm = min(tm, m)
    row = pl.BlockSpec((tm, d_b), lambda i: (i, 0))
    vec = pl.BlockSpec((1, d_b), lambda i: (0, 0))
    kern = functools.partial(_rwkv_post_kernel, dh=dh)
    return pl.pallas_call(
        kern,
        grid=(m // tm,),
        in_specs=[row] * 5 + [vec] * 3 + [pl.BlockSpec((d_b, d_b), lambda i: (0, 0))],
        out_specs=row,
        out_shape=jax.ShapeDtypeStruct((m, d_b), BF16),
        compiler_params=_cparams(("parallel",), 30 * tm * d_b * 4),
        name="rwkv_post",
    )(y, r, km, v, g, lnx_w, lnx_b, rk, ones_h)


def _round_up(x, m):
    return (x + m - 1) // m * m


def _block_ones(size, blk):
    idx = np.arange(size) // blk
    return jnp.asarray(idx[:, None] == idx[None, :], dtype=BF16)


def _block_diag(w):
    nb, bs, _ = w.shape
    eye = jnp.eye(nb, dtype=w.dtype)
    return (eye[:, None, :, None] * w[:, :, None, :]).reshape(nb * bs, nb * bs)


def kernel(x_prompt, x_sample, cache_fox_k, cache_fox_v, cache_fox_logf, state_rwkv_shift, state_rwkv_wkv, state_rglru_conv, state_rglru_h, cache_mem_k, cache_mem_v, page_table, mem_prompt, norm_mix, w_in, fox_bf, rw_mu, rw_w0, rw_w2, rw_a0, rw_a2, rw_g2, rw_kk, rw_ka, rw_rk, rw_lnx_w, rw_lnx_b, rg_conv_w, rg_conv_b, rg_wa, rg_ba, rg_wx, rg_bx, rg_lambda, w_out, norm_x, norm_mem, w_xq, w_xk, w_xv, w_xo, norm_ff, w_ff1, w_ff2, norm_f):
    nb_p, t_p, d_model = x_prompt.shape
    nb_s = x_sample.shape[0]
    depth, n_phys, page, h_a, dh_a = cache_fox_k.shape
    d_a = h_a * dh_a
    n_b_cols = state_rwkv_shift.shape[-1]
    _, _, h_b, dh_b, _ = state_rwkv_wkv.shape
    d_b = h_b * dh_b
    d_c = state_rglru_h.shape[-1]
    n_mem, h_x, dh_x = cache_mem_k.shape[2:]
    d_x = h_x * dh_x
    r_dec, r_icl, r_gate = rw_w2.shape[1], rw_a2.shape[1], rw_g2.shape[1]
    lr_w = r_dec + r_icl + r_gate
    zb_w = _round_up(n_b_cols + h_a, max(d_c, LANES))
    fa_blk = n_b_cols // LANES
    q_blk0 = zb_w // LANES
    xc_blk0 = (zb_w + 3 * d_a) // LANES
    assert n_b_cols % LANES == 0 and d_a % LANES == 0 and d_c % LANES == 0 and h_a <= LANES
    assert (zb_w + 3 * d_a) % d_c == 0 and n_b_cols == 3 * d_b + lr_w and (h_a & (h_a - 1)) == 0 and (h_x & (h_x - 1)) == 0

    splits = np.cumsum([d_a, d_a, d_a, h_a, n_b_cols, d_c])
    ones_h = _block_ones(d_b, dh_b)
    ones_pair = _block_ones(LANES, dh_b)
    g_mat = jnp.asarray(np.arange(page)[:, None] <= (np.arange(page * h_a)[None, :] // h_a), dtype=BF16)

    kc = cache_fox_k.reshape(depth * n_phys, page * h_a, dh_a)
    vc = cache_fox_v.reshape(depth * n_phys, page * h_a, dh_a)
    lf_t = jnp.swapaxes(cache_fox_logf, 2, 3).reshape(depth * n_phys, h_a, page)
    mk2 = cache_mem_k.reshape(depth * nb_s, n_mem * h_x, dh_x)
    mv2 = cache_mem_v.reshape(depth * nb_s, n_mem * h_x, dh_x)

    xp = x_prompt.reshape(nb_p * t_p, d_model)
    xs = x_sample.reshape(nb_s, d_model)
    memf = mem_prompt.reshape(nb_p * n_mem, d_model)
    row1 = lambda a: a.reshape(1, -1).astype(F32)
    pad_cols = lambda a, w: jnp.pad(a, ((0, 0), (0, w - a.shape[1])))

    d_ff = w_ff1.shape[-1]
    stacked_bf16 = [w.astype(BF16) for w in
                    (w_out, w_xq, jnp.concatenate([w_xk, w_xv], axis=2), w_xo, w_ff1, w_ff2)]
    wq, wk, wv, wf, wzb, wxc, wgc = jnp.split(w_in.astype(BF16), splits, axis=2)
    w_in_all = jnp.concatenate(
        [wzb, wf, jnp.zeros((depth, d_model, zb_w - n_b_cols - h_a), BF16), wq, wk, wv, wxc, wgc], axis=2)
    p_states, s_states = [], []
    for l in range(depth):
        w_in_r = (w_in_all, l)
        bf_pad = pad_cols(row1(fox_bf[l]), LANES)
        mu_pad = pad_cols(row1(rw_mu[l]), zb_w)
        zrow = lambda r0, w, rows: jnp.pad(w, ((r0, lr_w - r0 - rows), (0, 0))).astype(BF16)
        w2p = zrow(0, rw_w2[l], r_dec)
        a2p = zrow(r_dec, rw_a2[l], r_icl)
        g2p = zrow(r_dec + r_icl, rw_g2[l], r_gate)
        wa_d = _block_diag(rg_wa[l]).astype(BF16)
        wx_d = _block_diag(rg_wx[l]).astype(BF16)
        w_out_b, w_xq_b, w_xkv_b, w_xo_b, w_ff1_b, w_ff2_b = [(w, l) for w in stacked_bf16]
        rwkv_par = dict(zb_w=zb_w, d_b=d_b, mu=mu_pad, w0=row1(rw_w0[l]), a0=row1(rw_a0[l]), w2p=w2p, a2p=a2p,
                        g2p=g2p, kk=row1(rw_kk[l]), ka=row1(rw_ka[l]), ones_h=ones_h)
        post_par = dict(lnx_w=row1(rw_lnx_w[l]), lnx_b=row1(rw_lnx_b[l]), rk=row1(rw_rk[l]), ones_h=ones_h, dh=dh_b)
        rg_par = dict(d_c=d_c, xc_blk0=xc_blk0, conv_w_arr=rg_conv_w[l], conv_b=rg_conv_b[l], wa_d=wa_d, wx_d=wx_d,
                      ba=rg_ba[l], bx=rg_bx[l], lam=rg_lambda[l])

        def tail(x, ya, yb, yc, attend, tm, tm_wide):
            segs = [(w_out_b, d_a, 0), (w_out_b, d_b, d_a // d_b), (w_out_b, d_c, (d_a + d_b) // d_c)]
            x = _matmul([ya, yb, yc], segs, residual=x, tm=tm_wide, name="mix_out_proj")
            q = _matmul([x], [(w_xq_b, d_model, 0)], gain=norm_x[l], tm=tm_wide, name="mem_q_proj")
            o = attend(q)
            x = _matmul([o], [(w_xo_b, d_x, 0)], residual=x, tm=tm_wide, name="mem_out_proj")
            hid = _matmul([x], [(w_ff1_b, d_model, 0)], gain=norm_ff[l], epilogue="relu2", out_dtype=BF16,
                          tm=tm_wide, tn=1024, name="ff_up")
            return _matmul([hid], [(w_ff2_b, d_ff, 0)], residual=x, tm=tm, name="ff_down")

        m_p = nb_p * t_p
        mkv = _matmul([memf], [(w_xkv_b, d_model, 0)], gain=norm_mem[l], tm=512, name="mem_kv_proj")
        z = _matmul([xp], [(w_in_r, d_model, 0)], gain=norm_mix[l], tm=1024, tn=1024, name="mix_in_proj")
        lf, c_t = _logf_cumsum(z, fa_blk, bf_pad, nb_p, t_p)
        c_row = c_t[:, :h_a, :].reshape(nb_p * h_a, t_p)
        ya = _fox_prompt(z, c_row, n=nb_p, t=t_p, n_heads=h_a, dh=dh_a, q_blk0=q_blk0)
        r, lw, km, v, kn, bt, g = _rwkv_prep(z, None, seq_len=t_p, **rwkv_par)
        y, h_pair = _rwkv_chunk(r, lw, km, v, kn, bt, n=nb_p, t=t_p, d_b=d_b, dh=dh_b)
        yb = _rwkv_post(y, r, km, v, g, **post_par)
        yc, conv1, h1 = _rglru_seq(z, n=nb_p, t=t_p, **rg_par)
        xp = tail(xp, ya, yb, yc,
                  lambda q: _mem_attn(q, mkv, n=nb_p, t=t_p, n_mem=n_mem, n_heads=h_x, dh=dh_x), 512, 1024)
        z3 = z.reshape(nb_p, t_p, -1)
        hp = h_pair.reshape(nb_p, d_b // LANES, 2, dh_b, 2, dh_b)
        wkv = jnp.stack([hp[:, :, 0, :, 0, :], hp[:, :, 1, :, 1, :]], axis=2)
        wkv = jnp.swapaxes(wkv, -1, -2).reshape(nb_p, h_b, dh_b, dh_b)
        p_states.append((
            z3[:, :, zb_w + d_a:zb_w + 2 * d_a].reshape(nb_p, t_p, h_a, dh_a),
            z3[:, :, zb_w + 2 * d_a:zb_w + 3 * d_a].reshape(nb_p, t_p, h_a, dh_a),
            lf.reshape(nb_p, t_p, LANES)[:, :, :h_a],
            z3[:, t_p - 1, :n_b_cols],
            wkv,
            conv1,
            h1.reshape(nb_p, d_c),
            mkv[:, :d_x].reshape(nb_p, n_mem, h_x, dh_x),
            mkv[:, d_x:].reshape(nb_p, n_mem, h_x, dh_x),
        ))

        zs = _matmul([xs], [(w_in_r, d_model, 0)], gain=norm_mix[l], tm=nb_s, name="mix_in_proj_s")
        lf_s, _ = _logf_cumsum(zs, fa_blk, bf_pad, 1, nb_s)
        hd = lambda a: a.reshape(nb_s, h_a, dh_a)
        q_s = hd(zs[:, zb_w:zb_w + d_a])
        k_s = hd(zs[:, zb_w + d_a:zb_w + 2 * d_a])
        v_s = hd(zs[:, zb_w + 2 * d_a:zb_w + 3 * d_a])
        lfn = jnp.broadcast_to(lf_s[:, :h_a, None], (nb_s, h_a, dh_a))
        ya_s = _fox_decode(page_table, q_s, k_s, v_s, lfn, kc, vc, lf_t, g_mat, layer=l, n_phys=n_phys)
        ya_s = ya_s.reshape(nb_s, d_a).astype(BF16)
        zprev = pad_cols(state_rwkv_shift[l], zb_w)
        r, lw, km, v, kn, bt, g = _rwkv_prep(zs, zprev, seq_len=1, **rwkv_par)
        s2 = state_rwkv_wkv[l].reshape(nb_s, d_b // LANES, 2, dh_b, dh_b)
        s2 = jnp.swapaxes(s2, 2, 3).reshape(nb_s, d_b // LANES, dh_b, LANES)
        y, s2n = _rwkv_step(r, lw, km, v, kn, bt, s2, ones_pair, dh=dh_b)
        wkv_s = jnp.swapaxes(s2n.reshape(nb_s, d_b // LANES, dh_b, 2, dh_b), 2, 3).reshape(nb_s, h_b, dh_b, dh_b)
        yb_s = _rwkv_post(y, r, km, v, g, **post_par)
        yc_s, conv1_s, h1_s = _rglru_step(zs, state_rglru_conv[l], state_rglru_h[l], **rg_par)
        xs = tail(xs, ya_s, yb_s, yc_s,
                  lambda q: _mem_decode(q.reshape(nb_s, h_x, dh_x), mk2, mv2, layer=l).reshape(nb_s, d_x),
                  nb_s, nb_s)
        s_states.append((
            k_s.reshape(nb_s, 1, h_a, dh_a),
            v_s.reshape(nb_s, 1, h_a, dh_a),
            lf_s[:, :h_a].reshape(nb_s, 1, h_a),
            zs[:, :n_b_cols],
            wkv_s,
            conv1_s,
            h1_s,
        ))

    y_prompt = _rmsnorm(xp, norm_f).reshape(nb_p, t_p, d_model)
    y_sample = _rmsnorm(xs, norm_f).reshape(nb_s, 1, d_model)
    p_out = [jnp.stack(s) for s in zip(*p_states)]
    s_out = [jnp.stack(s) for s in zip(*s_states)]
    return (y_prompt, y_sample, *p_out, *s_out)
```

```python
import functools

import numpy as np
import jax
import jax.numpy as jnp
from jax import lax
from jax.experimental import pallas as pl
from jax.experimental.pallas import tpu as pltpu

F32 = jnp.float32
BF16 = jnp.bfloat16

NORM_EPS = 1e-6
LNX_EPS = 64e-5
RG_C = 8.0
NEG_INF = -1e30

LANES = 128
SUBLANES = 8
VMEM_CAP_BYTES = 60000 * 1024
RWKV_CHUNK = 64


def _cparams(semantics, est_bytes):
    limit = int(min(max(2 * est_bytes + (8 << 20), 24 << 20), VMEM_CAP_BYTES))
    return pltpu.CompilerParams(dimension_semantics=semantics, vmem_limit_bytes=limit)


def _split_bf16(x, parts):
    out = []
    r = x
    for i in range(parts):
        h = r.astype(BF16)
        out.append(h)
        if i + 1 < parts:
            r = r - h.astype(F32)
    return out


def _dot_nn(a, b):
    return jnp.dot(a, b, preferred_element_type=F32)


def _dot_nt(a, b):
    return lax.dot_general(a, b, (((1,), (1,)), ((), ())), preferred_element_type=F32)


def _dot_x_exact(x, w_exact, parts=3):
    return sum(_dot_nn(p, w_exact) for p in _split_bf16(x, parts))


def _dot_exact_x(w_exact, x, parts=3):
    return sum(_dot_nn(w_exact, p) for p in _split_bf16(x, parts))


def _dot3(a, b, nt=False):
    f = _dot_nt if nt else _dot_nn
    ah, al = _split_bf16(a, 2)
    bh, bl = _split_bf16(b, 2)
    return f(ah, bh) + f(ah, bl) + f(al, bh)


def _softplus(x):
    return jnp.maximum(x, 0.0) + jnp.log1p(jnp.exp(-jnp.abs(x)))


def _log_sigmoid(x):
    return -_softplus(-x)


def _gelu_tanh(x):
    c = np.float32(np.sqrt(2.0 / np.pi))
    return 0.5 * x * (1.0 + jnp.tanh(c * (x + 0.044715 * (x * x * x))))


def _mm_kernel(*refs, n_seg, has_gain, has_res, epilogue):
    xs = refs[:n_seg]
    ws = refs[n_seg:2 * n_seg]
    pos = 2 * n_seg
    g_ref = res_ref = None
    if has_gain:
        g_ref = refs[pos]
        pos += 1
    if has_res:
        res_ref = refs[pos]
        pos += 1
    o_ref = refs[pos]
    if has_gain:
        xn_ref = refs[pos + 1]

        @pl.when(pl.program_id(1) == 0)
        def _():
            x = xs[0][...]
            ms = jnp.mean(x * x, axis=-1, keepdims=True)
            xn_ref[...] = (x * lax.rsqrt(ms + NORM_EPS) * g_ref[...]).astype(BF16)

        acc = _dot_nn(xn_ref[...], ws[0][...])
    else:
        acc = _dot_nn(xs[0][...], ws[0][...])
        for x_ref, w_ref in zip(xs[1:], ws[1:]):
            acc = acc + _dot_nn(x_ref[...], w_ref[...])
    if epilogue == "relu2":
        r = jnp.maximum(acc, 0.0)
        acc = r * r
    if has_res:
        acc = acc + res_ref[...]
    o_ref[...] = acc.astype(o_ref.dtype)


def _matmul(xs, ws, *, gain=None, residual=None, epilogue="none", out_dtype=F32, tm=512, tn=512, name="mm"):
    m = xs[0].shape[0]
    w0 = ws[0][0]
    n = (w0[0] if isinstance(w0, tuple) else w0).shape[-1]
    tm = min(tm, m)
    tn = min(tn, n)
    while n % tn:
        tn //= 2
    assert m % tm == 0 and tn % LANES == 0
    in_specs, args = [], []
    est = 0
    for x in xs:
        k = x.shape[1]
        in_specs.append(pl.BlockSpec((tm, k), lambda i, j: (i, 0)))
        args.append(x)
        est += 2 * tm * k * x.dtype.itemsize
    for (w, k, rb) in ws:
        if isinstance(w, tuple):
            w, layer = w
            in_specs.append(pl.BlockSpec((None, k, tn), lambda i, j, rb=rb, layer=layer: (layer, rb, j)))
        else:
            in_specs.append(pl.BlockSpec((k, tn), lambda i, j, rb=rb: (rb, j)))
        args.append(w)
        est += 2 * k * tn * w.dtype.itemsize
    scratch = []
    if gain is not None:
        k = xs[0].shape[1]
        in_specs.append(pl.BlockSpec((1, k), lambda i, j: (0, 0)))
        args.append(gain.reshape(1, k).astype(F32))
        scratch.append(pltpu.VMEM((tm, k), BF16))
        est += tm * k * 2
    if residual is not None:
        in_specs.append(pl.BlockSpec((tm, tn), lambda i, j: (i, j)))
        args.append(residual)
        est += 2 * tm * tn * 4
    est += 3 * tm * tn * 4
    kern = functools.partial(_mm_kernel, n_seg=len(xs), has_gain=gain is not None,
                             has_res=residual is not None, epilogue=epilogue)
    return pl.pallas_call(
        kern,
        grid=(m // tm, n // tn),
        in_specs=in_specs,
        out_specs=pl.BlockSpec((tm, tn), lambda i, j: (i, j)),
        out_shape=jax.ShapeDtypeStruct((m, n), out_dtype),
        scratch_shapes=scratch,
        compiler_params=_cparams(("parallel", "arbitrary"), est),
        name=name,
    )(*args)


def _rmsnorm_kernel(x_ref, g_ref, o_ref):
    x = x_ref[...]
    ms = jnp.mean(x * x, axis=-1, keepdims=True)
    o_ref[...] = x * lax.rsqrt(ms + NORM_EPS) * g_ref[...]


def _rmsnorm(x, g, tm=512):
    m, d = x.shape
    tm = min(tm, m)
    return pl.pallas_call(
        _rmsnorm_kernel,
        grid=(m // tm,),
        in_specs=[pl.BlockSpec((tm, d), lambda i: (i, 0)), pl.BlockSpec((1, d), lambda i: (0, 0))],
        out_specs=pl.BlockSpec((tm, d), lambda i: (i, 0)),
        out_shape=jax.ShapeDtypeStruct((m, d), F32),
        compiler_params=_cparams(("parallel",), 4 * tm * d * 4),
        name="final_rmsnorm",
    )(x, g.reshape(1, d))


def _logf_kernel(fa_ref, bf_ref, lf_ref, ct_ref, *, t, blk):
    row = lax.broadcasted_iota(jnp.int32, (blk, blk), 0)
    col = lax.broadcasted_iota(jnp.int32, (blk, blk), 1)
    tri = (col <= row).astype(BF16)
    carry = jnp.zeros((1, LANES), F32)
    for b in range(t // blk):
        sl = pl.ds(b * blk, blk)
        lf = _log_sigmoid(fa_ref[sl, :] + bf_ref[...])
        lf_ref[sl, :] = lf
        c = _dot_exact_x(tri, lf) + carry
        ct_ref[:, sl] = c.T
        carry = c[blk - 1:blk, :]


def _logf_cumsum(z, fa_blk, bf_pad, n, t):
    blk = min(t, 256)
    kern = functools.partial(_logf_kernel, t=t, blk=blk)
    return pl.pallas_call(
        kern,
        grid=(n,),
        in_specs=[pl.BlockSpec((t, LANES), lambda i: (i, fa_blk)),
                  pl.BlockSpec((1, LANES), lambda i: (0, 0))],
        out_specs=[pl.BlockSpec((t, LANES), lambda i: (i, 0)),
                   pl.BlockSpec((None, LANES, t), lambda i: (i, 0, 0))],
        out_shape=[jax.ShapeDtypeStruct((n * t, LANES), F32),
                   jax.ShapeDtypeStruct((n, LANES, t), F32)],
        compiler_params=_cparams(("parallel",), 10 * t * LANES * 4),
        name="logf_cumsum",
    )(z, bf_pad)


def _fox_attn_kernel(q_ref, k_ref, v_ref, ck_ref, o_ref, m_sc, l_sc, acc_sc, *, scale, tq, dh, hb):
    qi = pl.program_id(2)
    m_sc[...] = jnp.full(m_sc.shape, NEG_INF, F32)
    l_sc[...] = jnp.zeros(l_sc.shape, F32)
    acc_sc[...] = jnp.zeros(acc_sc.shape, F32)
    cols = [slice(j * dh, (j + 1) * dh) for j in range(hb)]
    qs = [q_ref[:, cols[j]].astype(BF16) for j in range(hb)]

    def block(ki, diagonal):
        rows = pl.ds(pl.multiple_of(ki * tq, tq), tq)
        ss = [_dot_nt(qs[j], k_ref[rows, cols[j]].astype(BF16)) for j in range(hb)]
        ps, alphas = [], []
        for j in range(hb):
            s = ss[j] * scale - ck_ref[j, ki]
            if diagonal:
                causal = (lax.broadcasted_iota(jnp.int32, (tq, tq), 1)
                          <= lax.broadcasted_iota(jnp.int32, (tq, tq), 0))
                s = jnp.where(causal, s, NEG_INF)
            m_old = m_sc[j]
            m_new = jnp.maximum(m_old, jnp.max(s, axis=1, keepdims=True))
            alpha = jnp.exp(m_old - m_new)
            p = jnp.exp(s - m_new)
            l_sc[j] = alpha * l_sc[j] + jnp.sum(p, axis=1, keepdims=True)
            m_sc[j] = m_new
            ps.append(p.astype(BF16))
            alphas.append(alpha)
        pvs = [_dot_nn(ps[j], v_ref[rows, cols[j]].astype(BF16)) for j in range(hb)]
        for j in range(hb):
            acc_sc[j] = alphas[j] * acc_sc[j] + pvs[j]

    def body(ki, carry):
        block(ki, False)
        return carry

    lax.fori_loop(0, qi, body, 0)
    block(qi, True)
    for j in range(hb):
        o_ref[:, cols[j]] = (acc_sc[j] / l_sc[j]).astype(o_ref.dtype)


def _fox_prompt(z, c_row, *, n, t, n_heads, dh, q_blk0, tq=512, hb=8):
    tq = min(tq, t)
    nq = t // tq
    hb = min(hb, n_heads)
    assert n_heads % hb == 0 and q_blk0 * LANES % (hb * dh) == 0
    ng = n_heads // hb
    g0 = q_blk0 * LANES // (hb * dh)
    kern = functools.partial(_fox_attn_kernel, scale=float(dh) ** -0.5, tq=tq, dh=dh, hb=hb)
    return pl.pallas_call(
        kern,
        grid=(n, ng, nq),
        in_specs=[
            pl.BlockSpec((tq, hb * dh), lambda b, g, qi: (b * nq + qi, g0 + g)),
            pl.BlockSpec((t, hb * dh), lambda b, g, qi: (b, g0 + ng + g)),
            pl.BlockSpec((t, hb * dh), lambda b, g, qi: (b, g0 + 2 * ng + g)),
            pl.BlockSpec((hb, nq, 1, tq), lambda b, g, qi: (b * ng + g, 0, 0, 0)),
        ],
        out_specs=pl.BlockSpec((tq, hb * dh), lambda b, g, qi: (b * nq + qi, g)),
        out_shape=jax.ShapeDtypeStruct((n * t, n_heads * dh), BF16),
        scratch_shapes=[pltpu.VMEM((hb, tq, 1), F32), pltpu.VMEM((hb, tq, 1), F32),
                        pltpu.VMEM((hb, tq, dh), F32)],
        compiler_params=_cparams(("parallel", "parallel", "arbitrary"),
                                 hb * (4 * t * dh * 4 + 8 * tq * dh * 4 + 6 * tq * tq * 4)),
        name="fox_prompt_attn",
    )(z, z, z, c_row.reshape(n * n_heads, nq, 1, tq))


def _kv_cache_rows_kernel(*refs, n_heads, dh, has_prev):
    k_ref, v_ref = refs[:2]
    k5_ref, v5_ref = refs[-2:]
    tb = k_ref.shape[0]
    for h in range(n_heads):
        dst = pl.ds(h, tb, stride=n_heads)
        k5_ref[dst, :] = k_ref[:, h * dh:(h + 1) * dh]
        v5_ref[dst, :] = v_ref[:, h * dh:(h + 1) * dh]


def _kv_cache_rows(z, prev, *, layer, depth, n, t, n_heads, dh, k_blk0, tb=512):
    tb = min(tb, t)
    nt = t // tb
    d_a = n_heads * dh
    kb = k_blk0 * LANES // d_a
    assert k_blk0 * LANES % d_a == 0 and t % tb == 0
    shape = jax.ShapeDtypeStruct((depth, n, t * n_heads, dh), F32)
    out_spec = pl.BlockSpec((None, None, tb * n_heads, dh), lambda b, i: (layer, b, i, 0))
    in_specs = [pl.BlockSpec((tb, d_a), lambda b, i: (b * nt + i, kb)),
                pl.BlockSpec((tb, d_a), lambda b, i: (b * nt + i, kb + 1))]
    args = [z, z]
    aliases = {}
    if prev is not None:
        in_specs += [pl.BlockSpec(memory_space=pl.ANY)] * 2
        args += list(prev)
        aliases = {2: 0, 3: 1}
    kern = functools.partial(_kv_cache_rows_kernel, n_heads=n_heads, dh=dh, has_prev=prev is not None)
    return pl.pallas_call(
        kern,
        grid=(n, nt),
        in_specs=in_specs,
        out_specs=[out_spec, out_spec],
        out_shape=[shape, shape],
        input_output_aliases=aliases,
        compiler_params=_cparams(("parallel", "parallel"), 8 * tb * d_a * 4),
        name="kv_cache_rows",
    )(*args)


def _head_valid(n_heads, rows):
    lane = lax.broadcasted_iota(jnp.int32, (n_heads, rows), 1)
    sub = lax.broadcasted_iota(jnp.int32, (n_heads, rows), 0)
    return (lane & (n_heads - 1)) == sub


def _fox_decode_kernel(pt_ref, q_ref, kn_ref, vn_ref, lfn_ref, *rest, n_heads, scale, g):
    kcs, vcs, lfs = rest[0:g], rest[g:2 * g], rest[2 * g:3 * g]
    g_ref, o_ref, m_sc, l_sc, acc_sc, car_sc = rest[3 * g:]
    p = pl.program_id(1)

    @pl.when(p == 0)
    def _():
        m_sc[...] = jnp.full(m_sc.shape, NEG_INF, F32)
        l_sc[...] = jnp.zeros(l_sc.shape, F32)
        acc_sc[...] = jnp.zeros(acc_sc.shape, F32)
        car_sc[...] = jnp.zeros(car_sc.shape, F32)

    rows = kcs[0].shape[0]
    qb = q_ref[...].astype(BF16)
    valid = _head_valid(n_heads, rows)
    f_all = jnp.concatenate([lf[...] for lf in lfs], axis=0)
    cum_all = _dot_x_exact(f_all, g_ref[...])
    car = car_sc[...]
    ss = []
    for j in range(g):
        s = _dot_nt(qb, kcs[j][...].astype(BF16)) * scale
        ss.append(jnp.where(valid, s - (car + cum_all[j * n_heads:(j + 1) * n_heads]), NEG_INF))
        car = car + jnp.sum(lfs[j][...], axis=1, keepdims=True)
    car_new = car
    m_old = m_sc[...]
    m_new = m_old
    for s in ss:
        m_new = jnp.maximum(m_new, jnp.max(s, axis=1, keepdims=True))
    alpha = jnp.exp(m_old - m_new)
    l_new = alpha * l_sc[...]
    acc_new = alpha * acc_sc[...]
    for j in range(g):
        pr = jnp.exp(ss[j] - m_new)
        l_new = l_new + jnp.sum(pr, axis=1, keepdims=True)
        acc_new = acc_new + _dot_nn(pr.astype(BF16), vcs[j][...].astype(BF16))
    m_sc[...] = m_new
    l_sc[...] = l_new
    acc_sc[...] = acc_new
    car_sc[...] = car_new

    @pl.when(p == pl.num_programs(1) - 1)
    def _():
        s_new = jnp.sum(q_ref[...] * kn_ref[...], axis=1, keepdims=True) * scale - (car_new + lfn_ref[:, 0:1])
        m2 = jnp.maximum(m_new, s_new)
        a2 = jnp.exp(m_new - m2)
        pn = jnp.exp(s_new - m2)
        o_ref[...] = (a2 * acc_new + pn * vn_ref[...]) / (a2 * l_new + pn)


def _fox_decode(page_table, q, k_new, v_new, lf_new, kc, vc, lf_t, g_mat, *, layer, n_phys, g=8):
    b, n_heads, dh = q.shape
    n_pages = page_table.shape[1]
    g = min(g, n_pages)
    assert n_pages % g == 0
    rows = kc.shape[1]
    page = lf_t.shape[2]
    base = layer * n_phys
    tok = pl.BlockSpec((None, n_heads, dh), lambda i, p, pt: (i, 0, 0))
    pg = lambda j: (lambda i, p, pt: (base + pt[i, p * g + j], 0, 0))
    kern = functools.partial(_fox_decode_kernel, n_heads=n_heads, scale=float(dh) ** -0.5, g=g)
    return pl.pallas_call(
        kern,
        grid_spec=pltpu.PrefetchScalarGridSpec(
            num_scalar_prefetch=1,
            grid=(b, n_pages // g),
            in_specs=([tok] * 4
                      + [pl.BlockSpec((None, rows, dh), pg(j)) for j in range(g)]
                      + [pl.BlockSpec((None, rows, dh), pg(j)) for j in range(g)]
                      + [pl.BlockSpec((None, n_heads, page), pg(j)) for j in range(g)]
                      + [pl.BlockSpec((page, rows), lambda i, p, pt: (0, 0))]),
            out_specs=tok,
            scratch_shapes=[pltpu.VMEM((n_heads, 1), F32), pltpu.VMEM((n_heads, 1), F32),
                            pltpu.VMEM((n_heads, dh), F32), pltpu.VMEM((n_heads, 1), F32)],
        ),
        out_shape=jax.ShapeDtypeStruct((b, n_heads, dh), F32),
        compiler_params=_cparams(("parallel", "arbitrary"), g * 4 * rows * dh * 4 + 2 * page * rows * 2),
        name="fox_decode_attn",
    )(page_table, q, k_new, v_new, lf_new, *([kc] * g), *([vc] * g), *([lf_t] * g), g_mat)


def _mem_decode_kernel(q_ref, k_ref, v_ref, o_ref, *, n_heads, scale, bb):
    rows = k_ref.shape[1]
    valid = _head_valid(n_heads, rows)
    ss = [_dot_nt(q_ref[j].astype(BF16), k_ref[j].astype(BF16)) for j in range(bb)]
    ps, ls = [], []
    for s in ss:
        s = jnp.where(valid, s * scale, NEG_INF)
        p = jnp.exp(s - jnp.max(s, axis=1, keepdims=True))
        ls.append(jnp.sum(p, axis=1, keepdims=True))
        ps.append(p.astype(BF16))
    os_ = [_dot_nn(ps[j], v_ref[j].astype(BF16)) for j in range(bb)]
    for j in range(bb):
        o_ref[j] = (os_[j] / ls[j]).astype(o_ref.dtype)


def _mem_decode(q, k2, v2, *, layer, bb=4):
    b, n_heads, dh = q.shape
    rows = k2.shape[1]
    bb = min(bb, b)
    assert b % bb == 0
    base = layer * b // bb
    kern = functools.partial(_mem_decode_kernel, n_heads=n_heads, scale=float(dh) ** -0.5, bb=bb)
    return pl.pallas_call(
        kern,
        grid=(b // bb,),
        in_specs=[pl.BlockSpec((bb, n_heads, dh), lambda i: (i, 0, 0)),
                  pl.BlockSpec((bb, rows, dh), lambda i: (base + i, 0, 0)),
                  pl.BlockSpec((bb, rows, dh), lambda i: (base + i, 0, 0))],
        out_specs=pl.BlockSpec((bb, n_heads, dh), lambda i: (i, 0, 0)),
        out_shape=jax.ShapeDtypeStruct((b, n_heads, dh), BF16),
        compiler_params=_cparams(("parallel",), 4 * bb * rows * dh * 4),
        name="mem_decode_attn",
    )(q, k2, v2)


def _mem_attn_kernel(q_ref, k_ref, v_ref, o_ref, *, scale):
    s = _dot_nt(q_ref[...].astype(BF16), k_ref[...].astype(BF16)) * scale
    m = jnp.max(s, axis=1, keepdims=True)
    p = jnp.exp(s - m)
    l = jnp.sum(p, axis=1, keepdims=True)
    o_ref[...] = (_dot_nn(p.astype(BF16), v_ref[...].astype(BF16)) / l).astype(o_ref.dtype)


def _mem_attn(q, kv, *, n, t, n_mem, n_heads, dh, tq=512):
    tq = min(tq, t)
    nq = t // tq
    kern = functools.partial(_mem_attn_kernel, scale=float(dh) ** -0.5)
    return pl.pallas_call(
        kern,
        grid=(n, n_heads, nq),
        in_specs=[pl.BlockSpec((tq, dh), lambda b, h, qi: (b * nq + qi, h)),
                  pl.BlockSpec((n_mem, dh), lambda b, h, qi: (b, h)),
                  pl.BlockSpec((n_mem, dh), lambda b, h, qi: (b, n_heads + h))],
        out_specs=pl.BlockSpec((tq, dh), lambda b, h, qi: (b * nq + qi, h)),
        out_shape=jax.ShapeDtypeStruct((n * t, n_heads * dh), BF16),
        compiler_params=_cparams(("parallel", "parallel", "parallel"), 8 * tq * dh * 4 + 4 * tq * n_mem * 4),
        name="mem_attn",
    )(q, kv, kv)


def _rglru_gates(u, wa_ref, wx_ref, ba_ref, bx_ref, lam_ref):
    ub = u.astype(BF16)
    gate_a = jax.nn.sigmoid(_dot_nn(ub, wa_ref[...]) + ba_ref[...])
    gate_x = jax.nn.sigmoid(_dot_nn(ub, wx_ref[...]) + bx_ref[...])
    log_a = -RG_C * gate_a * _softplus(-lam_ref[...])
    a = jnp.exp(log_a)
    b = u * gate_x * jnp.sqrt(1.0 - jnp.exp(2.0 * log_a))
    return a, b


def _shift_rows(x, d, fill):
    t = x.shape[0]
    if d % SUBLANES == 0:
        return jnp.concatenate([jnp.full((d, x.shape[1]), fill, x.dtype), x[:t - d]], axis=0)
    rolled = pltpu.roll(x, d, 0)
    row = lax.broadcasted_iota(jnp.int32, x.shape, 0)
    return jnp.where(row < d, fill, rolled)


def _rglru_seq_kernel(xc_ref, gc_ref, cw_ref, cb_ref, wa_ref, wx_ref, ba_ref, bx_ref, lam_ref,
                      y_ref, conv_ref, h_ref, xpad_sc, *, t, conv_w):
    xpad_sc[0:SUBLANES, :] = jnp.zeros((SUBLANES, LANES), F32)
    xpad_sc[SUBLANES:, :] = xc_ref[...]
    u = cb_ref[...] + jnp.zeros((t, LANES), F32)
    for j in range(conv_w):
        u = u + xpad_sc[pl.ds(SUBLANES - (conv_w - 1) + j, t), :] * cw_ref[j:j + 1, :]
    a, b = _rglru_gates(u, wa_ref, wx_ref, ba_ref, bx_ref, lam_ref)
    d = 1
    while d < t:
        a_sh = _shift_rows(a, d, 1.0)
        b_sh = _shift_rows(b, d, 0.0)
        b = a * b_sh + b
        a = a * a_sh
        d *= 2
    y_ref[...] = (b * _gelu_tanh(gc_ref[...])).astype(y_ref.dtype)
    h_ref[...] = b[t - 1:t, :]
    conv_ref[...] = xpad_sc[pl.ds(SUBLANES + t - (conv_w - 1), conv_w - 1), :]


def _rglru_seq(z, *, n, t, d_c, xc_blk0, conv_w_arr, conv_b, wa_d, wx_d, ba, bx, lam):
    nc = d_c // LANES
    conv_w = conv_w_arr.shape[0]
    vec = lambda a: a.reshape(1, d_c)
    vspec = pl.BlockSpec((1, LANES), lambda b, c: (0, c))
    kern = functools.partial(_rglru_seq_kernel, t=t, conv_w=conv_w)
    return pl.pallas_call(
        kern,
        grid=(n, nc),
        in_specs=[pl.BlockSpec((t, LANES), lambda b, c: (b, xc_blk0 + c)),
                  pl.BlockSpec((t, LANES), lambda b, c: (b, xc_blk0 + nc + c)),
                  pl.BlockSpec((conv_w, LANES), lambda b, c: (0, c)),
                  vspec,
                  pl.BlockSpec((LANES, LANES), lambda b, c: (c, c)),
                  pl.BlockSpec((LANES, LANES), lambda b, c: (c, c)),
                  vspec, vspec, vspec],
        out_specs=[pl.BlockSpec((t, LANES), lambda b, c: (b, c)),
                   pl.BlockSpec((None, conv_w - 1, LANES), lambda b, c: (b, 0, c)),
                   pl.BlockSpec((None, 1, LANES), lambda b, c: (b, 0, c))],
        out_shape=[jax.ShapeDtypeStruct((n * t, d_c), BF16),
                   jax.ShapeDtypeStruct((n, conv_w - 1, d_c), F32),
                   jax.ShapeDtypeStruct((n, 1, d_c), F32)],
        scratch_shapes=[pltpu.VMEM((t + SUBLANES, LANES), F32)],
        compiler_params=_cparams(("parallel", "parallel"), 16 * t * LANES * 4),
        name="rglru_seq",
    )(z, z, conv_w_arr, vec(conv_b), wa_d, wx_d, vec(ba), vec(bx), vec(lam))


def _rglru_step_kernel(xc_ref, gc_ref, c0_ref, h0_ref, cw_ref, cb_ref, wa_ref, wx_ref, ba_ref, bx_ref, lam_ref,
                       y_ref, conv_ref, h_ref, *, conv_w):
    xc = xc_ref[...]
    u = cb_ref[...] + xc * cw_ref[conv_w - 1:conv_w, :]
    for j in range(conv_w - 1):
        u = u + c0_ref[:, j, :] * cw_ref[j:j + 1, :]
    a, b = _rglru_gates(u, wa_ref, wx_ref, ba_ref, bx_ref, lam_ref)
    h = a * h0_ref[...] + b
    y_ref[...] = (h * _gelu_tanh(gc_ref[...])).astype(y_ref.dtype)
    h_ref[...] = h
    for j in range(conv_w - 2):
        conv_ref[:, j, :] = c0_ref[:, j + 1, :]
    conv_ref[:, conv_w - 2, :] = xc


def _rglru_step(z, conv0, h0, *, d_c, xc_blk0, conv_w_arr, conv_b, wa_d, wx_d, ba, bx, lam):
    bsz = z.shape[0]
    conv_w = conv_w_arr.shape[0]
    ncb = d_c // LANES
    vec = lambda a: a.reshape(1, d_c)
    full = lambda shape: pl.BlockSpec(shape, lambda i: (0,) * len(shape))
    kern = functools.partial(_rglru_step_kernel, conv_w=conv_w)
    return pl.pallas_call(
        kern,
        grid=(1,),
        in_specs=[pl.BlockSpec((bsz, d_c), lambda i: (0, xc_blk0 * LANES // d_c)),
                  pl.BlockSpec((bsz, d_c), lambda i: (0, xc_blk0 * LANES // d_c + 1)),
                  full((bsz, conv_w - 1, d_c)), full((bsz, d_c)), full((conv_w, d_c)), full((1, d_c)),
                  full((d_c, d_c)), full((d_c, d_c)), full((1, d_c)), full((1, d_c)), full((1, d_c))],
        out_specs=[full((bsz, d_c)), full((bsz, conv_w - 1, d_c)), full((bsz, d_c))],
        out_shape=[jax.ShapeDtypeStruct((bsz, d_c), BF16),
                   jax.ShapeDtypeStruct((bsz, conv_w - 1, d_c), F32),
                   jax.ShapeDtypeStruct((bsz, d_c), F32)],
        compiler_params=_cparams(("arbitrary",), 16 * bsz * d_c * 4 + 4 * d_c * d_c * 2),
        name="rglru_step",
    )(z, z, conv0, h0, conv_w_arr, vec(conv_b), wa_d, wx_d, vec(ba), vec(bx), vec(lam))


def _rwkv_prep_kernel(*refs, d_b, seq_mode, blocks_per_seq):
    if seq_mode:
        zb_ref, prev_ref = refs[:2]
    else:
        zb_ref, zp_ref = refs[:2]
    (mu_ref, w0_ref, a0_ref, w2_ref, a2_ref, g2_ref, kk_ref, ka_ref, ones_ref,
     r_ref, lw_ref, km_ref, v_ref, kn_ref, bt_ref, g_ref) = refs[2:18]
    zb = zb_ref[...]
    tm = zb.shape[0]
    if seq_mode:
        sh_sc = refs[18]
        first = (pl.program_id(0) % blocks_per_seq) == 0
        prev = jnp.where(first, 0.0, prev_ref[...])
        sh_sc[0:SUBLANES, :] = prev
        sh_sc[SUBLANES:, :] = zb
        zp = sh_sc[pl.ds(SUBLANES - 1, tm), :]
    else:
        zp = zp_ref[...]
    zs = zb + mu_ref[...] * (zp - zb)
    r = zs[:, 0:d_b]
    k = zs[:, d_b:2 * d_b]
    v = zs[:, 2 * d_b:3 * d_b]
    lr = zs[:, 3 * d_b:3 * d_b + w2_ref.shape[0]]
    w_lin = _dot_nn(jnp.tanh(lr).astype(BF16), w2_ref[...])
    a_lin = _dot_nn(lr.astype(BF16), a2_ref[...])
    g = _dot_nn(jax.nn.sigmoid(lr).astype(BF16), g2_ref[...])
    w = -_softplus(-(w0_ref[...] + w_lin)) - 0.5
    a = jax.nn.sigmoid(a0_ref[...] + a_lin)
    kk = k * kk_ref[...]
    nrm2 = _dot_x_exact(kk * kk, ones_ref[...], parts=2)
    kn = kk / jnp.maximum(jnp.sqrt(nrm2), 1e-12)
    r_ref[...] = r
    lw_ref[...] = -jnp.exp(w)
    km_ref[...] = k * (1.0 + (a - 1.0) * ka_ref[...])
    v_ref[...] = v
    kn_ref[...] = kn
    bt_ref[...] = kn * a
    g_ref[...] = g


def _rwkv_prep(z, zprev, *, zb_w, d_b, mu, w0, a0, w2p, a2p, g2p, kk, ka, ones_h, seq_len, tm=256):
    m = z.shape[0]
    seq_mode = zprev is None
    tm = min(tm, seq_len if seq_mode else m)
    assert m % tm == 0
    row = lambda i: (i, 0)
    cst = lambda i: (0, 0)
    if seq_mode:
        per8 = tm // SUBLANES
        second = pl.BlockSpec((SUBLANES, zb_w), lambda i: (jnp.maximum(i * per8 - 1, 0), 0))
        second_arg = z
        scratch = [pltpu.VMEM((tm + SUBLANES, zb_w), F32)]
        bps = seq_len // tm
    else:
        second = pl.BlockSpec((tm, zb_w), row)
        second_arg = zprev
        scratch = []
        bps = 1
    lrw = w2p.shape[0]
    kern = functools.partial(_rwkv_prep_kernel, d_b=d_b, seq_mode=seq_mode, blocks_per_seq=bps)
    out = jax.ShapeDtypeStruct((m, d_b), F32)
    return pl.pallas_call(
        kern,
        grid=(m // tm,),
        in_specs=[pl.BlockSpec((tm, zb_w), row), second,
                  pl.BlockSpec((1, zb_w), cst), pl.BlockSpec((1, d_b), cst), pl.BlockSpec((1, d_b), cst),
                  pl.BlockSpec((lrw, d_b), cst), pl.BlockSpec((lrw, d_b), cst), pl.BlockSpec((lrw, d_b), cst),
                  pl.BlockSpec((1, d_b), cst), pl.BlockSpec((1, d_b), cst), pl.BlockSpec((d_b, d_b), cst)],
        out_specs=[pl.BlockSpec((tm, d_b), row)] * 7,
        out_shape=[out] * 7,
        scratch_shapes=scratch,
        compiler_params=_cparams(("parallel",), 6 * tm * zb_w * 4 + 30 * tm * d_b * 4),
        name="rwkv_prep",
    )(z, second_arg, mu, w0, a0, w2p, a2p, g2p, kk, ka, ones_h)


def _mm_p(a, b, passes, nt=False):
    if passes == 1:
        f = _dot_nt if nt else _dot_nn
        return f(a.astype(BF16), b.astype(BF16))
    return _dot3(a, b, nt=nt)


def _rwkv_chunk_kernel(*refs, c, dh, pb, nbb, passes):
    ins = [[ref.at[j] for ref in refs[:6]] for j in range(nbb)]
    y_refs = [refs[6].at[j] for j in range(nbb)]
    hout_ref, h_sc = refs[7:]
    ci = pl.program_id(2)

    @pl.when(ci == 0)
    def _():
        h_sc[...] = jnp.zeros(h_sc.shape, F32)

    c2 = 2 * c
    lane = lax.broadcasted_iota(jnp.int32, (c, LANES), 1)
    lane2 = lax.broadcasted_iota(jnp.int32, (c2, LANES), 1)
    t_idx = lax.broadcasted_iota(jnp.int32, (c, c2), 0)
    j_idx = lax.broadcasted_iota(jnp.int32, (c, c2), 1) & (c - 1)
    r128 = lax.broadcasted_iota(jnp.int32, (LANES, LANES), 0)
    c128 = lax.broadcasted_iota(jnp.int32, (LANES, LANES), 1)
    same_head = (r128 < dh) == (c128 < dh)
    zero_rows = jnp.zeros((c, c2), F32)
    head_masks = [lane2 < dh, lane2 >= dh]
    cat = lambda a, b: jnp.concatenate([a, b], axis=0)
    mm = functools.partial(_mm_p, passes=passes)

    units = []
    for j in range(nbb):
        r_ref, lw_ref, km_ref, v_ref, kn_ref, bt_ref = ins[j]
        lw_all = lw_ref[...]
        lc_all = lw_all
        d = 1
        while d < c:
            lc_all = lc_all + _shift_rows(lc_all, d, 0.0)
            d *= 2
        p_all = jnp.exp(lc_all)
        pinv_all = jnp.exp(-lc_all)
        pprev_all = jnp.exp(lc_all - lw_all)
        for pr in range(pb):
            sl = slice(pr * LANES, (pr + 1) * LANES)
            p = p_all[:, sl]
            pinv = pinv_all[:, sl]
            units.append(dict(
                ar=cat(-kn_ref[:, sl] * pprev_all[:, sl], r_ref[:, sl] * p),
                btt=bt_ref[:, sl] * pinv, kt=km_ref[:, sl] * pinv, v=v_ref[:, sl], pc=p[c - 1:c, :],
                h=h_sc[j * pb + pr], y_ref=y_refs[j], sl=sl))
    for un in units:
        un["bk"] = cat(un["btt"], un["kt"])
    arhs = [mm(un["ar"], un["h"]) for un in units]
    chains = []
    for un, arh in zip(units, arhs):
        for head in range(2):
            chains.append(dict(un=un, ah=arh[:c], rh=arh[c:], head=head))
    gs = [mm(jnp.where(head_masks[ch["head"]], ch["un"]["ar"], 0.0), ch["un"]["bk"], nt=True) for ch in chains]
    for ch, g in zip(chains, gs):
        ch["p_top"] = jnp.where(j_idx < t_idx, g[:c], 0.0)
        ch["mr"] = jnp.where(j_idx <= t_idx, g[c:], 0.0)
    upds = [mm(ch["p_top"], cat(ch["ah"], ch["un"]["v"])) for ch in chains]
    for ch, up in zip(chains, upds):
        ch["u"] = ch["ah"] + up
    for _ in range(int(np.log2(c2)) - 1):
        sq = [mm(ch["p_top"], cat(ch["p_top"], zero_rows)) for ch in chains]
        for ch, s in zip(chains, sq):
            ch["p_top"] = s
        upds = [mm(ch["p_top"], cat(ch["u"], ch["un"]["v"])) for ch in chains]
        for ch, up in zip(chains, upds):
            ch["u"] = ch["u"] + up
    yparts = [mm(ch["mr"], cat(ch["u"], ch["un"]["v"])) for ch in chains]
    ma_l = lane < dh
    for i, un in enumerate(units):
        c0, c1 = chains[2 * i], chains[2 * i + 1]
        un["u"] = jnp.where(ma_l, c0["u"], c1["u"])
        un["y_ref"][:, un["sl"]] = jnp.where(ma_l, c0["rh"] + yparts[2 * i], c1["rh"] + yparts[2 * i + 1])
    upds = [mm(cat(un["btt"] * un["pc"], un["kt"] * un["pc"]).T, cat(un["u"], un["v"])) for un in units]
    h_news = []
    for un, upd in zip(units, upds):
        pcol = jnp.broadcast_to(un["pc"], (SUBLANES, LANES)).T[:, 0:1]
        h_news.append(jnp.where(same_head, un["h"] * pcol + upd, 0.0))
    for i, h_new in enumerate(h_news):
        h_sc[i] = h_new

    @pl.when(ci == pl.num_programs(2) - 1)
    def _():
        for i, h_new in enumerate(h_news):
            hout_ref[i // pb, i % pb] = h_new


def _rwkv_chunk(r, lw, km, v, kn, bt, *, n, t, d_b, dh, pb=4, nbb=4, passes=1):
    c = RWKV_CHUNK
    assert t % c == 0 and 2 * dh == LANES and 2 * c == LANES
    npair = d_b // LANES
    pb = min(pb, npair)
    nbb = min(nbb, n)
    assert npair % pb == 0 and n % nbb == 0
    nchunk = t // c
    tok = pl.BlockSpec((nbb, c, pb * LANES), lambda b, pg, ci: (b, ci, pg))
    kern = functools.partial(_rwkv_chunk_kernel, c=c, dh=dh, pb=pb, nbb=nbb, passes=passes)
    seq = lambda a: a.reshape(n, t, d_b)
    y, h_pair = pl.pallas_call(
        kern,
        grid=(n // nbb, npair // pb, nchunk),
        in_specs=[tok] * 6,
        out_specs=[tok, pl.BlockSpec((nbb, pb, LANES, LANES), lambda b, pg, ci: (b, pg, 0, 0))],
        out_shape=[jax.ShapeDtypeStruct((n, t, d_b), F32),
                   jax.ShapeDtypeStruct((n, npair, LANES, LANES), F32)],
        scratch_shapes=[pltpu.VMEM((nbb * pb, LANES, LANES), F32)],
        compiler_params=_cparams(("parallel", "parallel", "arbitrary"), 64 * nbb * pb * LANES * LANES * 4),
        name="rwkv_chunk",
    )(seq(r), seq(lw), seq(km), seq(v), seq(kn), seq(bt))
    return y.reshape(n * t, d_b), h_pair


def _rwkv_step_kernel(r_ref, lw_ref, km_ref, v_ref, kn_ref, bt_ref, s_ref, ones_ref, y_ref, sout_ref,
                      *, bb, npair, dh):
    i2 = (lax.broadcasted_iota(jnp.int32, (dh, LANES), 1) & (dh - 1)) == \
        lax.broadcasted_iota(jnp.int32, (dh, LANES), 0)
    ones = ones_ref[...]
    units = [(b, pr, slice(b, b + 1), slice(pr * LANES, (pr + 1) * LANES)) for b in range(bb) for pr in range(npair)]
    ss = [jnp.concatenate([s_ref[b, 2 * pr], s_ref[b, 2 * pr + 1]], axis=1)
          for (b, pr, rb, sl) in units]
    sas = [_dot_x_exact(s * (-kn_ref[rb, sl]), ones) for s, (b, pr, rb, sl) in zip(ss, units)]
    vcols = [_dot_x_exact(jnp.where(i2, v_ref[rb, sl], 0.0), ones) for (b, pr, rb, sl) in units]
    s_news = [s * jnp.exp(lw_ref[rb, sl]) + sa * bt_ref[rb, sl] + vcol * km_ref[rb, sl]
              for s, sa, vcol, (b, pr, rb, sl) in zip(ss, sas, vcols, units)]
    ybs = [_dot_x_exact(s_new * r_ref[rb, sl], ones) for s_new, (b, pr, rb, sl) in zip(s_news, units)]
    for s_new, yb, (b, pr, rb, sl) in zip(s_news, ybs, units):
        y_ref[rb, sl] = jnp.sum(jnp.where(i2, yb, 0.0), axis=0, keepdims=True)
        sout_ref[b, 2 * pr] = s_new[:, :dh]
        sout_ref[b, 2 * pr + 1] = s_new[:, dh:]


def _rwkv_step(r, lw, km, v, kn, bt, s_all, ones_pair, *, dh, layer, bb=8):
    bsz, d_b = r.shape
    npair = d_b // LANES
    bb = min(bb, bsz)
    base = layer * bsz // bb
    tok = pl.BlockSpec((bb, d_b), lambda i: (i, 0))
    st_in = pl.BlockSpec((bb, 2 * npair, dh, dh), lambda i: (base + i, 0, 0, 0))
    st = pl.BlockSpec((bb, 2 * npair, dh, dh), lambda i: (i, 0, 0, 0))
    kern = functools.partial(_rwkv_step_kernel, bb=bb, npair=npair, dh=dh)
    return pl.pallas_call(
        kern,
        grid=(bsz // bb,),
        in_specs=[tok] * 6 + [st_in, pl.BlockSpec((LANES, LANES), lambda i: (0, 0))],
        out_specs=[tok, st],
        out_shape=[jax.ShapeDtypeStruct((bsz, d_b), F32),
                   jax.ShapeDtypeStruct((bsz, 2 * npair, dh, dh), F32)],
        compiler_params=_cparams(("parallel",), 16 * bb * npair * dh * LANES * 4),
        name="rwkv_step",
    )(r, lw, km, v, kn, bt, s_all, ones_pair)


def _rwkv_post_kernel(y_ref, r_ref, km_ref, v_ref, g_ref, lw_ref, lb_ref, rk_ref, ones_ref, o_ref, *, dh):
    ones = ones_ref[...]
    y = y_ref[...]
    inv = 1.0 / dh
    mu = _dot_x_exact(y, ones, parts=3) * inv
    d = y - mu
    var = _dot_x_exact(d * d, ones, parts=2) * inv
    yn = d * lax.rsqrt(var + LNX_EPS) * lw_ref[...] + lb_ref[...]
    bonus = _dot_x_exact(r_ref[...] * km_ref[...] * rk_ref[...], ones, parts=3) * v_ref[...]
    o_ref[...] = ((yn + bonus) * g_ref[...]).astype(o_ref.dtype)


def _rwkv_post(y, r, km, v, g, *, lnx_w, lnx_b, rk, ones_h, dh, tm=256):
    m, d_b = y.shape
    tm = min(tm, m)
    row = pl.BlockSpec((tm, d_b), lambda i: (i, 0))
    vec = pl.BlockSpec((1, d_b), lambda i: (0, 0))
    kern = functools.partial(_rwkv_post_kernel, dh=dh)
    return pl.pallas_call(
        kern,
        grid=(m // tm,),
        in_specs=[row] * 5 + [vec] * 3 + [pl.BlockSpec((d_b, d_b), lambda i: (0, 0))],
        out_specs=row,
        out_shape=jax.ShapeDtypeStruct((m, d_b), BF16),
        compiler_params=_cparams(("parallel",), 30 * tm * d_b * 4),
        name="rwkv_post",
    )(y, r, km, v, g, lnx_w, lnx_b, rk, ones_h)


def _round_up(x, m):
    return (x + m - 1) // m * m


def _block_ones(size, blk):
    idx = np.arange(size) // blk
    return jnp.asarray(idx[:, None] == idx[None, :], dtype=BF16)


def _block_diag(w):
    nb, bs, _ = w.shape
    eye = jnp.eye(nb, dtype=w.dtype)
    return (eye[:, None, :, None] * w[:, :, None, :]).reshape(nb * bs, nb * bs)


def kernel(x_prompt, x_sample, cache_fox_k, cache_fox_v, cache_fox_logf, state_rwkv_shift, state_rwkv_wkv, state_rglru_conv, state_rglru_h, cache_mem_k, cache_mem_v, page_table, mem_prompt, norm_mix, w_in, fox_bf, rw_mu, rw_w0, rw_w2, rw_a0, rw_a2, rw_g2, rw_kk, rw_ka, rw_rk, rw_lnx_w, rw_lnx_b, rg_conv_w, rg_conv_b, rg_wa, rg_ba, rg_wx, rg_bx, rg_lambda, w_out, norm_x, norm_mem, w_xq, w_xk, w_xv, w_xo, norm_ff, w_ff1, w_ff2, norm_f):
    nb_p, t_p, d_model = x_prompt.shape
    nb_s = x_sample.shape[0]
    depth, n_phys, page, h_a, dh_a = cache_fox_k.shape
    d_a = h_a * dh_a
    n_b_cols = state_rwkv_shift.shape[-1]
    _, _, h_b, dh_b, _ = state_rwkv_wkv.shape
    d_b = h_b * dh_b
    d_c = state_rglru_h.shape[-1]
    n_mem, h_x, dh_x = cache_mem_k.shape[2:]
    d_x = h_x * dh_x
    r_dec, r_icl, r_gate = rw_w2.shape[1], rw_a2.shape[1], rw_g2.shape[1]
    lr_w = r_dec + r_icl + r_gate
    zb_w = _round_up(n_b_cols + h_a, max(d_c, LANES))
    fa_blk = n_b_cols // LANES
    q_blk0 = zb_w // LANES
    xc_blk0 = (zb_w + 3 * d_a) // LANES
    assert n_b_cols % LANES == 0 and d_a % LANES == 0 and d_c % LANES == 0 and h_a <= LANES
    assert (zb_w + 3 * d_a) % d_c == 0 and n_b_cols == 3 * d_b + lr_w and (h_a & (h_a - 1)) == 0 and (h_x & (h_x - 1)) == 0

    splits = np.cumsum([d_a, d_a, d_a, h_a, n_b_cols, d_c])
    ones_h = _block_ones(d_b, dh_b)
    ones_pair = _block_ones(LANES, dh_b)
    g_mat = jnp.asarray(np.arange(page)[:, None] <= (np.arange(page * h_a)[None, :] // h_a), dtype=BF16)

    kc = cache_fox_k.reshape(depth * n_phys, page * h_a, dh_a)
    vc = cache_fox_v.reshape(depth * n_phys, page * h_a, dh_a)
    lf_t = jnp.swapaxes(cache_fox_logf, 2, 3).reshape(depth * n_phys, h_a, page)
    mk2 = cache_mem_k.reshape(depth * nb_s, n_mem * h_x, dh_x)
    mv2 = cache_mem_v.reshape(depth * nb_s, n_mem * h_x, dh_x)

    xp = x_prompt.reshape(nb_p * t_p, d_model)
    xs = x_sample.reshape(nb_s, d_model)
    memf = mem_prompt.reshape(nb_p * n_mem, d_model)
    row1 = lambda a: a.reshape(1, -1).astype(F32)
    pad_cols = lambda a, w: jnp.pad(a, ((0, 0), (0, w - a.shape[1])))

    d_ff = w_ff1.shape[-1]
    stacked_bf16 = [w.astype(BF16) for w in
                    (w_out, w_xq, jnp.concatenate([w_xk, w_xv], axis=2), w_xo, w_ff1, w_ff2)]
    wq, wk, wv, wf, wzb, wxc, wgc = jnp.split(w_in.astype(BF16), splits, axis=2)
    w_in_all = jnp.concatenate(
        [wzb, wf, jnp.zeros((depth, d_model, zb_w - n_b_cols - h_a), BF16), wq, wk, wv, wxc, wgc], axis=2)
    p_states, s_states = [], []
    kv5 = None
    for l in range(depth):
        w_in_r = (w_in_all, l)
        bf_pad = pad_cols(row1(fox_bf[l]), LANES)
        mu_pad = pad_cols(row1(rw_mu[l]), zb_w)
        zrow = lambda r0, w, rows: jnp.pad(w, ((r0, lr_w - r0 - rows), (0, 0))).astype(BF16)
        w2p = zrow(0, rw_w2[l], r_dec)
        a2p = zrow(r_dec, rw_a2[l], r_icl)
        g2p = zrow(r_dec + r_icl, rw_g2[l], r_gate)
        wa_d = _block_diag(rg_wa[l]).astype(BF16)
        wx_d = _block_diag(rg_wx[l]).astype(BF16)
        w_out_b, w_xq_b, w_xkv_b, w_xo_b, w_ff1_b, w_ff2_b = [(w, l) for w in stacked_bf16]
        rwkv_par = dict(zb_w=zb_w, d_b=d_b, mu=mu_pad, w0=row1(rw_w0[l]), a0=row1(rw_a0[l]), w2p=w2p, a2p=a2p,
                        g2p=g2p, kk=row1(rw_kk[l]), ka=row1(rw_ka[l]), ones_h=ones_h)
        post_par = dict(lnx_w=row1(rw_lnx_w[l]), lnx_b=row1(rw_lnx_b[l]), rk=row1(rw_rk[l]), ones_h=ones_h, dh=dh_b)
        rg_par = dict(d_c=d_c, xc_blk0=xc_blk0, conv_w_arr=rg_conv_w[l], conv_b=rg_conv_b[l], wa_d=wa_d, wx_d=wx_d,
                      ba=rg_ba[l], bx=rg_bx[l], lam=rg_lambda[l])

        def tail(x, ya, yb, yc, attend, tm, tm_wide):
            segs = [(w_out_b, d_a, 0), (w_out_b, d_b, d_a // d_b), (w_out_b, d_c, (d_a + d_b) // d_c)]
            x = _matmul([ya, yb, yc], segs, residual=x, tm=tm_wide, name="mix_out_proj")
            q = _matmul([x], [(w_xq_b, d_model, 0)], gain=norm_x[l], tm=tm_wide, name="mem_q_proj")
            o = attend(q)
            x = _matmul([o], [(w_xo_b, d_x, 0)], residual=x, tm=tm_wide, name="mem_out_proj")
            hid = _matmul([x], [(w_ff1_b, d_model, 0)], gain=norm_ff[l], epilogue="relu2", out_dtype=BF16,
                          tm=tm_wide, tn=1024, name="ff_up")
            return _matmul([hid], [(w_ff2_b, d_ff, 0)], residual=x, tm=tm, name="ff_down")

        m_p = nb_p * t_p
        mkv = _matmul([memf], [(w_xkv_b, d_model, 0)], gain=norm_mem[l], tm=512, name="mem_kv_proj")
        z = _matmul([xp], [(w_in_r, d_model, 0)], gain=norm_mix[l], tm=1024, tn=1024, name="mix_in_proj")
        lf, c_t = _logf_cumsum(z, fa_blk, bf_pad, nb_p, t_p)
        c_row = c_t[:, :h_a, :].reshape(nb_p * h_a, t_p)
        ya = _fox_prompt(z, c_row, n=nb_p, t=t_p, n_heads=h_a, dh=dh_a, q_blk0=q_blk0)
        r, lw, km, v, kn, bt, g = _rwkv_prep(z, None, seq_len=t_p, **rwkv_par)
        y, h_pair = _rwkv_chunk(r, lw, km, v, kn, bt, n=nb_p, t=t_p, d_b=d_b, dh=dh_b)
        yb = _rwkv_post(y, r, km, v, g, **post_par)
        yc, conv1, h1 = _rglru_seq(z, n=nb_p, t=t_p, **rg_par)
        xp = tail(xp, ya, yb, yc,
                  lambda q: _mem_attn(q, mkv, n=nb_p, t=t_p, n_mem=n_mem, n_heads=h_x, dh=dh_x), 512, 1024)
        z3 = z.reshape(nb_p, t_p, -1)
        hp = h_pair.reshape(nb_p, d_b // LANES, 2, dh_b, 2, dh_b)
        wkv = jnp.stack([hp[:, :, 0, :, 0, :], hp[:, :, 1, :, 1, :]], axis=2)
        wkv = jnp.swapaxes(wkv, -1, -2).reshape(nb_p, h_b, dh_b, dh_b)
        kv5 = _kv_cache_rows(z, kv5, layer=l, depth=depth, n=nb_p, t=t_p, n_heads=h_a, dh=dh_a,
                             k_blk0=(zb_w + d_a) // LANES)
        p_states.append((
            lf.reshape(nb_p, t_p, LANES)[:, :, :h_a],
            z3[:, t_p - 1, :n_b_cols],
            wkv,
            conv1,
            h1.reshape(nb_p, d_c),
            mkv[:, :d_x].reshape(nb_p, n_mem, h_x, dh_x),
            mkv[:, d_x:].reshape(nb_p, n_mem, h_x, dh_x),
        ))

        zs = _matmul([xs], [(w_in_r, d_model, 0)], gain=norm_mix[l], tm=nb_s, name="mix_in_proj_s")
        lf_s, _ = _logf_cumsum(zs, fa_blk, bf_pad, 1, nb_s)
        hd = lambda a: a.reshape(nb_s, h_a, dh_a)
        q_s = hd(zs[:, zb_w:zb_w + d_a])
        k_s = hd(zs[:, zb_w + d_a:zb_w + 2 * d_a])
        v_s = hd(zs[:, zb_w + 2 * d_a:zb_w + 3 * d_a])
        lfn = jnp.broadcast_to(lf_s[:, :h_a, None], (nb_s, h_a, dh_a))
        ya_s = _fox_decode(page_table, q_s, k_s, v_s, lfn, kc, vc, lf_t, g_mat, layer=l, n_phys=n_phys)
        ya_s = ya_s.reshape(nb_s, d_a).astype(BF16)
        zprev = pad_cols(state_rwkv_shift[l], zb_w)
        r, lw, km, v, kn, bt, g = _rwkv_prep(zs, zprev, seq_len=1, **rwkv_par)
        y, wkv_s = _rwkv_step(r, lw, km, v, kn, bt, state_rwkv_wkv.reshape(depth * nb_s, h_b, dh_b, dh_b),
                              ones_pair, dh=dh_b, layer=l)
        yb_s = _rwkv_post(y, r, km, v, g, **post_par)
        yc_s, conv1_s, h1_s = _rglru_step(zs, state_rglru_conv[l], state_rglru_h[l], **rg_par)
        xs = tail(xs, ya_s, yb_s, yc_s,
                  lambda q: _mem_decode(q.reshape(nb_s, h_x, dh_x), mk2, mv2, layer=l).reshape(nb_s, d_x),
                  nb_s, nb_s)
        s_states.append((
            k_s.reshape(nb_s, 1, h_a, dh_a),
            v_s.reshape(nb_s, 1, h_a, dh_a),
            lf_s[:, :h_a].reshape(nb_s, 1, h_a),
            zs[:, :n_b_cols],
            wkv_s,
            conv1_s,
            h1_s,
        ))

    y_prompt = _rmsnorm(xp, norm_f).reshape(nb_p, t_p, d_model)
    y_sample = _rmsnorm(xs, norm_f).reshape(nb_s, 1, d_model)
    p_out = [jnp.stack(s) for s in zip(*p_states)]
    s_out = [jnp.stack(s) for s in zip(*s_states)]
    fox_k_p, fox_v_p = [a.reshape(depth, nb_p, t_p, h_a, dh_a) for a in kv5]
    return (y_prompt, y_sample, fox_k_p, fox_v_p, *p_out, *s_out)
```

```python
import functools

import numpy as np
import jax
import jax.numpy as jnp
from jax import lax
from jax.experimental import pallas as pl
from jax.experimental.pallas import tpu as pltpu

F32 = jnp.float32
BF16 = jnp.bfloat16

NORM_EPS = 1e-6
LNX_EPS = 64e-5
RG_C = 8.0
NEG_INF = -1e30

LANES = 128
SUBLANES = 8
VMEM_CAP_BYTES = 60000 * 1024
RWKV_CHUNK = 64


def _cparams(semantics, est_bytes):
    limit = int(min(max(2 * est_bytes + (8 << 20), 24 << 20), VMEM_CAP_BYTES))
    return pltpu.CompilerParams(dimension_semantics=semantics, vmem_limit_bytes=limit)


def _split_bf16(x, parts):
    out = []
    r = x
    for i in range(parts):
        h = r.astype(BF16)
        out.append(h)
        if i + 1 < parts:
            r = r - h.astype(F32)
    return out


def _dot_nn(a, b):
    return jnp.dot(a, b, preferred_element_type=F32)


def _dot_nt(a, b):
    return lax.dot_general(a, b, (((1,), (1,)), ((), ())), preferred_element_type=F32)


def _dot_x_exact(x, w_exact, parts=3):
    return sum(_dot_nn(p, w_exact) for p in _split_bf16(x, parts))


def _dot_exact_x(w_exact, x, parts=3):
    return sum(_dot_nn(w_exact, p) for p in _split_bf16(x, parts))


def _dot3(a, b, nt=False):
    f = _dot_nt if nt else _dot_nn
    ah, al = _split_bf16(a, 2)
    bh, bl = _split_bf16(b, 2)
    return f(ah, bh) + f(ah, bl) + f(al, bh)


def _softplus(x):
    return jnp.maximum(x, 0.0) + jnp.log1p(jnp.exp(-jnp.abs(x)))


def _log_sigmoid(x):
    return -_softplus(-x)


def _gelu_tanh(x):
    c = np.float32(np.sqrt(2.0 / np.pi))
    return 0.5 * x * (1.0 + jnp.tanh(c * (x + 0.044715 * (x * x * x))))


def _mm_kernel(*refs, n_seg, has_gain, has_res, epilogue):
    xs = refs[:n_seg]
    ws = refs[n_seg:2 * n_seg]
    pos = 2 * n_seg
    g_ref = res_ref = None
    if has_gain:
        g_ref = refs[pos]
        pos += 1
    if has_res:
        res_ref = refs[pos]
        pos += 1
    o_ref = refs[pos]
    if has_gain:
        xn_ref = refs[pos + 1]

        @pl.when(pl.program_id(1) == 0)
        def _():
            x = xs[0][...]
            ms = jnp.mean(x * x, axis=-1, keepdims=True)
            xn_ref[...] = (x * lax.rsqrt(ms + NORM_EPS) * g_ref[...]).astype(BF16)

        acc = _dot_nn(xn_ref[...], ws[0][...])
    else:
        acc = _dot_nn(xs[0][...], ws[0][...])
        for x_ref, w_ref in zip(xs[1:], ws[1:]):
            acc = acc + _dot_nn(x_ref[...], w_ref[...])
    if epilogue == "relu2":
        r = jnp.maximum(acc, 0.0)
        acc = r * r
    if has_res:
        acc = acc + res_ref[...]
    o_ref[...] = acc.astype(o_ref.dtype)


def _matmul(xs, ws, *, gain=None, residual=None, epilogue="none", out_dtype=F32, tm=512, tn=512, name="mm"):
    m = xs[0].shape[0]
    w0 = ws[0][0]
    n = (w0[0] if isinstance(w0, tuple) else w0).shape[-1]
    tm = min(tm, m)
    tn = min(tn, n)
    while n % tn:
        tn //= 2
    assert m % tm == 0 and tn % LANES == 0
    in_specs, args = [], []
    est = 0
    for x in xs:
        k = x.shape[1]
        in_specs.append(pl.BlockSpec((tm, k), lambda i, j: (i, 0)))
        args.append(x)
        est += 2 * tm * k * x.dtype.itemsize
    for (w, k, rb) in ws:
        if isinstance(w, tuple):
            w, layer = w
            in_specs.append(pl.BlockSpec((None, k, tn), lambda i, j, rb=rb, layer=layer: (layer, rb, j)))
        else:
            in_specs.append(pl.BlockSpec((k, tn), lambda i, j, rb=rb: (rb, j)))
        args.append(w)
        est += 2 * k * tn * w.dtype.itemsize
    scratch = []
    if gain is not None:
        k = xs[0].shape[1]
        in_specs.append(pl.BlockSpec((1, k), lambda i, j: (0, 0)))
        args.append(gain.reshape(1, k).astype(F32))
        scratch.append(pltpu.VMEM((tm, k), BF16))
        est += tm * k * 2
    if residual is not None:
        in_specs.append(pl.BlockSpec((tm, tn), lambda i, j: (i, j)))
        args.append(residual)
        est += 2 * tm * tn * 4
    est += 3 * tm * tn * 4
    kern = functools.partial(_mm_kernel, n_seg=len(xs), has_gain=gain is not None,
                             has_res=residual is not None, epilogue=epilogue)
    return pl.pallas_call(
        kern,
        grid=(m // tm, n // tn),
        in_specs=in_specs,
        out_specs=pl.BlockSpec((tm, tn), lambda i, j: (i, j)),
        out_shape=jax.ShapeDtypeStruct((m, n), out_dtype),
        scratch_shapes=scratch,
        compiler_params=_cparams(("parallel", "arbitrary"), est),
        name=name,
    )(*args)


def _rmsnorm_kernel(x_ref, g_ref, o_ref):
    x = x_ref[...]
    ms = jnp.mean(x * x, axis=-1, keepdims=True)
    o_ref[...] = x * lax.rsqrt(ms + NORM_EPS) * g_ref[...]


def _rmsnorm(x, g, tm=1024):
    m, d = x.shape
    tm = min(tm, m)
    return pl.pallas_call(
        _rmsnorm_kernel,
        grid=(m // tm,),
        in_specs=[pl.BlockSpec((tm, d), lambda i: (i, 0)), pl.BlockSpec((1, d), lambda i: (0, 0))],
        out_specs=pl.BlockSpec((tm, d), lambda i: (i, 0)),
        out_shape=jax.ShapeDtypeStruct((m, d), F32),
        compiler_params=_cparams(("parallel",), 4 * tm * d * 4),
        name="final_rmsnorm",
    )(x, g.reshape(1, d))


def _logf_kernel(fa_ref, bf_ref, lf_ref, ct_ref, *, t, blk):
    row = lax.broadcasted_iota(jnp.int32, (blk, blk), 0)
    col = lax.broadcasted_iota(jnp.int32, (blk, blk), 1)
    tri = (col <= row).astype(BF16)
    carry = jnp.zeros((1, LANES), F32)
    for b in range(t // blk):
        sl = pl.ds(b * blk, blk)
        lf = _log_sigmoid(fa_ref[sl, :] + bf_ref[...])
        lf_ref[sl, :] = lf
        c = _dot_exact_x(tri, lf) + carry
        ct_ref[:, sl] = c.T
        carry = c[blk - 1:blk, :]


def _logf_cumsum(z, fa_blk, bf_pad, n, t):
    blk = min(t, 256)
    kern = functools.partial(_logf_kernel, t=t, blk=blk)
    return pl.pallas_call(
        kern,
        grid=(n,),
        in_specs=[pl.BlockSpec((t, LANES), lambda i: (i, fa_blk)),
                  pl.BlockSpec((1, LANES), lambda i: (0, 0))],
        out_specs=[pl.BlockSpec((t, LANES), lambda i: (i, 0)),
                   pl.BlockSpec((None, LANES, t), lambda i: (i, 0, 0))],
        out_shape=[jax.ShapeDtypeStruct((n * t, LANES), F32),
                   jax.ShapeDtypeStruct((n, LANES, t), F32)],
        compiler_params=_cparams(("parallel",), 10 * t * LANES * 4),
        name="logf_cumsum",
    )(z, bf_pad)


def _fox_attn_kernel(q_ref, k_ref, v_ref, ck_ref, o_ref, m_sc, l_sc, acc_sc, *, scale, tq, dh, hb):
    qi = pl.program_id(2)
    m_sc[...] = jnp.full(m_sc.shape, NEG_INF, F32)
    l_sc[...] = jnp.zeros(l_sc.shape, F32)
    acc_sc[...] = jnp.zeros(acc_sc.shape, F32)
    cols = [slice(j * dh, (j + 1) * dh) for j in range(hb)]
    qs = [q_ref[:, cols[j]].astype(BF16) for j in range(hb)]

    def block(ki, diagonal):
        rows = pl.ds(pl.multiple_of(ki * tq, tq), tq)
        ss = [_dot_nt(qs[j], k_ref[rows, cols[j]].astype(BF16)) for j in range(hb)]
        ps, alphas = [], []
        for j in range(hb):
            s = ss[j] * scale - ck_ref[j, ki]
            if diagonal:
                causal = (lax.broadcasted_iota(jnp.int32, (tq, tq), 1)
                          <= lax.broadcasted_iota(jnp.int32, (tq, tq), 0))
                s = jnp.where(causal, s, NEG_INF)
            m_old = m_sc[j]
            m_new = jnp.maximum(m_old, jnp.max(s, axis=1, keepdims=True))
            alpha = jnp.exp(m_old - m_new)
            p = jnp.exp(s - m_new)
            l_sc[j] = alpha * l_sc[j] + jnp.sum(p, axis=1, keepdims=True)
            m_sc[j] = m_new
            ps.append(p.astype(BF16))
            alphas.append(alpha)
        pvs = [_dot_nn(ps[j], v_ref[rows, cols[j]].astype(BF16)) for j in range(hb)]
        for j in range(hb):
            acc_sc[j] = alphas[j] * acc_sc[j] + pvs[j]

    def body(ki, carry):
        block(ki, False)
        return carry

    lax.fori_loop(0, qi, body, 0)
    block(qi, True)
    for j in range(hb):
        o_ref[:, cols[j]] = (acc_sc[j] / l_sc[j]).astype(o_ref.dtype)


def _fox_prompt(z, c_row, *, n, t, n_heads, dh, q_blk0, tq=512, hb=8):
    tq = min(tq, t)
    nq = t // tq
    hb = min(hb, n_heads)
    assert n_heads % hb == 0 and q_blk0 * LANES % (hb * dh) == 0
    ng = n_heads // hb
    g0 = q_blk0 * LANES // (hb * dh)
    kern = functools.partial(_fox_attn_kernel, scale=float(dh) ** -0.5, tq=tq, dh=dh, hb=hb)
    return pl.pallas_call(
        kern,
        grid=(n, ng, nq),
        in_specs=[
            pl.BlockSpec((tq, hb * dh), lambda b, g, qi: (b * nq + qi, g0 + g)),
            pl.BlockSpec((t, hb * dh), lambda b, g, qi: (b, g0 + ng + g)),
            pl.BlockSpec((t, hb * dh), lambda b, g, qi: (b, g0 + 2 * ng + g)),
            pl.BlockSpec((hb, nq, 1, tq), lambda b, g, qi: (b * ng + g, 0, 0, 0)),
        ],
        out_specs=pl.BlockSpec((tq, hb * dh), lambda b, g, qi: (b * nq + qi, g)),
        out_shape=jax.ShapeDtypeStruct((n * t, n_heads * dh), BF16),
        scratch_shapes=[pltpu.VMEM((hb, tq, 1), F32), pltpu.VMEM((hb, tq, 1), F32),
                        pltpu.VMEM((hb, tq, dh), F32)],
        compiler_params=_cparams(("parallel", "parallel", "arbitrary"),
                                 hb * (4 * t * dh * 4 + 8 * tq * dh * 4 + 6 * tq * tq * 4)),
        name="fox_prompt_attn",
    )(z, z, z, c_row.reshape(n * n_heads, nq, 1, tq))


def _kv_cache_rows_kernel(*refs, n_heads, dh, has_prev):
    k_ref, v_ref = refs[:2]
    k5_ref, v5_ref = refs[-2:]
    tb = k_ref.shape[0]
    for h in range(n_heads):
        dst = pl.ds(h, tb, stride=n_heads)
        k5_ref[dst, :] = k_ref[:, h * dh:(h + 1) * dh]
        v5_ref[dst, :] = v_ref[:, h * dh:(h + 1) * dh]


def _kv_cache_rows(z, prev, *, layer, depth, n, t, n_heads, dh, k_blk0, tb=512):
    tb = min(tb, t)
    nt = t // tb
    d_a = n_heads * dh
    kb = k_blk0 * LANES // d_a
    assert k_blk0 * LANES % d_a == 0 and t % tb == 0
    shape = jax.ShapeDtypeStruct((depth, n, t * n_heads, dh), F32)
    out_spec = pl.BlockSpec((None, None, tb * n_heads, dh), lambda b, i: (layer, b, i, 0))
    in_specs = [pl.BlockSpec((tb, d_a), lambda b, i: (b * nt + i, kb)),
                pl.BlockSpec((tb, d_a), lambda b, i: (b * nt + i, kb + 1))]
    args = [z, z]
    aliases = {}
    if prev is not None:
        in_specs += [pl.BlockSpec(memory_space=pl.ANY)] * 2
        args += list(prev)
        aliases = {2: 0, 3: 1}
    kern = functools.partial(_kv_cache_rows_kernel, n_heads=n_heads, dh=dh, has_prev=prev is not None)
    return pl.pallas_call(
        kern,
        grid=(n, nt),
        in_specs=in_specs,
        out_specs=[out_spec, out_spec],
        out_shape=[shape, shape],
        input_output_aliases=aliases,
        compiler_params=_cparams(("parallel", "parallel"), 8 * tb * d_a * 4),
        name="kv_cache_rows",
    )(*args)


def _head_valid(n_heads, rows):
    lane = lax.broadcasted_iota(jnp.int32, (n_heads, rows), 1)
    sub = lax.broadcasted_iota(jnp.int32, (n_heads, rows), 0)
    return (lane & (n_heads - 1)) == sub


def _fox_decode_kernel(pt_ref, q_ref, kn_ref, vn_ref, lfn_ref, *rest, n_heads, scale, g):
    kcs, vcs, lfs = rest[0:g], rest[g:2 * g], rest[2 * g:3 * g]
    g_ref, o_ref, m_sc, l_sc, acc_sc, car_sc = rest[3 * g:]
    p = pl.program_id(1)

    @pl.when(p == 0)
    def _():
        m_sc[...] = jnp.full(m_sc.shape, NEG_INF, F32)
        l_sc[...] = jnp.zeros(l_sc.shape, F32)
        acc_sc[...] = jnp.zeros(acc_sc.shape, F32)
        car_sc[...] = jnp.zeros(car_sc.shape, F32)

    rows = kcs[0].shape[0]
    qb = q_ref[...].astype(BF16)
    valid = _head_valid(n_heads, rows)
    f_all = jnp.concatenate([lf[...] for lf in lfs], axis=0)
    cum_all = _dot_x_exact(f_all, g_ref[...])
    car = car_sc[...]
    ss = []
    for j in range(g):
        s = _dot_nt(qb, kcs[j][...].astype(BF16)) * scale
        ss.append(jnp.where(valid, s - (car + cum_all[j * n_heads:(j + 1) * n_heads]), NEG_INF))
        car = car + jnp.sum(lfs[j][...], axis=1, keepdims=True)
    car_new = car
    m_old = m_sc[...]
    m_new = m_old
    for s in ss:
        m_new = jnp.maximum(m_new, jnp.max(s, axis=1, keepdims=True))
    alpha = jnp.exp(m_old - m_new)
    l_new = alpha * l_sc[...]
    acc_new = alpha * acc_sc[...]
    for j in range(g):
        pr = jnp.exp(ss[j] - m_new)
        l_new = l_new + jnp.sum(pr, axis=1, keepdims=True)
        acc_new = acc_new + _dot_nn(pr.astype(BF16), vcs[j][...].astype(BF16))
    m_sc[...] = m_new
    l_sc[...] = l_new
    acc_sc[...] = acc_new
    car_sc[...] = car_new

    @pl.when(p == pl.num_programs(1) - 1)
    def _():
        s_new = jnp.sum(q_ref[...] * kn_ref[...], axis=1, keepdims=True) * scale - (car_new + lfn_ref[:, 0:1])
        m2 = jnp.maximum(m_new, s_new)
        a2 = jnp.exp(m_new - m2)
        pn = jnp.exp(s_new - m2)
        o_ref[...] = (a2 * acc_new + pn * vn_ref[...]) / (a2 * l_new + pn)


def _fox_decode(page_table, q, k_new, v_new, lf_new, kc, vc, lf_t, g_mat, *, layer, n_phys, g=8):
    b, n_heads, dh = q.shape
    n_pages = page_table.shape[1]
    g = min(g, n_pages)
    assert n_pages % g == 0
    rows = kc.shape[1]
    page = lf_t.shape[2]
    base = layer * n_phys
    tok = pl.BlockSpec((None, n_heads, dh), lambda i, p, pt: (i, 0, 0))
    pg = lambda j: (lambda i, p, pt: (base + pt[i, p * g + j], 0, 0))
    kern = functools.partial(_fox_decode_kernel, n_heads=n_heads, scale=float(dh) ** -0.5, g=g)
    return pl.pallas_call(
        kern,
        grid_spec=pltpu.PrefetchScalarGridSpec(
            num_scalar_prefetch=1,
            grid=(b, n_pages // g),
            in_specs=([tok] * 4
                      + [pl.BlockSpec((None, rows, dh), pg(j)) for j in range(g)]
                      + [pl.BlockSpec((None, rows, dh), pg(j)) for j in range(g)]
                      + [pl.BlockSpec((None, n_heads, page), pg(j)) for j in range(g)]
                      + [pl.BlockSpec((page, rows), lambda i, p, pt: (0, 0))]),
            out_specs=tok,
            scratch_shapes=[pltpu.VMEM((n_heads, 1), F32), pltpu.VMEM((n_heads, 1), F32),
                            pltpu.VMEM((n_heads, dh), F32), pltpu.VMEM((n_heads, 1), F32)],
        ),
        out_shape=jax.ShapeDtypeStruct((b, n_heads, dh), F32),
        compiler_params=_cparams(("parallel", "arbitrary"), g * 4 * rows * dh * 4 + 2 * page * rows * 2),
        name="fox_decode_attn",
    )(page_table, q, k_new, v_new, lf_new, *([kc] * g), *([vc] * g), *([lf_t] * g), g_mat)


def _mem_decode_kernel(q_ref, k_ref, v_ref, o_ref, *, n_heads, scale, bb):
    rows = k_ref.shape[1]
    valid = _head_valid(n_heads, rows)
    ss = [_dot_nt(q_ref[j].astype(BF16), k_ref[j].astype(BF16)) for j in range(bb)]
    ps, ls = [], []
    for s in ss:
        s = jnp.where(valid, s * scale, NEG_INF)
        p = jnp.exp(s - jnp.max(s, axis=1, keepdims=True))
        ls.append(jnp.sum(p, axis=1, keepdims=True))
        ps.append(p.astype(BF16))
    os_ = [_dot_nn(ps[j], v_ref[j].astype(BF16)) for j in range(bb)]
    for j in range(bb):
        o_ref[j] = (os_[j] / ls[j]).astype(o_ref.dtype)


def _mem_decode(q, k2, v2, *, layer, bb=4):
    b, n_heads, dh = q.shape
    rows = k2.shape[1]
    bb = min(bb, b)
    assert b % bb == 0
    base = layer * b // bb
    kern = functools.partial(_mem_decode_kernel, n_heads=n_heads, scale=float(dh) ** -0.5, bb=bb)
    return pl.pallas_call(
        kern,
        grid=(b // bb,),
        in_specs=[pl.BlockSpec((bb, n_heads, dh), lambda i: (i, 0, 0)),
                  pl.BlockSpec((bb, rows, dh), lambda i: (base + i, 0, 0)),
                  pl.BlockSpec((bb, rows, dh), lambda i: (base + i, 0, 0))],
        out_specs=pl.BlockSpec((bb, n_heads, dh), lambda i: (i, 0, 0)),
        out_shape=jax.ShapeDtypeStruct((b, n_heads, dh), BF16),
        compiler_params=_cparams(("parallel",), 4 * bb * rows * dh * 4),
        name="mem_decode_attn",
    )(q, k2, v2)


def _mem_attn_kernel(q_ref, k_ref, v_ref, o_ref, *, scale):
    s = _dot_nt(q_ref[...].astype(BF16), k_ref[...].astype(BF16)) * scale
    m = jnp.max(s, axis=1, keepdims=True)
    p = jnp.exp(s - m)
    l = jnp.sum(p, axis=1, keepdims=True)
    o_ref[...] = (_dot_nn(p.astype(BF16), v_ref[...].astype(BF16)) / l).astype(o_ref.dtype)


def _mem_attn(q, kv, *, n, t, n_mem, n_heads, dh, tq=1024):
    tq = min(tq, t)
    nq = t // tq
    kern = functools.partial(_mem_attn_kernel, scale=float(dh) ** -0.5)
    return pl.pallas_call(
        kern,
        grid=(n, n_heads, nq),
        in_specs=[pl.BlockSpec((tq, dh), lambda b, h, qi: (b * nq + qi, h)),
                  pl.BlockSpec((n_mem, dh), lambda b, h, qi: (b, h)),
                  pl.BlockSpec((n_mem, dh), lambda b, h, qi: (b, n_heads + h))],
        out_specs=pl.BlockSpec((tq, dh), lambda b, h, qi: (b * nq + qi, h)),
        out_shape=jax.ShapeDtypeStruct((n * t, n_heads * dh), BF16),
        compiler_params=_cparams(("parallel", "parallel", "parallel"), 8 * tq * dh * 4 + 4 * tq * n_mem * 4),
        name="mem_attn",
    )(q, kv, kv)


def _rglru_gates(u, wa_ref, wx_ref, ba_ref, bx_ref, lam_ref):
    ub = u.astype(BF16)
    gate_a = jax.nn.sigmoid(_dot_nn(ub, wa_ref[...]) + ba_ref[...])
    gate_x = jax.nn.sigmoid(_dot_nn(ub, wx_ref[...]) + bx_ref[...])
    log_a = -RG_C * gate_a * _softplus(-lam_ref[...])
    a = jnp.exp(log_a)
    b = u * gate_x * jnp.sqrt(1.0 - jnp.exp(2.0 * log_a))
    return a, b


def _shift_rows(x, d, fill):
    t = x.shape[0]
    if d % SUBLANES == 0:
        return jnp.concatenate([jnp.full((d, x.shape[1]), fill, x.dtype), x[:t - d]], axis=0)
    rolled = pltpu.roll(x, d, 0)
    row = lax.broadcasted_iota(jnp.int32, x.shape, 0)
    return jnp.where(row < d, fill, rolled)


def _rglru_seq_kernel(xc_ref, gc_ref, cw_ref, cb_ref, wa_ref, wx_ref, ba_ref, bx_ref, lam_ref,
                      y_ref, conv_ref, h_ref, xpad_sc, *, t, conv_w):
    xpad_sc[0:SUBLANES, :] = jnp.zeros((SUBLANES, LANES), F32)
    xpad_sc[SUBLANES:, :] = xc_ref[...]
    u = cb_ref[...] + jnp.zeros((t, LANES), F32)
    for j in range(conv_w):
        u = u + xpad_sc[pl.ds(SUBLANES - (conv_w - 1) + j, t), :] * cw_ref[j:j + 1, :]
    a, b = _rglru_gates(u, wa_ref, wx_ref, ba_ref, bx_ref, lam_ref)
    d = 1
    while d < t:
        a_sh = _shift_rows(a, d, 1.0)
        b_sh = _shift_rows(b, d, 0.0)
        b = a * b_sh + b
        a = a * a_sh
        d *= 2
    y_ref[...] = (b * _gelu_tanh(gc_ref[...])).astype(y_ref.dtype)
    h_ref[...] = b[t - 1:t, :]
    conv_ref[...] = xpad_sc[pl.ds(SUBLANES + t - (conv_w - 1), conv_w - 1), :]


def _rglru_seq(z, *, n, t, d_c, xc_blk0, conv_w_arr, conv_b, wa_d, wx_d, ba, bx, lam):
    nc = d_c // LANES
    conv_w = conv_w_arr.shape[0]
    vec = lambda a: a.reshape(1, d_c)
    vspec = pl.BlockSpec((1, LANES), lambda b, c: (0, c))
    kern = functools.partial(_rglru_seq_kernel, t=t, conv_w=conv_w)
    return pl.pallas_call(
        kern,
        grid=(n, nc),
        in_specs=[pl.BlockSpec((t, LANES), lambda b, c: (b, xc_blk0 + c)),
                  pl.BlockSpec((t, LANES), lambda b, c: (b, xc_blk0 + nc + c)),
                  pl.BlockSpec((conv_w, LANES), lambda b, c: (0, c)),
                  vspec,
                  pl.BlockSpec((LANES, LANES), lambda b, c: (c, c)),
                  pl.BlockSpec((LANES, LANES), lambda b, c: (c, c)),
                  vspec, vspec, vspec],
        out_specs=[pl.BlockSpec((t, LANES), lambda b, c: (b, c)),
                   pl.BlockSpec((None, conv_w - 1, LANES), lambda b, c: (b, 0, c)),
                   pl.BlockSpec((None, 1, LANES), lambda b, c: (b, 0, c))],
        out_shape=[jax.ShapeDtypeStruct((n * t, d_c), BF16),
                   jax.ShapeDtypeStruct((n, conv_w - 1, d_c), F32),
                   jax.ShapeDtypeStruct((n, 1, d_c), F32)],
        scratch_shapes=[pltpu.VMEM((t + SUBLANES, LANES), F32)],
        compiler_params=_cparams(("parallel", "parallel"), 16 * t * LANES * 4),
        name="rglru_seq",
    )(z, z, conv_w_arr, vec(conv_b), wa_d, wx_d, vec(ba), vec(bx), vec(lam))


def _rglru_step_kernel(xc_ref, gc_ref, c0_ref, h0_ref, cw_ref, cb_ref, wa_ref, wx_ref, ba_ref, bx_ref, lam_ref,
                       y_ref, conv_ref, h_ref, *, conv_w):
    xc = xc_ref[...]
    u = cb_ref[...] + xc * cw_ref[conv_w - 1:conv_w, :]
    for j in range(conv_w - 1):
        u = u + c0_ref[:, j, :] * cw_ref[j:j + 1, :]
    a, b = _rglru_gates(u, wa_ref, wx_ref, ba_ref, bx_ref, lam_ref)
    h = a * h0_ref[...] + b
    y_ref[...] = (h * _gelu_tanh(gc_ref[...])).astype(y_ref.dtype)
    h_ref[...] = h
    for j in range(conv_w - 2):
        conv_ref[:, j, :] = c0_ref[:, j + 1, :]
    conv_ref[:, conv_w - 2, :] = xc


def _rglru_step(z, conv0, h0, *, d_c, xc_blk0, conv_w_arr, conv_b, wa_d, wx_d, ba, bx, lam):
    bsz = z.shape[0]
    conv_w = conv_w_arr.shape[0]
    ncb = d_c // LANES
    vec = lambda a: a.reshape(1, d_c)
    full = lambda shape: pl.BlockSpec(shape, lambda i: (0,) * len(shape))
    kern = functools.partial(_rglru_step_kernel, conv_w=conv_w)
    return pl.pallas_call(
        kern,
        grid=(1,),
        in_specs=[pl.BlockSpec((bsz, d_c), lambda i: (0, xc_blk0 * LANES // d_c)),
                  pl.BlockSpec((bsz, d_c), lambda i: (0, xc_blk0 * LANES // d_c + 1)),
                  full((bsz, conv_w - 1, d_c)), full((bsz, d_c)), full((conv_w, d_c)), full((1, d_c)),
                  full((d_c, d_c)), full((d_c, d_c)), full((1, d_c)), full((1, d_c)), full((1, d_c))],
        out_specs=[full((bsz, d_c)), full((bsz, conv_w - 1, d_c)), full((bsz, d_c))],
        out_shape=[jax.ShapeDtypeStruct((bsz, d_c), BF16),
                   jax.ShapeDtypeStruct((bsz, conv_w - 1, d_c), F32),
                   jax.ShapeDtypeStruct((bsz, d_c), F32)],
        compiler_params=_cparams(("arbitrary",), 16 * bsz * d_c * 4 + 4 * d_c * d_c * 2),
        name="rglru_step",
    )(z, z, conv0, h0, conv_w_arr, vec(conv_b), wa_d, wx_d, vec(ba), vec(bx), vec(lam))


def _rwkv_prep_kernel(*refs, d_b, seq_mode, blocks_per_seq):
    if seq_mode:
        zb_ref, prev_ref = refs[:2]
    else:
        zb_ref, zp_ref = refs[:2]
    (mu_ref, w0_ref, a0_ref, w2_ref, a2_ref, g2_ref, kk_ref, ka_ref, ones_ref,
     r_ref, lw_ref, km_ref, v_ref, kn_ref, bt_ref, g_ref) = refs[2:18]
    zb = zb_ref[...]
    tm = zb.shape[0]
    if seq_mode:
        sh_sc = refs[18]
        first = (pl.program_id(0) % blocks_per_seq) == 0
        prev = jnp.where(first, 0.0, prev_ref[...])
        sh_sc[0:SUBLANES, :] = prev
        sh_sc[SUBLANES:, :] = zb
        zp = sh_sc[pl.ds(SUBLANES - 1, tm), :]
    else:
        zp = zp_ref[...]
    zs = zb + mu_ref[...] * (zp - zb)
    r = zs[:, 0:d_b]
    k = zs[:, d_b:2 * d_b]
    v = zs[:, 2 * d_b:3 * d_b]
    lr = zs[:, 3 * d_b:3 * d_b + w2_ref.shape[0]]
    w_lin = _dot_nn(jnp.tanh(lr).astype(BF16), w2_ref[...])
    a_lin = _dot_nn(lr.astype(BF16), a2_ref[...])
    g = _dot_nn(jax.nn.sigmoid(lr).astype(BF16), g2_ref[...])
    w = -_softplus(-(w0_ref[...] + w_lin)) - 0.5
    a = jax.nn.sigmoid(a0_ref[...] + a_lin)
    kk = k * kk_ref[...]
    nrm2 = _dot_x_exact(kk * kk, ones_ref[...], parts=2)
    kn = kk / jnp.maximum(jnp.sqrt(nrm2), 1e-12)
    r_ref[...] = r
    lw_ref[...] = -jnp.exp(w)
    km_ref[...] = k * (1.0 + (a - 1.0) * ka_ref[...])
    v_ref[...] = v
    kn_ref[...] = kn
    bt_ref[...] = kn * a
    g_ref[...] = g


def _rwkv_prep(z, zprev, *, zb_w, d_b, mu, w0, a0, w2p, a2p, g2p, kk, ka, ones_h, seq_len, tm=512):
    m = z.shape[0]
    seq_mode = zprev is None
    tm = min(tm, seq_len if seq_mode else m)
    assert m % tm == 0
    row = lambda i: (i, 0)
    cst = lambda i: (0, 0)
    if seq_mode:
        per8 = tm // SUBLANES
        second = pl.BlockSpec((SUBLANES, zb_w), lambda i: (jnp.maximum(i * per8 - 1, 0), 0))
        second_arg = z
        scratch = [pltpu.VMEM((tm + SUBLANES, zb_w), F32)]
        bps = seq_len // tm
    else:
        second = pl.BlockSpec((tm, zb_w), row)
        second_arg = zprev
        scratch = []
        bps = 1
    lrw = w2p.shape[0]
    kern = functools.partial(_rwkv_prep_kernel, d_b=d_b, seq_mode=seq_mode, blocks_per_seq=bps)
    out = jax.ShapeDtypeStruct((m, d_b), F32)
    return pl.pallas_call(
        kern,
        grid=(m // tm,),
        in_specs=[pl.BlockSpec((tm, zb_w), row), second,
                  pl.BlockSpec((1, zb_w), cst), pl.BlockSpec((1, d_b), cst), pl.BlockSpec((1, d_b), cst),
                  pl.BlockSpec((lrw, d_b), cst), pl.BlockSpec((lrw, d_b), cst), pl.BlockSpec((lrw, d_b), cst),
                  pl.BlockSpec((1, d_b), cst), pl.BlockSpec((1, d_b), cst), pl.BlockSpec((d_b, d_b), cst)],
        out_specs=[pl.BlockSpec((tm, d_b), row)] * 7,
        out_shape=[out] * 7,
        scratch_shapes=scratch,
        compiler_params=_cparams(("parallel",), 6 * tm * zb_w * 4 + 30 * tm * d_b * 4),
        name="rwkv_prep",
    )(z, second_arg, mu, w0, a0, w2p, a2p, g2p, kk, ka, ones_h)


def _mm_p(a, b, passes, nt=False):
    if passes == 1:
        f = _dot_nt if nt else _dot_nn
        return f(a.astype(BF16), b.astype(BF16))
    return _dot3(a, b, nt=nt)


def _rwkv_chunk_kernel(*refs, c, dh, pb, nbb, passes):
    ins = [[ref.at[j] for ref in refs[:6]] for j in range(nbb)]
    y_refs = [refs[6].at[j] for j in range(nbb)]
    hout_ref, h_sc = refs[7:]
    ci = pl.program_id(2)

    @pl.when(ci == 0)
    def _():
        h_sc[...] = jnp.zeros(h_sc.shape, F32)

    c2 = 2 * c
    lane = lax.broadcasted_iota(jnp.int32, (c, LANES), 1)
    lane2 = lax.broadcasted_iota(jnp.int32, (c2, LANES), 1)
    t_idx = lax.broadcasted_iota(jnp.int32, (c, c2), 0)
    j_idx = lax.broadcasted_iota(jnp.int32, (c, c2), 1) & (c - 1)
    r128 = lax.broadcasted_iota(jnp.int32, (LANES, LANES), 0)
    c128 = lax.broadcasted_iota(jnp.int32, (LANES, LANES), 1)
    same_head = (r128 < dh) == (c128 < dh)
    zero_rows = jnp.zeros((c, c2), F32)
    head_masks = [lane2 < dh, lane2 >= dh]
    cat = lambda a, b: jnp.concatenate([a, b], axis=0)
    mm = functools.partial(_mm_p, passes=passes)

    units = []
    for j in range(nbb):
        r_ref, lw_ref, km_ref, v_ref, kn_ref, bt_ref = ins[j]
        lw_all = lw_ref[...]
        lc_all = lw_all
        d = 1
        while d < c:
            lc_all = lc_all + _shift_rows(lc_all, d, 0.0)
            d *= 2
        p_all = jnp.exp(lc_all)
        pinv_all = jnp.exp(-lc_all)
        pprev_all = jnp.exp(lc_all - lw_all)
        for pr in range(pb):
            sl = slice(pr * LANES, (pr + 1) * LANES)
            p = p_all[:, sl]
            pinv = pinv_all[:, sl]
            units.append(dict(
                ar=cat(-kn_ref[:, sl] * pprev_all[:, sl], r_ref[:, sl] * p),
                btt=bt_ref[:, sl] * pinv, kt=km_ref[:, sl] * pinv, v=v_ref[:, sl], pc=p[c - 1:c, :],
                h=h_sc[j * pb + pr], y_ref=y_refs[j], sl=sl))
    for un in units:
        un["bk"] = cat(un["btt"], un["kt"])
    arhs = [mm(un["ar"], un["h"]) for un in units]
    chains = []
    for un, arh in zip(units, arhs):
        for head in range(2):
            chains.append(dict(un=un, ah=arh[:c], rh=arh[c:], head=head))
    gs = [mm(jnp.where(head_masks[ch["head"]], ch["un"]["ar"], 0.0), ch["un"]["bk"], nt=True) for ch in chains]
    for ch, g in zip(chains, gs):
        ch["p_top"] = jnp.where(j_idx < t_idx, g[:c], 0.0)
        ch["mr"] = jnp.where(j_idx <= t_idx, g[c:], 0.0)
    upds = [mm(ch["p_top"], cat(ch["ah"], ch["un"]["v"])) for ch in chains]
    for ch, up in zip(chains, upds):
        ch["u"] = ch["ah"] + up
    for _ in range(int(np.log2(c2)) - 1):
        sq = [mm(ch["p_top"], cat(ch["p_top"], zero_rows)) for ch in chains]
        for ch, s in zip(chains, sq):
            ch["p_top"] = s
        upds = [mm(ch["p_top"], cat(ch["u"], ch["un"]["v"])) for ch in chains]
        for ch, up in zip(chains, upds):
            ch["u"] = ch["u"] + up
    yparts = [mm(ch["mr"], cat(ch["u"], ch["un"]["v"])) for ch in chains]
    ma_l = lane < dh
    for i, un in enumerate(units):
        c0, c1 = chains[2 * i], chains[2 * i + 1]
        un["u"] = jnp.where(ma_l, c0["u"], c1["u"])
        un["y_ref"][:, un["sl"]] = jnp.where(ma_l, c0["rh"] + yparts[2 * i], c1["rh"] + yparts[2 * i + 1])
    upds = [mm(cat(un["btt"] * un["pc"], un["kt"] * un["pc"]).T, cat(un["u"], un["v"])) for un in units]
    h_news = []
    for un, upd in zip(units, upds):
        pcol = jnp.broadcast_to(un["pc"], (SUBLANES, LANES)).T[:, 0:1]
        h_news.append(jnp.where(same_head, un["h"] * pcol + upd, 0.0))
    for i, h_new in enumerate(h_news):
        h_sc[i] = h_new

    @pl.when(ci == pl.num_programs(2) - 1)
    def _():
        for i, h_new in enumerate(h_news):
            hout_ref[i // pb, i % pb] = h_new


def _rwkv_chunk(r, lw, km, v, kn, bt, *, n, t, d_b, dh, pb=4, nbb=4, passes=1):
    c = RWKV_CHUNK
    assert t % c == 0 and 2 * dh == LANES and 2 * c == LANES
    npair = d_b // LANES
    pb = min(pb, npair)
    nbb = min(nbb, n)
    assert npair % pb == 0 and n % nbb == 0
    nchunk = t // c
    tok = pl.BlockSpec((nbb, c, pb * LANES), lambda b, pg, ci: (b, ci, pg))
    kern = functools.partial(_rwkv_chunk_kernel, c=c, dh=dh, pb=pb, nbb=nbb, passes=passes)
    seq = lambda a: a.reshape(n, t, d_b)
    y, h_pair = pl.pallas_call(
        kern,
        grid=(n // nbb, npair // pb, nchunk),
        in_specs=[tok] * 6,
        out_specs=[tok, pl.BlockSpec((nbb, pb, LANES, LANES), lambda b, pg, ci: (b, pg, 0, 0))],
        out_shape=[jax.ShapeDtypeStruct((n, t, d_b), F32),
                   jax.ShapeDtypeStruct((n, npair, LANES, LANES), F32)],
        scratch_shapes=[pltpu.VMEM((nbb * pb, LANES, LANES), F32)],
        compiler_params=_cparams(("parallel", "parallel", "arbitrary"), 64 * nbb * pb * LANES * LANES * 4),
        name="rwkv_chunk",
    )(seq(r), seq(lw), seq(km), seq(v), seq(kn), seq(bt))
    return y.reshape(n * t, d_b), h_pair


def _rwkv_step_kernel(*refs, bb, npair, dh):
    r_ref, lw_ref, km_ref, v_ref, kn_ref, bt_ref, s_ref, ones_ref = refs[:8]
    y_ref, sout_ref = refs[-2:]
    i2 = (lax.broadcasted_iota(jnp.int32, (dh, LANES), 1) & (dh - 1)) == \
        lax.broadcasted_iota(jnp.int32, (dh, LANES), 0)
    ones = ones_ref[...]
    units = [(b, pr, slice(b, b + 1), slice(pr * LANES, (pr + 1) * LANES)) for b in range(bb) for pr in range(npair)]
    ss = [jnp.concatenate([s_ref[b, 2 * pr], s_ref[b, 2 * pr + 1]], axis=1)
          for (b, pr, rb, sl) in units]
    sas = [_dot_x_exact(s * (-kn_ref[rb, sl]), ones) for s, (b, pr, rb, sl) in zip(ss, units)]
    vcols = [_dot_x_exact(jnp.where(i2, v_ref[rb, sl], 0.0), ones) for (b, pr, rb, sl) in units]
    s_news = [s * jnp.exp(lw_ref[rb, sl]) + sa * bt_ref[rb, sl] + vcol * km_ref[rb, sl]
              for s, sa, vcol, (b, pr, rb, sl) in zip(ss, sas, vcols, units)]
    ybs = [_dot_x_exact(s_new * r_ref[rb, sl], ones) for s_new, (b, pr, rb, sl) in zip(s_news, units)]
    for s_new, yb, (b, pr, rb, sl) in zip(s_news, ybs, units):
        y_ref[rb, sl] = jnp.sum(jnp.where(i2, yb, 0.0), axis=0, keepdims=True)
        sout_ref[b, 2 * pr] = s_new[:, :dh]
        sout_ref[b, 2 * pr + 1] = s_new[:, dh:]


def _rwkv_step(r, lw, km, v, kn, bt, s_all, ones_pair, s_prev, *, dh, layer, bb=8):
    bsz, d_b = r.shape
    npair = d_b // LANES
    bb = min(bb, bsz)
    base = layer * bsz // bb
    tok = pl.BlockSpec((bb, d_b), lambda i: (i, 0))
    st = pl.BlockSpec((bb, 2 * npair, dh, dh), lambda i: (base + i, 0, 0, 0))
    in_specs = [tok] * 6 + [st, pl.BlockSpec((LANES, LANES), lambda i: (0, 0))]
    args = [r, lw, km, v, kn, bt, s_all, ones_pair]
    aliases = {}
    if s_prev is not None:
        in_specs.append(pl.BlockSpec(memory_space=pl.ANY))
        args.append(s_prev)
        aliases = {8: 1}
    kern = functools.partial(_rwkv_step_kernel, bb=bb, npair=npair, dh=dh)
    return pl.pallas_call(
        kern,
        grid=(bsz // bb,),
        in_specs=in_specs,
        out_specs=[tok, st],
        out_shape=[jax.ShapeDtypeStruct((bsz, d_b), F32), jax.ShapeDtypeStruct(s_all.shape, F32)],
        input_output_aliases=aliases,
        compiler_params=_cparams(("parallel",), 16 * bb * npair * dh * LANES * 4),
        name="rwkv_step",
    )(*args)


def _rwkv_post_kernel(y_ref, r_ref, km_ref, v_ref, g_ref, lw_ref, lb_ref, rk_ref, ones_ref, o_ref, *, dh):
    ones = ones_ref[...]
    y = y_ref[...]
    inv = 1.0 / dh
    mu = _dot_x_exact(y, ones, parts=3) * inv
    d = y - mu
    var = _dot_x_exact(d * d, ones, parts=2) * inv
    yn = d * lax.rsqrt(var + LNX_EPS) * lw_ref[...] + lb_ref[...]
    bonus = _dot_x_exact(r_ref[...] * km_ref[...] * rk_ref[...], ones, parts=3) * v_ref[...]
    o_ref[...] = ((yn + bonus) * g_ref[...]).astype(o_ref.dtype)


def _rwkv_post(y, r, km, v, g, *, lnx_w, lnx_b, rk, ones_h, dh, tm=512):
    m, d_b = y.shape
    tm = min(tm, m)
    row = pl.BlockSpec((tm, d_b), lambda i: (i, 0))
    vec = pl.BlockSpec((1, d_b), lambda i: (0, 0))
    kern = functools.partial(_rwkv_post_kernel, dh=dh)
    return pl.pallas_call(
        kern,
        grid=(m // tm,),
        in_specs=[row] * 5 + [vec] * 3 + [pl.BlockSpec((d_b, d_b), lambda i: (0, 0))],
        out_specs=row,
        out_shape=jax.ShapeDtypeStruct((m, d_b), BF16),
        compiler_params=_cparams(("parallel",), 30 * tm * d_b * 4),
        name="rwkv_post",
    )(y, r, km, v, g, lnx_w, lnx_b, rk, ones_h)


def _round_up(x, m):
    return (x + m - 1) // m * m


def _block_ones(size, blk):
    idx = np.arange(size) // blk
    return jnp.asarray(idx[:, None] == idx[None, :], dtype=BF16)


def _block_diag(w):
    nb, bs, _ = w.shape
    eye = jnp.eye(nb, dtype=w.dtype)
    return (eye[:, None, :, None] * w[:, :, None, :]).reshape(nb * bs, nb * bs)


def kernel(x_prompt, x_sample, cache_fox_k, cache_fox_v, cache_fox_logf, state_rwkv_shift, state_rwkv_wkv, state_rglru_conv, state_rglru_h, cache_mem_k, cache_mem_v, page_table, mem_prompt, norm_mix, w_in, fox_bf, rw_mu, rw_w0, rw_w2, rw_a0, rw_a2, rw_g2, rw_kk, rw_ka, rw_rk, rw_lnx_w, rw_lnx_b, rg_conv_w, rg_conv_b, rg_wa, rg_ba, rg_wx, rg_bx, rg_lambda, w_out, norm_x, norm_mem, w_xq, w_xk, w_xv, w_xo, norm_ff, w_ff1, w_ff2, norm_f):
    nb_p, t_p, d_model = x_prompt.shape
    nb_s = x_sample.shape[0]
    depth, n_phys, page, h_a, dh_a = cache_fox_k.shape
    d_a = h_a * dh_a
    n_b_cols = state_rwkv_shift.shape[-1]
    _, _, h_b, dh_b, _ = state_rwkv_wkv.shape
    d_b = h_b * dh_b
    d_c = state_rglru_h.shape[-1]
    n_mem, h_x, dh_x = cache_mem_k.shape[2:]
    d_x = h_x * dh_x
    r_dec, r_icl, r_gate = rw_w2.shape[1], rw_a2.shape[1], rw_g2.shape[1]
    lr_w = r_dec + r_icl + r_gate
    zb_w = _round_up(n_b_cols + h_a, max(d_c, LANES))
    fa_blk = n_b_cols // LANES
    q_blk0 = zb_w // LANES
    xc_blk0 = (zb_w + 3 * d_a) // LANES
    assert n_b_cols % LANES == 0 and d_a % LANES == 0 and d_c % LANES == 0 and h_a <= LANES
    assert (zb_w + 3 * d_a) % d_c == 0 and n_b_cols == 3 * d_b + lr_w and (h_a & (h_a - 1)) == 0 and (h_x & (h_x - 1)) == 0

    splits = np.cumsum([d_a, d_a, d_a, h_a, n_b_cols, d_c])
    ones_h = _block_ones(d_b, dh_b)
    ones_pair = _block_ones(LANES, dh_b)
    g_mat = jnp.asarray(np.arange(page)[:, None] <= (np.arange(page * h_a)[None, :] // h_a), dtype=BF16)

    kc = cache_fox_k.reshape(depth * n_phys, page * h_a, dh_a)
    vc = cache_fox_v.reshape(depth * n_phys, page * h_a, dh_a)
    lf_t = jnp.swapaxes(cache_fox_logf, 2, 3).reshape(depth * n_phys, h_a, page)
    mk2 = cache_mem_k.reshape(depth * nb_s, n_mem * h_x, dh_x)
    mv2 = cache_mem_v.reshape(depth * nb_s, n_mem * h_x, dh_x)

    xp = x_prompt.reshape(nb_p * t_p, d_model)
    xs = x_sample.reshape(nb_s, d_model)
    memf = mem_prompt.reshape(nb_p * n_mem, d_model)
    row1 = lambda a: a.reshape(1, -1).astype(F32)
    pad_cols = lambda a, w: jnp.pad(a, ((0, 0), (0, w - a.shape[1])))

    d_ff = w_ff1.shape[-1]
    stacked_bf16 = [w.astype(BF16) for w in
                    (w_out, w_xq, jnp.concatenate([w_xk, w_xv], axis=2), w_xo, w_ff1, w_ff2)]
    wq, wk, wv, wf, wzb, wxc, wgc = jnp.split(w_in.astype(BF16), splits, axis=2)
    w_in_all = jnp.concatenate(
        [wzb, wf, jnp.zeros((depth, d_model, zb_w - n_b_cols - h_a), BF16), wq, wk, wv, wxc, wgc], axis=2)
    p_states, s_states = [], []
    kv5 = None
    wkv_s_all = None
    for l in range(depth):
        w_in_r = (w_in_all, l)
        bf_pad = pad_cols(row1(fox_bf[l]), LANES)
        mu_pad = pad_cols(row1(rw_mu[l]), zb_w)
        zrow = lambda r0, w, rows: jnp.pad(w, ((r0, lr_w - r0 - rows), (0, 0))).astype(BF16)
        w2p = zrow(0, rw_w2[l], r_dec)
        a2p = zrow(r_dec, rw_a2[l], r_icl)
        g2p = zrow(r_dec + r_icl, rw_g2[l], r_gate)
        wa_d = _block_diag(rg_wa[l]).astype(BF16)
        wx_d = _block_diag(rg_wx[l]).astype(BF16)
        w_out_b, w_xq_b, w_xkv_b, w_xo_b, w_ff1_b, w_ff2_b = [(w, l) for w in stacked_bf16]
        rwkv_par = dict(zb_w=zb_w, d_b=d_b, mu=mu_pad, w0=row1(rw_w0[l]), a0=row1(rw_a0[l]), w2p=w2p, a2p=a2p,
                        g2p=g2p, kk=row1(rw_kk[l]), ka=row1(rw_ka[l]), ones_h=ones_h)
        post_par = dict(lnx_w=row1(rw_lnx_w[l]), lnx_b=row1(rw_lnx_b[l]), rk=row1(rw_rk[l]), ones_h=ones_h, dh=dh_b)
        rg_par = dict(d_c=d_c, xc_blk0=xc_blk0, conv_w_arr=rg_conv_w[l], conv_b=rg_conv_b[l], wa_d=wa_d, wx_d=wx_d,
                      ba=rg_ba[l], bx=rg_bx[l], lam=rg_lambda[l])

        def tail(x, ya, yb, yc, attend, tm, tm_wide):
            segs = [(w_out_b, d_a, 0), (w_out_b, d_b, d_a // d_b), (w_out_b, d_c, (d_a + d_b) // d_c)]
            x = _matmul([ya, yb, yc], segs, residual=x, tm=tm_wide, tn=1024, name="mix_out_proj")
            q = _matmul([x], [(w_xq_b, d_model, 0)], gain=norm_x[l], tm=tm_wide, name="mem_q_proj")
            o = attend(q)
            x = _matmul([o], [(w_xo_b, d_x, 0)], residual=x, tm=tm_wide, tn=1024, name="mem_out_proj")
            hid = _matmul([x], [(w_ff1_b, d_model, 0)], gain=norm_ff[l], epilogue="relu2", out_dtype=BF16,
                          tm=tm_wide, tn=1024, name="ff_up")
            return _matmul([hid], [(w_ff2_b, d_ff, 0)], residual=x, tm=tm, name="ff_down")

        m_p = nb_p * t_p
        mkv = _matmul([memf], [(w_xkv_b, d_model, 0)], gain=norm_mem[l], tm=512, name="mem_kv_proj")
        z = _matmul([xp], [(w_in_r, d_model, 0)], gain=norm_mix[l], tm=1024, tn=1024, name="mix_in_proj")
        lf, c_t = _logf_cumsum(z, fa_blk, bf_pad, nb_p, t_p)
        c_row = c_t[:, :h_a, :].reshape(nb_p * h_a, t_p)
        ya = _fox_prompt(z, c_row, n=nb_p, t=t_p, n_heads=h_a, dh=dh_a, q_blk0=q_blk0)
        r, lw, km, v, kn, bt, g = _rwkv_prep(z, None, seq_len=t_p, **rwkv_par)
        y, h_pair = _rwkv_chunk(r, lw, km, v, kn, bt, n=nb_p, t=t_p, d_b=d_b, dh=dh_b)
        yb = _rwkv_post(y, r, km, v, g, **post_par)
        yc, conv1, h1 = _rglru_seq(z, n=nb_p, t=t_p, **rg_par)
        xp = tail(xp, ya, yb, yc,
                  lambda q: _mem_attn(q, mkv, n=nb_p, t=t_p, n_mem=n_mem, n_heads=h_x, dh=dh_x), 512, 1024)
        z3 = z.reshape(nb_p, t_p, -1)
        hp = h_pair.reshape(nb_p, d_b // LANES, 2, dh_b, 2, dh_b)
        wkv = jnp.stack([hp[:, :, 0, :, 0, :], hp[:, :, 1, :, 1, :]], axis=2)
        wkv = jnp.swapaxes(wkv, -1, -2).reshape(nb_p, h_b, dh_b, dh_b)
        kv5 = _kv_cache_rows(z, kv5, layer=l, depth=depth, n=nb_p, t=t_p, n_heads=h_a, dh=dh_a,
                             k_blk0=(zb_w + d_a) // LANES)
        p_states.append((
            lf.reshape(nb_p, t_p, LANES)[:, :, :h_a],
            z3[:, t_p - 1, :n_b_cols],
            wkv,
            conv1,
            h1.reshape(nb_p, d_c),
            mkv[:, :d_x].reshape(nb_p, n_mem, h_x, dh_x),
            mkv[:, d_x:].reshape(nb_p, n_mem, h_x, dh_x),
        ))

        zs = _matmul([xs], [(w_in_r, d_model, 0)], gain=norm_mix[l], tm=nb_s, name="mix_in_proj_s")
        lf_s, _ = _logf_cumsum(zs, fa_blk, bf_pad, 1, nb_s)
        hd = lambda a: a.reshape(nb_s, h_a, dh_a)
        q_s = hd(zs[:, zb_w:zb_w + d_a])
        k_s = hd(zs[:, zb_w + d_a:zb_w + 2 * d_a])
        v_s = hd(zs[:, zb_w + 2 * d_a:zb_w + 3 * d_a])
        lfn = jnp.broadcast_to(lf_s[:, :h_a, None], (nb_s, h_a, dh_a))
        ya_s = _fox_decode(page_table, q_s, k_s, v_s, lfn, kc, vc, lf_t, g_mat, layer=l, n_phys=n_phys)
        ya_s = ya_s.reshape(nb_s, d_a).astype(BF16)
        zprev = pad_cols(state_rwkv_shift[l], zb_w)
        r, lw, km, v, kn, bt, g = _rwkv_prep(zs, zprev, seq_len=1, **rwkv_par)
        y, wkv_s_all = _rwkv_step(r, lw, km, v, kn, bt, state_rwkv_wkv.reshape(depth * nb_s, h_b, dh_b, dh_b),
                                  ones_pair, wkv_s_all, dh=dh_b, layer=l)
        yb_s = _rwkv_post(y, r, km, v, g, **post_par)
        yc_s, conv1_s, h1_s = _rglru_step(zs, state_rglru_conv[l], state_rglru_h[l], **rg_par)
        xs = tail(xs, ya_s, yb_s, yc_s,
                  lambda q: _mem_decode(q.reshape(nb_s, h_x, dh_x), mk2, mv2, layer=l).reshape(nb_s, d_x),
                  nb_s, nb_s)
        s_states.append((
            k_s.reshape(nb_s, 1, h_a, dh_a),
            v_s.reshape(nb_s, 1, h_a, dh_a),
            lf_s[:, :h_a].reshape(nb_s, 1, h_a),
            zs[:, :n_b_cols],
            conv1_s,
            h1_s,
        ))

    y_prompt = _rmsnorm(xp, norm_f).reshape(nb_p, t_p, d_model)
    y_sample = _rmsnorm(xs, norm_f).reshape(nb_s, 1, d_model)
    p_out = [jnp.stack(s) for s in zip(*p_states)]
    s_out = [jnp.stack(s) for s in zip(*s_states)]
    fox_k_p, fox_v_p = [a.reshape(depth, nb_p, t_p, h_a, dh_a) for a in kv5]
    rwkv_wkv_s = wkv_s_all.reshape(depth, nb_s, h_b, dh_b, dh_b)
    return (y_prompt, y_sample, fox_k_p, fox_v_p, *p_out, *s_out[:4], rwkv_wkv_s, *s_out[4:])
```

```python
import functools

import numpy as np
import jax
import jax.numpy as jnp
from jax import lax
from jax.experimental import pallas as pl
from jax.experimental.pallas import tpu as pltpu

F32 = jnp.float32
BF16 = jnp.bfloat16

NORM_EPS = 1e-6
LNX_EPS = 64e-5
RG_C = 8.0
NEG_INF = -1e30

LANES = 128
SUBLANES = 8
VMEM_CAP_BYTES = 60000 * 1024
RWKV_CHUNK = 64


def _cparams(semantics, est_bytes):
    limit = int(min(max(2 * est_bytes + (8 << 20), 24 << 20), VMEM_CAP_BYTES))
    return pltpu.CompilerParams(dimension_semantics=semantics, vmem_limit_bytes=limit)


def _split_bf16(x, parts):
    out = []
    r = x
    for i in range(parts):
        h = r.astype(BF16)
        out.append(h)
        if i + 1 < parts:
            r = r - h.astype(F32)
    return out


def _dot_nn(a, b):
    return jnp.dot(a, b, preferred_element_type=F32)


def _dot_nt(a, b):
    return lax.dot_general(a, b, (((1,), (1,)), ((), ())), preferred_element_type=F32)


def _dot_x_exact(x, w_exact, parts=3):
    return sum(_dot_nn(p, w_exact) for p in _split_bf16(x, parts))


def _dot_exact_x(w_exact, x, parts=3):
    return sum(_dot_nn(w_exact, p) for p in _split_bf16(x, parts))


def _dot3(a, b, nt=False):
    f = _dot_nt if nt else _dot_nn
    ah, al = _split_bf16(a, 2)
    bh, bl = _split_bf16(b, 2)
    return f(ah, bh) + f(ah, bl) + f(al, bh)


def _softplus(x):
    return jnp.maximum(x, 0.0) + jnp.log1p(jnp.exp(-jnp.abs(x)))


def _log_sigmoid(x):
    return -_softplus(-x)


def _gelu_tanh(x):
    c = np.float32(np.sqrt(2.0 / np.pi))
    return 0.5 * x * (1.0 + jnp.tanh(c * (x + 0.044715 * (x * x * x))))


def _mm_kernel(*refs, n_seg, has_gain, has_res, epilogue):
    xs = refs[:n_seg]
    ws = refs[n_seg:2 * n_seg]
    pos = 2 * n_seg
    g_ref = res_ref = None
    if has_gain:
        g_ref = refs[pos]
        pos += 1
    if has_res:
        res_ref = refs[pos]
        pos += 1
    o_ref = refs[pos]
    if has_gain:
        xn_ref = refs[pos + 1]

        @pl.when(pl.program_id(1) == 0)
        def _():
            x = xs[0][...]
            ms = jnp.mean(x * x, axis=-1, keepdims=True)
            xn_ref[...] = (x * lax.rsqrt(ms + NORM_EPS) * g_ref[...]).astype(BF16)

        acc = _dot_nn(xn_ref[...], ws[0][...])
    else:
        acc = _dot_nn(xs[0][...], ws[0][...])
        for x_ref, w_ref in zip(xs[1:], ws[1:]):
            acc = acc + _dot_nn(x_ref[...], w_ref[...])
    if epilogue == "relu2":
        r = jnp.maximum(acc, 0.0)
        acc = r * r
    if has_res:
        acc = acc + res_ref[...]
    o_ref[...] = acc.astype(o_ref.dtype)


def _matmul(xs, ws, *, gain=None, residual=None, epilogue="none", out_dtype=F32, tm=512, tn=512, name="mm"):
    m = xs[0].shape[0]
    w0 = ws[0][0]
    n = (w0[0] if isinstance(w0, tuple) else w0).shape[-1]
    tm = min(tm, m)
    tn = min(tn, n)
    while n % tn:
        tn //= 2
    assert m % tm == 0 and tn % LANES == 0
    in_specs, args = [], []
    est = 0
    for x in xs:
        k = x.shape[1]
        in_specs.append(pl.BlockSpec((tm, k), lambda i, j: (i, 0)))
        args.append(x)
        est += 2 * tm * k * x.dtype.itemsize
    for (w, k, rb) in ws:
        if isinstance(w, tuple):
            w, layer = w
            in_specs.append(pl.BlockSpec((None, k, tn), lambda i, j, rb=rb, layer=layer: (layer, rb, j)))
        else:
            in_specs.append(pl.BlockSpec((k, tn), lambda i, j, rb=rb: (rb, j)))
        args.append(w)
        est += 2 * k * tn * w.dtype.itemsize
    scratch = []
    if gain is not None:
        k = xs[0].shape[1]
        in_specs.append(pl.BlockSpec((1, k), lambda i, j: (0, 0)))
        args.append(gain.reshape(1, k).astype(F32))
        scratch.append(pltpu.VMEM((tm, k), BF16))
        est += tm * k * 2
    if residual is not None:
        in_specs.append(pl.BlockSpec((tm, tn), lambda i, j: (i, j)))
        args.append(residual)
        est += 2 * tm * tn * 4
    est += 3 * tm * tn * 4
    kern = functools.partial(_mm_kernel, n_seg=len(xs), has_gain=gain is not None,
                             has_res=residual is not None, epilogue=epilogue)
    return pl.pallas_call(
        kern,
        grid=(m // tm, n // tn),
        in_specs=in_specs,
        out_specs=pl.BlockSpec((tm, tn), lambda i, j: (i, j)),
        out_shape=jax.ShapeDtypeStruct((m, n), out_dtype),
        scratch_shapes=scratch,
        compiler_params=_cparams(("parallel", "arbitrary"), est),
        name=name,
    )(*args)


def _rmsnorm_kernel(x_ref, g_ref, o_ref):
    x = x_ref[...]
    ms = jnp.mean(x * x, axis=-1, keepdims=True)
    o_ref[...] = x * lax.rsqrt(ms + NORM_EPS) * g_ref[...]


def _rmsnorm(x, g, tm=1024):
    m, d = x.shape
    tm = min(tm, m)
    return pl.pallas_call(
        _rmsnorm_kernel,
        grid=(m // tm,),
        in_specs=[pl.BlockSpec((tm, d), lambda i: (i, 0)), pl.BlockSpec((1, d), lambda i: (0, 0))],
        out_specs=pl.BlockSpec((tm, d), lambda i: (i, 0)),
        out_shape=jax.ShapeDtypeStruct((m, d), F32),
        compiler_params=_cparams(("parallel",), 4 * tm * d * 4),
        name="final_rmsnorm",
    )(x, g.reshape(1, d))


def _logf_kernel(fa_ref, bf_ref, lf_ref, ct_ref, *, t, blk):
    row = lax.broadcasted_iota(jnp.int32, (blk, blk), 0)
    col = lax.broadcasted_iota(jnp.int32, (blk, blk), 1)
    tri = (col <= row).astype(BF16)
    carry = jnp.zeros((1, LANES), F32)
    for b in range(t // blk):
        sl = pl.ds(b * blk, blk)
        lf = _log_sigmoid(fa_ref[sl, :] + bf_ref[...])
        lf_ref[sl, :] = lf
        c = _dot_exact_x(tri, lf) + carry
        ct_ref[:, sl] = c.T
        carry = c[blk - 1:blk, :]


def _logf_cumsum(z, fa_blk, bf_pad, n, t):
    blk = min(t, 256)
    kern = functools.partial(_logf_kernel, t=t, blk=blk)
    return pl.pallas_call(
        kern,
        grid=(n,),
        in_specs=[pl.BlockSpec((t, LANES), lambda i: (i, fa_blk)),
                  pl.BlockSpec((1, LANES), lambda i: (0, 0))],
        out_specs=[pl.BlockSpec((t, LANES), lambda i: (i, 0)),
                   pl.BlockSpec((None, LANES, t), lambda i: (i, 0, 0))],
        out_shape=[jax.ShapeDtypeStruct((n * t, LANES), F32),
                   jax.ShapeDtypeStruct((n, LANES, t), F32)],
        compiler_params=_cparams(("parallel",), 10 * t * LANES * 4),
        name="logf_cumsum",
    )(z, bf_pad)


def _fox_attn_kernel(q_ref, k_ref, v_ref, ck_ref, o_ref, m_sc, l_sc, acc_sc, *, scale, tq, dh, hb):
    qi = pl.program_id(2)
    m_sc[...] = jnp.full(m_sc.shape, NEG_INF, F32)
    l_sc[...] = jnp.zeros(l_sc.shape, F32)
    acc_sc[...] = jnp.zeros(acc_sc.shape, F32)
    cols = [slice(j * dh, (j + 1) * dh) for j in range(hb)]
    qs = [q_ref[:, cols[j]].astype(BF16) for j in range(hb)]

    def block(ki, diagonal):
        rows = pl.ds(pl.multiple_of(ki * tq, tq), tq)
        ss = [_dot_nt(qs[j], k_ref[rows, cols[j]].astype(BF16)) for j in range(hb)]
        ps, alphas = [], []
        for j in range(hb):
            s = ss[j] * scale - ck_ref[j, ki]
            if diagonal:
                causal = (lax.broadcasted_iota(jnp.int32, (tq, tq), 1)
                          <= lax.broadcasted_iota(jnp.int32, (tq, tq), 0))
                s = jnp.where(causal, s, NEG_INF)
            m_old = m_sc[j]
            m_new = jnp.maximum(m_old, jnp.max(s, axis=1, keepdims=True))
            alpha = jnp.exp(m_old - m_new)
            p = jnp.exp(s - m_new)
            l_sc[j] = alpha * l_sc[j] + jnp.sum(p, axis=1, keepdims=True)
            m_sc[j] = m_new
            ps.append(p.astype(BF16))
            alphas.append(alpha)
        pvs = [_dot_nn(ps[j], v_ref[rows, cols[j]].astype(BF16)) for j in range(hb)]
        for j in range(hb):
            acc_sc[j] = alphas[j] * acc_sc[j] + pvs[j]

    def body(ki, carry):
        block(ki, False)
        return carry

    lax.fori_loop(0, qi, body, 0)
    block(qi, True)
    for j in range(hb):
        o_ref[:, cols[j]] = (acc_sc[j] / l_sc[j]).astype(o_ref.dtype)


def _fox_prompt(z, c_row, *, n, t, n_heads, dh, q_blk0, tq=512, hb=8):
    tq = min(tq, t)
    nq = t // tq
    hb = min(hb, n_heads)
    assert n_heads % hb == 0 and q_blk0 * LANES % (hb * dh) == 0
    ng = n_heads // hb
    g0 = q_blk0 * LANES // (hb * dh)
    kern = functools.partial(_fox_attn_kernel, scale=float(dh) ** -0.5, tq=tq, dh=dh, hb=hb)
    return pl.pallas_call(
        kern,
        grid=(n, ng, nq),
        in_specs=[
            pl.BlockSpec((tq, hb * dh), lambda b, g, qi: (b * nq + qi, g0 + g)),
            pl.BlockSpec((t, hb * dh), lambda b, g, qi: (b, g0 + ng + g)),
            pl.BlockSpec((t, hb * dh), lambda b, g, qi: (b, g0 + 2 * ng + g)),
            pl.BlockSpec((hb, nq, 1, tq), lambda b, g, qi: (b * ng + g, 0, 0, 0)),
        ],
        out_specs=pl.BlockSpec((tq, hb * dh), lambda b, g, qi: (b * nq + qi, g)),
        out_shape=jax.ShapeDtypeStruct((n * t, n_heads * dh), BF16),
        scratch_shapes=[pltpu.VMEM((hb, tq, 1), F32), pltpu.VMEM((hb, tq, 1), F32),
                        pltpu.VMEM((hb, tq, dh), F32)],
        compiler_params=_cparams(("parallel", "parallel", "arbitrary"),
                                 hb * (4 * t * dh * 4 + 8 * tq * dh * 4 + 6 * tq * tq * 4)),
        name="fox_prompt_attn",
    )(z, z, z, c_row.reshape(n * n_heads, nq, 1, tq))


def _kv_cache_rows_kernel(*refs, n_heads, dh, has_prev):
    k_ref, v_ref = refs[:2]
    k5_ref, v5_ref = refs[-2:]
    tb = k_ref.shape[0]
    for h in range(n_heads):
        dst = pl.ds(h, tb, stride=n_heads)
        k5_ref[dst, :] = k_ref[:, h * dh:(h + 1) * dh]
        v5_ref[dst, :] = v_ref[:, h * dh:(h + 1) * dh]


def _kv_cache_rows(z, prev, *, layer, depth, n, t, n_heads, dh, k_blk0, tb=512):
    tb = min(tb, t)
    nt = t // tb
    d_a = n_heads * dh
    kb = k_blk0 * LANES // d_a
    assert k_blk0 * LANES % d_a == 0 and t % tb == 0
    shape = jax.ShapeDtypeStruct((depth, n, t * n_heads, dh), F32)
    out_spec = pl.BlockSpec((None, None, tb * n_heads, dh), lambda b, i: (layer, b, i, 0))
    in_specs = [pl.BlockSpec((tb, d_a), lambda b, i: (b * nt + i, kb)),
                pl.BlockSpec((tb, d_a), lambda b, i: (b * nt + i, kb + 1))]
    args = [z, z]
    aliases = {}
    if prev is not None:
        in_specs += [pl.BlockSpec(memory_space=pl.ANY)] * 2
        args += list(prev)
        aliases = {2: 0, 3: 1}
    kern = functools.partial(_kv_cache_rows_kernel, n_heads=n_heads, dh=dh, has_prev=prev is not None)
    return pl.pallas_call(
        kern,
        grid=(n, nt),
        in_specs=in_specs,
        out_specs=[out_spec, out_spec],
        out_shape=[shape, shape],
        input_output_aliases=aliases,
        compiler_params=_cparams(("parallel", "parallel"), 8 * tb * d_a * 4),
        name="kv_cache_rows",
    )(*args)


def _head_valid(n_heads, rows):
    lane = lax.broadcasted_iota(jnp.int32, (n_heads, rows), 1)
    sub = lax.broadcasted_iota(jnp.int32, (n_heads, rows), 0)
    return (lane & (n_heads - 1)) == sub


def _fox_decode_kernel(pt_ref, q_ref, kn_ref, vn_ref, lfn_ref, *rest, n_heads, scale, g):
    kcs, vcs, lfs = rest[0:g], rest[g:2 * g], rest[2 * g:3 * g]
    g_ref, o_ref, m_sc, l_sc, acc_sc, car_sc = rest[3 * g:]
    p = pl.program_id(1)

    @pl.when(p == 0)
    def _():
        m_sc[...] = jnp.full(m_sc.shape, NEG_INF, F32)
        l_sc[...] = jnp.zeros(l_sc.shape, F32)
        acc_sc[...] = jnp.zeros(acc_sc.shape, F32)
        car_sc[...] = jnp.zeros(car_sc.shape, F32)

    rows = kcs[0].shape[0]
    qb = q_ref[...].astype(BF16)
    valid = _head_valid(n_heads, rows)
    f_all = jnp.concatenate([lf[...] for lf in lfs], axis=0)
    cum_all = _dot_x_exact(f_all, g_ref[...])
    car = car_sc[...]
    ss = []
    for j in range(g):
        s = _dot_nt(qb, kcs[j][...].astype(BF16)) * scale
        ss.append(jnp.where(valid, s - (car + cum_all[j * n_heads:(j + 1) * n_heads]), NEG_INF))
        car = car + jnp.sum(lfs[j][...], axis=1, keepdims=True)
    car_new = car
    m_old = m_sc[...]
    m_new = m_old
    for s in ss:
        m_new = jnp.maximum(m_new, jnp.max(s, axis=1, keepdims=True))
    alpha = jnp.exp(m_old - m_new)
    l_new = alpha * l_sc[...]
    acc_new = alpha * acc_sc[...]
    for j in range(g):
        pr = jnp.exp(ss[j] - m_new)
        l_new = l_new + jnp.sum(pr, axis=1, keepdims=True)
        acc_new = acc_new + _dot_nn(pr.astype(BF16), vcs[j][...].astype(BF16))
    m_sc[...] = m_new
    l_sc[...] = l_new
    acc_sc[...] = acc_new
    car_sc[...] = car_new

    @pl.when(p == pl.num_programs(1) - 1)
    def _():
        s_new = jnp.sum(q_ref[...] * kn_ref[...], axis=1, keepdims=True) * scale - (car_new + lfn_ref[:, 0:1])
        m2 = jnp.maximum(m_new, s_new)
        a2 = jnp.exp(m_new - m2)
        pn = jnp.exp(s_new - m2)
        o_ref[...] = (a2 * acc_new + pn * vn_ref[...]) / (a2 * l_new + pn)


def _fox_decode(page_table, q, k_new, v_new, lf_new, kc, vc, lf_t, g_mat, *, layer, n_phys, g=16):
    b, n_heads, dh = q.shape
    n_pages = page_table.shape[1]
    g = min(g, n_pages)
    assert n_pages % g == 0
    rows = kc.shape[1]
    page = lf_t.shape[2]
    base = layer * n_phys
    tok = pl.BlockSpec((None, n_heads, dh), lambda i, p, pt: (i, 0, 0))
    pg = lambda j: (lambda i, p, pt: (base + pt[i, p * g + j], 0, 0))
    kern = functools.partial(_fox_decode_kernel, n_heads=n_heads, scale=float(dh) ** -0.5, g=g)
    return pl.pallas_call(
        kern,
        grid_spec=pltpu.PrefetchScalarGridSpec(
            num_scalar_prefetch=1,
            grid=(b, n_pages // g),
            in_specs=([tok] * 4
                      + [pl.BlockSpec((None, rows, dh), pg(j)) for j in range(g)]
                      + [pl.BlockSpec((None, rows, dh), pg(j)) for j in range(g)]
                      + [pl.BlockSpec((None, n_heads, page), pg(j)) for j in range(g)]
                      + [pl.BlockSpec((page, rows), lambda i, p, pt: (0, 0))]),
            out_specs=tok,
            scratch_shapes=[pltpu.VMEM((n_heads, 1), F32), pltpu.VMEM((n_heads, 1), F32),
                            pltpu.VMEM((n_heads, dh), F32), pltpu.VMEM((n_heads, 1), F32)],
        ),
        out_shape=jax.ShapeDtypeStruct((b, n_heads, dh), F32),
        compiler_params=_cparams(("parallel", "arbitrary"), g * 4 * rows * dh * 4 + 2 * page * rows * 2),
        name="fox_decode_attn",
    )(page_table, q, k_new, v_new, lf_new, *([kc] * g), *([vc] * g), *([lf_t] * g), g_mat)


def _mem_decode_kernel(q_ref, k_ref, v_ref, o_ref, *, n_heads, scale, bb):
    rows = k_ref.shape[1]
    valid = _head_valid(n_heads, rows)
    ss = [_dot_nt(q_ref[j].astype(BF16), k_ref[j].astype(BF16)) for j in range(bb)]
    ps, ls = [], []
    for s in ss:
        s = jnp.where(valid, s * scale, NEG_INF)
        p = jnp.exp(s - jnp.max(s, axis=1, keepdims=True))
        ls.append(jnp.sum(p, axis=1, keepdims=True))
        ps.append(p.astype(BF16))
    os_ = [_dot_nn(ps[j], v_ref[j].astype(BF16)) for j in range(bb)]
    for j in range(bb):
        o_ref[j] = (os_[j] / ls[j]).astype(o_ref.dtype)


def _mem_decode(q, k2, v2, *, layer, bb=4):
    b, n_heads, dh = q.shape
    rows = k2.shape[1]
    bb = min(bb, b)
    assert b % bb == 0
    base = layer * b // bb
    kern = functools.partial(_mem_decode_kernel, n_heads=n_heads, scale=float(dh) ** -0.5, bb=bb)
    return pl.pallas_call(
        kern,
        grid=(b // bb,),
        in_specs=[pl.BlockSpec((bb, n_heads, dh), lambda i: (i, 0, 0)),
                  pl.BlockSpec((bb, rows, dh), lambda i: (base + i, 0, 0)),
                  pl.BlockSpec((bb, rows, dh), lambda i: (base + i, 0, 0))],
        out_specs=pl.BlockSpec((bb, n_heads, dh), lambda i: (i, 0, 0)),
        out_shape=jax.ShapeDtypeStruct((b, n_heads, dh), BF16),
        compiler_params=_cparams(("parallel",), 4 * bb * rows * dh * 4),
        name="mem_decode_attn",
    )(q, k2, v2)


def _mem_attn_kernel(q_ref, k_ref, v_ref, o_ref, *, scale):
    s = _dot_nt(q_ref[...].astype(BF16), k_ref[...].astype(BF16)) * scale
    m = jnp.max(s, axis=1, keepdims=True)
    p = jnp.exp(s - m)
    l = jnp.sum(p, axis=1, keepdims=True)
    o_ref[...] = (_dot_nn(p.astype(BF16), v_ref[...].astype(BF16)) / l).astype(o_ref.dtype)


def _mem_attn(q, kv, *, n, t, n_mem, n_heads, dh, tq=1024):
    tq = min(tq, t)
    nq = t // tq
    kern = functools.partial(_mem_attn_kernel, scale=float(dh) ** -0.5)
    return pl.pallas_call(
        kern,
        grid=(n, n_heads, nq),
        in_specs=[pl.BlockSpec((tq, dh), lambda b, h, qi: (b * nq + qi, h)),
                  pl.BlockSpec((n_mem, dh), lambda b, h, qi: (b, h)),
                  pl.BlockSpec((n_mem, dh), lambda b, h, qi: (b, n_heads + h))],
        out_specs=pl.BlockSpec((tq, dh), lambda b, h, qi: (b * nq + qi, h)),
        out_shape=jax.ShapeDtypeStruct((n * t, n_heads * dh), BF16),
        compiler_params=_cparams(("parallel", "parallel", "parallel"), 8 * tq * dh * 4 + 4 * tq * n_mem * 4),
        name="mem_attn",
    )(q, kv, kv)


def _rglru_gates(u, wa_ref, wx_ref, ba_ref, bx_ref, lam_ref):
    ub = u.astype(BF16)
    gate_a = jax.nn.sigmoid(_dot_nn(ub, wa_ref[...]) + ba_ref[...])
    gate_x = jax.nn.sigmoid(_dot_nn(ub, wx_ref[...]) + bx_ref[...])
    log_a = -RG_C * gate_a * _softplus(-lam_ref[...])
    a = jnp.exp(log_a)
    b = u * gate_x * jnp.sqrt(1.0 - jnp.exp(2.0 * log_a))
    return a, b


def _shift_rows(x, d, fill):
    t = x.shape[0]
    if d % SUBLANES == 0:
        return jnp.concatenate([jnp.full((d, x.shape[1]), fill, x.dtype), x[:t - d]], axis=0)
    rolled = pltpu.roll(x, d, 0)
    row = lax.broadcasted_iota(jnp.int32, x.shape, 0)
    return jnp.where(row < d, fill, rolled)


def _rglru_seq_kernel(xc_ref, gc_ref, cw_ref, cb_ref, wa_ref, wx_ref, ba_ref, bx_ref, lam_ref,
                      y_ref, conv_ref, h_ref, xpad_sc, *, t, conv_w):
    xpad_sc[0:SUBLANES, :] = jnp.zeros((SUBLANES, LANES), F32)
    xpad_sc[SUBLANES:, :] = xc_ref[...]
    u = cb_ref[...] + jnp.zeros((t, LANES), F32)
    for j in range(conv_w):
        u = u + xpad_sc[pl.ds(SUBLANES - (conv_w - 1) + j, t), :] * cw_ref[j:j + 1, :]
    a, b = _rglru_gates(u, wa_ref, wx_ref, ba_ref, bx_ref, lam_ref)
    d = 1
    while d < t:
        a_sh = _shift_rows(a, d, 1.0)
        b_sh = _shift_rows(b, d, 0.0)
        b = a * b_sh + b
        a = a * a_sh
        d *= 2
    y_ref[...] = (b * _gelu_tanh(gc_ref[...])).astype(y_ref.dtype)
    h_ref[...] = b[t - 1:t, :]
    conv_ref[...] = xpad_sc[pl.ds(SUBLANES + t - (conv_w - 1), conv_w - 1), :]


def _rglru_seq(z, *, n, t, d_c, xc_blk0, conv_w_arr, conv_b, wa_d, wx_d, ba, bx, lam):
    nc = d_c // LANES
    conv_w = conv_w_arr.shape[0]
    vec = lambda a: a.reshape(1, d_c)
    vspec = pl.BlockSpec((1, LANES), lambda b, c: (0, c))
    kern = functools.partial(_rglru_seq_kernel, t=t, conv_w=conv_w)
    return pl.pallas_call(
        kern,
        grid=(n, nc),
        in_specs=[pl.BlockSpec((t, LANES), lambda b, c: (b, xc_blk0 + c)),
                  pl.BlockSpec((t, LANES), lambda b, c: (b, xc_blk0 + nc + c)),
                  pl.BlockSpec((conv_w, LANES), lambda b, c: (0, c)),
                  vspec,
                  pl.BlockSpec((LANES, LANES), lambda b, c: (c, c)),
                  pl.BlockSpec((LANES, LANES), lambda b, c: (c, c)),
                  vspec, vspec, vspec],
        out_specs=[pl.BlockSpec((t, LANES), lambda b, c: (b, c)),
                   pl.BlockSpec((None, conv_w - 1, LANES), lambda b, c: (b, 0, c)),
                   pl.BlockSpec((None, 1, LANES), lambda b, c: (b, 0, c))],
        out_shape=[jax.ShapeDtypeStruct((n * t, d_c), BF16),
                   jax.ShapeDtypeStruct((n, conv_w - 1, d_c), F32),
                   jax.ShapeDtypeStruct((n, 1, d_c), F32)],
        scratch_shapes=[pltpu.VMEM((t + SUBLANES, LANES), F32)],
        compiler_params=_cparams(("parallel", "parallel"), 16 * t * LANES * 4),
        name="rglru_seq",
    )(z, z, conv_w_arr, vec(conv_b), wa_d, wx_d, vec(ba), vec(bx), vec(lam))


def _rglru_step_kernel(xc_ref, gc_ref, c0_ref, h0_ref, cw_ref, cb_ref, wa_ref, wx_ref, ba_ref, bx_ref, lam_ref,
                       y_ref, conv_ref, h_ref, *, conv_w):
    xc = xc_ref[...]
    u = cb_ref[...] + xc * cw_ref[conv_w - 1:conv_w, :]
    for j in range(conv_w - 1):
        u = u + c0_ref[:, j, :] * cw_ref[j:j + 1, :]
    a, b = _rglru_gates(u, wa_ref, wx_ref, ba_ref, bx_ref, lam_ref)
    h = a * h0_ref[...] + b
    y_ref[...] = (h * _gelu_tanh(gc_ref[...])).astype(y_ref.dtype)
    h_ref[...] = h
    for j in range(conv_w - 2):
        conv_ref[:, j, :] = c0_ref[:, j + 1, :]
    conv_ref[:, conv_w - 2, :] = xc


def _rglru_step(z, conv0, h0, *, d_c, xc_blk0, conv_w_arr, conv_b, wa_d, wx_d, ba, bx, lam):
    bsz = z.shape[0]
    conv_w = conv_w_arr.shape[0]
    ncb = d_c // LANES
    vec = lambda a: a.reshape(1, d_c)
    full = lambda shape: pl.BlockSpec(shape, lambda i: (0,) * len(shape))
    kern = functools.partial(_rglru_step_kernel, conv_w=conv_w)
    return pl.pallas_call(
        kern,
        grid=(1,),
        in_specs=[pl.BlockSpec((bsz, d_c), lambda i: (0, xc_blk0 * LANES // d_c)),
                  pl.BlockSpec((bsz, d_c), lambda i: (0, xc_blk0 * LANES // d_c + 1)),
                  full((bsz, conv_w - 1, d_c)), full((bsz, d_c)), full((conv_w, d_c)), full((1, d_c)),
                  full((d_c, d_c)), full((d_c, d_c)), full((1, d_c)), full((1, d_c)), full((1, d_c))],
        out_specs=[full((bsz, d_c)), full((bsz, conv_w - 1, d_c)), full((bsz, d_c))],
        out_shape=[jax.ShapeDtypeStruct((bsz, d_c), BF16),
                   jax.ShapeDtypeStruct((bsz, conv_w - 1, d_c), F32),
                   jax.ShapeDtypeStruct((bsz, d_c), F32)],
        compiler_params=_cparams(("arbitrary",), 16 * bsz * d_c * 4 + 4 * d_c * d_c * 2),
        name="rglru_step",
    )(z, z, conv0, h0, conv_w_arr, vec(conv_b), wa_d, wx_d, vec(ba), vec(bx), vec(lam))


def _rwkv_prep_kernel(*refs, d_b, seq_mode, blocks_per_seq):
    if seq_mode:
        zb_ref, prev_ref = refs[:2]
    else:
        zb_ref, zp_ref = refs[:2]
    (mu_ref, w0_ref, a0_ref, w2_ref, a2_ref, g2_ref, kk_ref, ka_ref, ones_ref,
     r_ref, lw_ref, km_ref, v_ref, kn_ref, bt_ref, g_ref) = refs[2:18]
    zb = zb_ref[...]
    tm = zb.shape[0]
    if seq_mode:
        sh_sc = refs[18]
        first = (pl.program_id(0) % blocks_per_seq) == 0
        prev = jnp.where(first, 0.0, prev_ref[...])
        sh_sc[0:SUBLANES, :] = prev
        sh_sc[SUBLANES:, :] = zb
        zp = sh_sc[pl.ds(SUBLANES - 1, tm), :]
    else:
        zp = zp_ref[...]
    zs = zb + mu_ref[...] * (zp - zb)
    r = zs[:, 0:d_b]
    k = zs[:, d_b:2 * d_b]
    v = zs[:, 2 * d_b:3 * d_b]
    lr = zs[:, 3 * d_b:3 * d_b + w2_ref.shape[0]]
    w_lin = _dot_nn(jnp.tanh(lr).astype(BF16), w2_ref[...])
    a_lin = _dot_nn(lr.astype(BF16), a2_ref[...])
    g = _dot_nn(jax.nn.sigmoid(lr).astype(BF16), g2_ref[...])
    w = -_softplus(-(w0_ref[...] + w_lin)) - 0.5
    a = jax.nn.sigmoid(a0_ref[...] + a_lin)
    kk = k * kk_ref[...]
    nrm2 = _dot_x_exact(kk * kk, ones_ref[...], parts=2)
    kn = kk / jnp.maximum(jnp.sqrt(nrm2), 1e-12)
    r_ref[...] = r
    lw_ref[...] = -jnp.exp(w)
    km_ref[...] = k * (1.0 + (a - 1.0) * ka_ref[...])
    v_ref[...] = v
    kn_ref[...] = kn
    bt_ref[...] = kn * a
    g_ref[...] = g


def _rwkv_prep(z, zprev, *, zb_w, d_b, mu, w0, a0, w2p, a2p, g2p, kk, ka, ones_h, seq_len, tm=512):
    m = z.shape[0]
    seq_mode = zprev is None
    tm = min(tm, seq_len if seq_mode else m)
    assert m % tm == 0
    row = lambda i: (i, 0)
    cst = lambda i: (0, 0)
    if seq_mode:
        per8 = tm // SUBLANES
        second = pl.BlockSpec((SUBLANES, zb_w), lambda i: (jnp.maximum(i * per8 - 1, 0), 0))
        second_arg = z
        scratch = [pltpu.VMEM((tm + SUBLANES, zb_w), F32)]
        bps = seq_len // tm
    else:
        second = pl.BlockSpec((tm, zb_w), row)
        second_arg = zprev
        scratch = []
        bps = 1
    lrw = w2p.shape[0]
    kern = functools.partial(_rwkv_prep_kernel, d_b=d_b, seq_mode=seq_mode, blocks_per_seq=bps)
    out = jax.ShapeDtypeStruct((m, d_b), F32)
    return pl.pallas_call(
        kern,
        grid=(m // tm,),
        in_specs=[pl.BlockSpec((tm, zb_w), row), second,
                  pl.BlockSpec((1, zb_w), cst), pl.BlockSpec((1, d_b), cst), pl.BlockSpec((1, d_b), cst),
                  pl.BlockSpec((lrw, d_b), cst), pl.BlockSpec((lrw, d_b), cst), pl.BlockSpec((lrw, d_b), cst),
                  pl.BlockSpec((1, d_b), cst), pl.BlockSpec((1, d_b), cst), pl.BlockSpec((d_b, d_b), cst)],
        out_specs=[pl.BlockSpec((tm, d_b), row)] * 7,
        out_shape=[out] * 7,
        scratch_shapes=scratch,
        compiler_params=_cparams(("parallel",), 6 * tm * zb_w * 4 + 30 * tm * d_b * 4),
        name="rwkv_prep",
    )(z, second_arg, mu, w0, a0, w2p, a2p, g2p, kk, ka, ones_h)


def _mm_p(a, b, passes, nt=False):
    if passes == 1:
        f = _dot_nt if nt else _dot_nn
        return f(a.astype(BF16), b.astype(BF16))
    return _dot3(a, b, nt=nt)


def _rwkv_chunk_kernel(*refs, c, dh, pb, nbb, passes):
    ins = [[ref.at[j] for ref in refs[:6]] for j in range(nbb)]
    y_refs = [refs[6].at[j] for j in range(nbb)]
    hout_ref, h_sc = refs[7:]
    ci = pl.program_id(2)

    @pl.when(ci == 0)
    def _():
        h_sc[...] = jnp.zeros(h_sc.shape, F32)

    c2 = 2 * c
    lane = lax.broadcasted_iota(jnp.int32, (c, LANES), 1)
    lane2 = lax.broadcasted_iota(jnp.int32, (c2, LANES), 1)
    t_idx = lax.broadcasted_iota(jnp.int32, (c, c2), 0)
    j_idx = lax.broadcasted_iota(jnp.int32, (c, c2), 1) & (c - 1)
    r128 = lax.broadcasted_iota(jnp.int32, (LANES, LANES), 0)
    c128 = lax.broadcasted_iota(jnp.int32, (LANES, LANES), 1)
    same_head = (r128 < dh) == (c128 < dh)
    zero_rows = jnp.zeros((c, c2), F32)
    head_masks = [lane2 < dh, lane2 >= dh]
    cat = lambda a, b: jnp.concatenate([a, b], axis=0)
    mm = functools.partial(_mm_p, passes=passes)

    units = []
    for j in range(nbb):
        r_ref, lw_ref, km_ref, v_ref, kn_ref, bt_ref = ins[j]
        lw_all = lw_ref[...]
        lc_all = lw_all
        d = 1
        while d < c:
            lc_all = lc_all + _shift_rows(lc_all, d, 0.0)
            d *= 2
        p_all = jnp.exp(lc_all)
        pinv_all = jnp.exp(-lc_all)
        pprev_all = jnp.exp(lc_all - lw_all)
        for pr in range(pb):
            sl = slice(pr * LANES, (pr + 1) * LANES)
            p = p_all[:, sl]
            pinv = pinv_all[:, sl]
            units.append(dict(
                ar=cat(-kn_ref[:, sl] * pprev_all[:, sl], r_ref[:, sl] * p),
                btt=bt_ref[:, sl] * pinv, kt=km_ref[:, sl] * pinv, v=v_ref[:, sl], pc=p[c - 1:c, :],
                h=h_sc[j * pb + pr], y_ref=y_refs[j], sl=sl))
    for un in units:
        un["bk"] = cat(un["btt"], un["kt"])
    arhs = [mm(un["ar"], un["h"]) for un in units]
    chains = []
    for un, arh in zip(units, arhs):
        for head in range(2):
            chains.append(dict(un=un, ah=arh[:c], rh=arh[c:], head=head))
    gs = [mm(jnp.where(head_masks[ch["head"]], ch["un"]["ar"], 0.0), ch["un"]["bk"], nt=True) for ch in chains]
    for ch, g in zip(chains, gs):
        ch["p_top"] = jnp.where(j_idx < t_idx, g[:c], 0.0)
        ch["mr"] = jnp.where(j_idx <= t_idx, g[c:], 0.0)
    upds = [mm(ch["p_top"], cat(ch["ah"], ch["un"]["v"])) for ch in chains]
    for ch, up in zip(chains, upds):
        ch["u"] = ch["ah"] + up
    for _ in range(int(np.log2(c2)) - 1):
        sq = [mm(ch["p_top"], cat(ch["p_top"], zero_rows)) for ch in chains]
        for ch, s in zip(chains, sq):
            ch["p_top"] = s
        upds = [mm(ch["p_top"], cat(ch["u"], ch["un"]["v"])) for ch in chains]
        for ch, up in zip(chains, upds):
            ch["u"] = ch["u"] + up
    yparts = [mm(ch["mr"], cat(ch["u"], ch["un"]["v"])) for ch in chains]
    ma_l = lane < dh
    for i, un in enumerate(units):
        c0, c1 = chains[2 * i], chains[2 * i + 1]
        un["u"] = jnp.where(ma_l, c0["u"], c1["u"])
        un["y_ref"][:, un["sl"]] = jnp.where(ma_l, c0["rh"] + yparts[2 * i], c1["rh"] + yparts[2 * i + 1])
    upds = [mm(cat(un["btt"] * un["pc"], un["kt"] * un["pc"]).T, cat(un["u"], un["v"])) for un in units]
    h_news = []
    for un, upd in zip(units, upds):
        pcol = jnp.broadcast_to(un["pc"], (SUBLANES, LANES)).T[:, 0:1]
        h_news.append(jnp.where(same_head, un["h"] * pcol + upd, 0.0))
    for i, h_new in enumerate(h_news):
        h_sc[i] = h_new

    @pl.when(ci == pl.num_programs(2) - 1)
    def _():
        for i, h_new in enumerate(h_news):
            hout_ref[i // pb, i % pb] = h_new


def _rwkv_chunk(r, lw, km, v, kn, bt, *, n, t, d_b, dh, pb=4, nbb=4, passes=1):
    c = RWKV_CHUNK
    assert t % c == 0 and 2 * dh == LANES and 2 * c == LANES
    npair = d_b // LANES
    pb = min(pb, npair)
    nbb = min(nbb, n)
    assert npair % pb == 0 and n % nbb == 0
    nchunk = t // c
    tok = pl.BlockSpec((nbb, c, pb * LANES), lambda b, pg, ci: (b, ci, pg))
    kern = functools.partial(_rwkv_chunk_kernel, c=c, dh=dh, pb=pb, nbb=nbb, passes=passes)
    seq = lambda a: a.reshape(n, t, d_b)
    y, h_pair = pl.pallas_call(
        kern,
        grid=(n // nbb, npair // pb, nchunk),
        in_specs=[tok] * 6,
        out_specs=[tok, pl.BlockSpec((nbb, pb, LANES, LANES), lambda b, pg, ci: (b, pg, 0, 0))],
        out_shape=[jax.ShapeDtypeStruct((n, t, d_b), F32),
                   jax.ShapeDtypeStruct((n, npair, LANES, LANES), F32)],
        scratch_shapes=[pltpu.VMEM((nbb * pb, LANES, LANES), F32)],
        compiler_params=_cparams(("parallel", "parallel", "arbitrary"), 64 * nbb * pb * LANES * LANES * 4),
        name="rwkv_chunk",
    )(seq(r), seq(lw), seq(km), seq(v), seq(kn), seq(bt))
    return y.reshape(n * t, d_b), h_pair


def _rwkv_step_kernel(*refs, bb, npair, dh):
    r_ref, lw_ref, km_ref, v_ref, kn_ref, bt_ref, s_ref, ones_ref = refs[:8]
    y_ref, sout_ref = refs[-2:]
    i2 = (lax.broadcasted_iota(jnp.int32, (dh, LANES), 1) & (dh - 1)) == \
        lax.broadcasted_iota(jnp.int32, (dh, LANES), 0)
    ones = ones_ref[...]
    units = [(b, pr, slice(b, b + 1), slice(pr * LANES, (pr + 1) * LANES)) for b in range(bb) for pr in range(npair)]
    ss = [jnp.concatenate([s_ref[b, 2 * pr], s_ref[b, 2 * pr + 1]], axis=1)
          for (b, pr, rb, sl) in units]
    sas = [_dot_x_exact(s * (-kn_ref[rb, sl]), ones) for s, (b, pr, rb, sl) in zip(ss, units)]
    vcols = [_dot_x_exact(jnp.where(i2, v_ref[rb, sl], 0.0), ones) for (b, pr, rb, sl) in units]
    s_news = [s * jnp.exp(lw_ref[rb, sl]) + sa * bt_ref[rb, sl] + vcol * km_ref[rb, sl]
              for s, sa, vcol, (b, pr, rb, sl) in zip(ss, sas, vcols, units)]
    ybs = [_dot_x_exact(s_new * r_ref[rb, sl], ones) for s_new, (b, pr, rb, sl) in zip(s_news, units)]
    for s_new, yb, (b, pr, rb, sl) in zip(s_news, ybs, units):
        y_ref[rb, sl] = jnp.sum(jnp.where(i2, yb, 0.0), axis=0, keepdims=True)
        sout_ref[b, 2 * pr] = s_new[:, :dh]
        sout_ref[b, 2 * pr + 1] = s_new[:, dh:]


def _rwkv_step(r, lw, km, v, kn, bt, s_all, ones_pair, s_prev, *, dh, layer, bb=8):
    bsz, d_b = r.shape
    npair = d_b // LANES
    bb = min(bb, bsz)
    base = layer * bsz // bb
    tok = pl.BlockSpec((bb, d_b), lambda i: (i, 0))
    st = pl.BlockSpec((bb, 2 * npair, dh, dh), lambda i: (base + i, 0, 0, 0))
    in_specs = [tok] * 6 + [st, pl.BlockSpec((LANES, LANES), lambda i: (0, 0))]
    args = [r, lw, km, v, kn, bt, s_all, ones_pair]
    aliases = {}
    if s_prev is not None:
        in_specs.append(pl.BlockSpec(memory_space=pl.ANY))
        args.append(s_prev)
        aliases = {8: 1}
    kern = functools.partial(_rwkv_step_kernel, bb=bb, npair=npair, dh=dh)
    return pl.pallas_call(
        kern,
        grid=(bsz // bb,),
        in_specs=in_specs,
        out_specs=[tok, st],
        out_shape=[jax.ShapeDtypeStruct((bsz, d_b), F32), jax.ShapeDtypeStruct(s_all.shape, F32)],
        input_output_aliases=aliases,
        compiler_params=_cparams(("parallel",), 16 * bb * npair * dh * LANES * 4),
        name="rwkv_step",
    )(*args)


def _rwkv_post_kernel(y_ref, r_ref, km_ref, v_ref, g_ref, lw_ref, lb_ref, rk_ref, ones_ref, o_ref, *, dh):
    ones = ones_ref[...]
    y = y_ref[...]
    inv = 1.0 / dh
    mu = _dot_x_exact(y, ones, parts=3) * inv
    d = y - mu
    var = _dot_x_exact(d * d, ones, parts=2) * inv
    yn = d * lax.rsqrt(var + LNX_EPS) * lw_ref[...] + lb_ref[...]
    bonus = _dot_x_exact(r_ref[...] * km_ref[...] * rk_ref[...], ones, parts=3) * v_ref[...]
    o_ref[...] = ((yn + bonus) * g_ref[...]).astype(o_ref.dtype)


def _rwkv_post(y, r, km, v, g, *, lnx_w, lnx_b, rk, ones_h, dh, tm=512):
    m, d_b = y.shape
    tm = min(tm, m)
    row = pl.BlockSpec((tm, d_b), lambda i: (i, 0))
    vec = pl.BlockSpec((1, d_b), lambda i: (0, 0))
    kern = functools.partial(_rwkv_post_kernel, dh=dh)
    return pl.pallas_call(
        kern,
        grid=(m // tm,),
        in_specs=[row] * 5 + [vec] * 3 + [pl.BlockSpec((d_b, d_b), lambda i: (0, 0))],
        out_specs=row,
        out_shape=jax.ShapeDtypeStruct((m, d_b), BF16),
        compiler_params=_cparams(("parallel",), 30 * tm * d_b * 4),
        name="rwkv_post",
    )(y, r, km, v, g, lnx_w, lnx_b, rk, ones_h)


def _round_up(x, m):
    return (x + m - 1) // m * m


def _block_ones(size, blk):
    idx = np.arange(size) // blk
    return jnp.asarray(idx[:, None] == idx[None, :], dtype=BF16)


def _block_diag(w):
    nb, bs, _ = w.shape
    eye = jnp.eye(nb, dtype=w.dtype)
    return (eye[:, None, :, None] * w[:, :, None, :]).reshape(nb * bs, nb * bs)


def kernel(x_prompt, x_sample, cache_fox_k, cache_fox_v, cache_fox_logf, state_rwkv_shift, state_rwkv_wkv, state_rglru_conv, state_rglru_h, cache_mem_k, cache_mem_v, page_table, mem_prompt, norm_mix, w_in, fox_bf, rw_mu, rw_w0, rw_w2, rw_a0, rw_a2, rw_g2, rw_kk, rw_ka, rw_rk, rw_lnx_w, rw_lnx_b, rg_conv_w, rg_conv_b, rg_wa, rg_ba, rg_wx, rg_bx, rg_lambda, w_out, norm_x, norm_mem, w_xq, w_xk, w_xv, w_xo, norm_ff, w_ff1, w_ff2, norm_f):
    nb_p, t_p, d_model = x_prompt.shape
    nb_s = x_sample.shape[0]
    depth, n_phys, page, h_a, dh_a = cache_fox_k.shape
    d_a = h_a * dh_a
    n_b_cols = state_rwkv_shift.shape[-1]
    _, _, h_b, dh_b, _ = state_rwkv_wkv.shape
    d_b = h_b * dh_b
    d_c = state_rglru_h.shape[-1]
    n_mem, h_x, dh_x = cache_mem_k.shape[2:]
    d_x = h_x * dh_x
    r_dec, r_icl, r_gate = rw_w2.shape[1], rw_a2.shape[1], rw_g2.shape[1]
    lr_w = r_dec + r_icl + r_gate
    zb_w = _round_up(n_b_cols + h_a, max(d_c, LANES))
    fa_blk = n_b_cols // LANES
    q_blk0 = zb_w // LANES
    xc_blk0 = (zb_w + 3 * d_a) // LANES
    assert n_b_cols % LANES == 0 and d_a % LANES == 0 and d_c % LANES == 0 and h_a <= LANES
    assert (zb_w + 3 * d_a) % d_c == 0 and n_b_cols == 3 * d_b + lr_w and (h_a & (h_a - 1)) == 0 and (h_x & (h_x - 1)) == 0

    splits = np.cumsum([d_a, d_a, d_a, h_a, n_b_cols, d_c])
    ones_h = _block_ones(d_b, dh_b)
    ones_pair = _block_ones(LANES, dh_b)
    g_mat = jnp.asarray(np.arange(page)[:, None] <= (np.arange(page * h_a)[None, :] // h_a), dtype=BF16)

    kc = cache_fox_k.reshape(depth * n_phys, page * h_a, dh_a)
    vc = cache_fox_v.reshape(depth * n_phys, page * h_a, dh_a)
    lf_t = jnp.swapaxes(cache_fox_logf, 2, 3).reshape(depth * n_phys, h_a, page)
    mk2 = cache_mem_k.reshape(depth * nb_s, n_mem * h_x, dh_x)
    mv2 = cache_mem_v.reshape(depth * nb_s, n_mem * h_x, dh_x)

    xp = x_prompt.reshape(nb_p * t_p, d_model)
    xs = x_sample.reshape(nb_s, d_model)
    memf = mem_prompt.reshape(nb_p * n_mem, d_model)
    row1 = lambda a: a.reshape(1, -1).astype(F32)
    pad_cols = lambda a, w: jnp.pad(a, ((0, 0), (0, w - a.shape[1])))

    d_ff = w_ff1.shape[-1]
    stacked_bf16 = [w.astype(BF16) for w in
                    (w_out, w_xq, jnp.concatenate([w_xk, w_xv], axis=2), w_xo, w_ff1, w_ff2)]
    wq, wk, wv, wf, wzb, wxc, wgc = jnp.split(w_in.astype(BF16), splits, axis=2)
    w_in_all = jnp.concatenate(
        [wzb, wf, jnp.zeros((depth, d_model, zb_w - n_b_cols - h_a), BF16), wq, wk, wv, wxc, wgc], axis=2)
    p_states, s_states = [], []
    kv5 = None
    wkv_s_all = None
    for l in range(depth):
        w_in_r = (w_in_all, l)
        bf_pad = pad_cols(row1(fox_bf[l]), LANES)
        mu_pad = pad_cols(row1(rw_mu[l]), zb_w)
        zrow = lambda r0, w, rows: jnp.pad(w, ((r0, lr_w - r0 - rows), (0, 0))).astype(BF16)
        w2p = zrow(0, rw_w2[l], r_dec)
        a2p = zrow(r_dec, rw_a2[l], r_icl)
        g2p = zrow(r_dec + r_icl, rw_g2[l], r_gate)
        wa_d = _block_diag(rg_wa[l]).astype(BF16)
        wx_d = _block_diag(rg_wx[l]).astype(BF16)
        w_out_b, w_xq_b, w_xkv_b, w_xo_b, w_ff1_b, w_ff2_b = [(w, l) for w in stacked_bf16]
        rwkv_par = dict(zb_w=zb_w, d_b=d_b, mu=mu_pad, w0=row1(rw_w0[l]), a0=row1(rw_a0[l]), w2p=w2p, a2p=a2p,
                        g2p=g2p, kk=row1(rw_kk[l]), ka=row1(rw_ka[l]), ones_h=ones_h)
        post_par = dict(lnx_w=row1(rw_lnx_w[l]), lnx_b=row1(rw_lnx_b[l]), rk=row1(rw_rk[l]), ones_h=ones_h, dh=dh_b)
        rg_par = dict(d_c=d_c, xc_blk0=xc_blk0, conv_w_arr=rg_conv_w[l], conv_b=rg_conv_b[l], wa_d=wa_d, wx_d=wx_d,
                      ba=rg_ba[l], bx=rg_bx[l], lam=rg_lambda[l])

        def tail(x, ya, yb, yc, attend, tm, tm_wide):
            segs = [(w_out_b, d_a, 0), (w_out_b, d_b, d_a // d_b), (w_out_b, d_c, (d_a + d_b) // d_c)]
            x = _matmul([ya, yb, yc], segs, residual=x, tm=tm_wide, tn=1024, name="mix_out_proj")
            q = _matmul([x], [(w_xq_b, d_model, 0)], gain=norm_x[l], tm=tm_wide, name="mem_q_proj")
            o = attend(q)
            x = _matmul([o], [(w_xo_b, d_x, 0)], residual=x, tm=tm_wide, tn=1024, name="mem_out_proj")
            hid = _matmul([x], [(w_ff1_b, d_model, 0)], gain=norm_ff[l], epilogue="relu2", out_dtype=BF16,
                          tm=tm_wide, tn=1024, name="ff_up")
            return _matmul([hid], [(w_ff2_b, d_ff, 0)], residual=x, tm=tm, name="ff_down")

        m_p = nb_p * t_p
        mkv = _matmul([memf], [(w_xkv_b, d_model, 0)], gain=norm_mem[l], tm=512, name="mem_kv_proj")
        z = _matmul([xp], [(w_in_r, d_model, 0)], gain=norm_mix[l], tm=1024, tn=1024, name="mix_in_proj")
        lf, c_t = _logf_cumsum(z, fa_blk, bf_pad, nb_p, t_p)
        c_row = c_t[:, :h_a, :].reshape(nb_p * h_a, t_p)
        ya = _fox_prompt(z, c_row, n=nb_p, t=t_p, n_heads=h_a, dh=dh_a, q_blk0=q_blk0)
        r, lw, km, v, kn, bt, g = _rwkv_prep(z, None, seq_len=t_p, **rwkv_par)
        y, h_pair = _rwkv_chunk(r, lw, km, v, kn, bt, n=nb_p, t=t_p, d_b=d_b, dh=dh_b)
        yb = _rwkv_post(y, r, km, v, g, **post_par)
        yc, conv1, h1 = _rglru_seq(z, n=nb_p, t=t_p, **rg_par)
        xp = tail(xp, ya, yb, yc,
                  lambda q: _mem_attn(q, mkv, n=nb_p, t=t_p, n_mem=n_mem, n_heads=h_x, dh=dh_x), 512, 1024)
        z3 = z.reshape(nb_p, t_p, -1)
        hp = h_pair.reshape(nb_p, d_b // LANES, 2, dh_b, 2, dh_b)
        wkv = jnp.stack([hp[:, :, 0, :, 0, :], hp[:, :, 1, :, 1, :]], axis=2)
        wkv = jnp.swapaxes(wkv, -1, -2).reshape(nb_p, h_b, dh_b, dh_b)
        kv5 = _kv_cache_rows(z, kv5, layer=l, depth=depth, n=nb_p, t=t_p, n_heads=h_a, dh=dh_a,
                             k_blk0=(zb_w + d_a) // LANES)
        p_states.append((
            lf.reshape(nb_p, t_p, LANES)[:, :, :h_a],
            z3[:, t_p - 1, :n_b_cols],
            wkv,
            conv1,
            h1.reshape(nb_p, d_c),
            mkv[:, :d_x].reshape(nb_p, n_mem, h_x, dh_x),
            mkv[:, d_x:].reshape(nb_p, n_mem, h_x, dh_x),
        ))

        zs = _matmul([xs], [(w_in_r, d_model, 0)], gain=norm_mix[l], tm=nb_s, name="mix_in_proj_s")
        lf_s, _ = _logf_cumsum(zs, fa_blk, bf_pad, 1, nb_s)
        hd = lambda a: a.reshape(nb_s, h_a, dh_a)
        q_s = hd(zs[:, zb_w:zb_w + d_a])
        k_s = hd(zs[:, zb_w + d_a:zb_w + 2 * d_a])
        v_s = hd(zs[:, zb_w + 2 * d_a:zb_w + 3 * d_a])
        lfn = jnp.broadcast_to(lf_s[:, :h_a, None], (nb_s, h_a, dh_a))
        ya_s = _fox_decode(page_table, q_s, k_s, v_s, lfn, kc, vc, lf_t, g_mat, layer=l, n_phys=n_phys)
        ya_s = ya_s.reshape(nb_s, d_a).astype(BF16)
        zprev = pad_cols(state_rwkv_shift[l], zb_w)
        r, lw, km, v, kn, bt, g = _rwkv_prep(zs, zprev, seq_len=1, **rwkv_par)
        y, wkv_s_all = _rwkv_step(r, lw, km, v, kn, bt, state_rwkv_wkv.reshape(depth * nb_s, h_b, dh_b, dh_b),
                                  ones_pair, wkv_s_all, dh=dh_b, layer=l)
        yb_s = _rwkv_post(y, r, km, v, g, **post_par)
        yc_s, conv1_s, h1_s = _rglru_step(zs, state_rglru_conv[l], state_rglru_h[l], **rg_par)
        xs = tail(xs, ya_s, yb_s, yc_s,
                  lambda q: _mem_decode(q.reshape(nb_s, h_x, dh_x), mk2, mv2, layer=l).reshape(nb_s, d_x),
                  nb_s, nb_s)
        s_states.append((
            k_s.reshape(nb_s, 1, h_a, dh_a),
            v_s.reshape(nb_s, 1, h_a, dh_a),
            lf_s[:, :h_a].reshape(nb_s, 1, h_a),
            zs[:, :n_b_cols],
            conv1_s,
            h1_s,
        ))

    y_prompt = _rmsnorm(xp, norm_f).reshape(nb_p, t_p, d_model)
    y_sample = _rmsnorm(xs, norm_f).reshape(nb_s, 1, d_model)
    p_out = [jnp.stack(s) for s in zip(*p_states)]
    s_out = [jnp.stack(s) for s in zip(*s_states)]
    fox_k_p, fox_v_p = [a.reshape(depth, nb_p, t_p, h_a, dh_a) for a in kv5]
    rwkv_wkv_s = wkv_s_all.reshape(depth, nb_s, h_b, dh_b, dh_b)
    return (y_prompt, y_sample, fox_k_p, fox_v_p, *p_out, *s_out[:4], rwkv_wkv_s, *s_out[4:])
```
